```python
import jax
import jax.numpy as jnp
from jax import lax
import numpy as np

D_MODEL = 1024
BATCH = 16
SEQ = 4096
DEPTH = 2
DEC_BATCH = 32
DEC_SEQ = 16
PAST_LEN = 2048

CHUNK = 64
Q_BLOCK = 128
EPS = 1e-6
NEG_INF = -1e30

LRU_WIDTH = D_MODEL
LRU_BLOCKS = 8
LRU_BLOCK_W = LRU_WIDTH // LRU_BLOCKS
CONV_W = 4
LRU_C = 8.0

MLA_HEADS = 8
QK_NOPE = 128
QK_ROPE = 64
V_DIM = 128
Q_RANK = 768
KV_RANK = 256
MLA_WIDTH = MLA_HEADS * V_DIM
ROPE_THETA = 10000.0
SOFTMAX_SCALE = (QK_NOPE + QK_ROPE) ** -0.5

IN_COLS = (LRU_WIDTH, LRU_WIDTH, Q_RANK, KV_RANK, QK_ROPE, MLA_WIDTH, D_MODEL, D_MODEL)
IN_DIM = LRU_WIDTH * 2 + Q_RANK + KV_RANK + QK_ROPE + MLA_WIDTH + D_MODEL * 2

kernel_name = 'hybrid_rglru_mla_stream_step'


def _rmsnorm(x, g):
    x32 = x.astype(jnp.float32)
    y = x32 * lax.rsqrt(jnp.mean(x32 * x32, axis=-1, keepdims=True) + EPS)
    return (y * g.astype(jnp.float32)).astype(x.dtype)


def _split_cols(z):
    outs = []
    start = 0
    for w in IN_COLS:
        outs.append(z[..., start:start + w])
        start += w
    return outs


def _rope(x, pos):
    half = QK_ROPE // 2
    inv = ROPE_THETA ** (-jnp.arange(half, dtype=jnp.float32) / half)
    ang = pos.astype(jnp.float32)[:, None] * inv[None, :]
    cos = jnp.cos(ang)[None, :, None, :]
    sin = jnp.sin(ang)[None, :, None, :]
    x32 = x.astype(jnp.float32)
    x1, x2 = x32[..., :half], x32[..., half:]
    return jnp.concatenate([x1 * cos - x2 * sin, x1 * sin + x2 * cos], axis=-1).astype(x.dtype)


def _rglru(xc, h0, wa, ba, wx, bx, lam):
    B, S, W = xc.shape
    xb = xc.reshape(B, S, LRU_BLOCKS, LRU_BLOCK_W)
    r = jax.nn.sigmoid(jnp.einsum('bsnk,nkj->bsnj', xb, wa).reshape(B, S, W) + ba)
    i = jax.nn.sigmoid(jnp.einsum('bsnk,nkj->bsnj', xb, wx).reshape(B, S, W) + bx)
    log_a = (-LRU_C * r.astype(jnp.float32)) * jax.nn.softplus(-lam.astype(jnp.float32))
    a = jnp.exp(log_a)
    mult = jnp.sqrt(-jnp.expm1(2.0 * log_a))
    b = mult * (i * xc).astype(jnp.float32)
    b = b.at[:, 0].add(a[:, 0] * h0.astype(jnp.float32))

    def combine(left, right):
        a_l, b_l = left
        a_r, b_r = right
        return a_l * a_r, a_r * b_l + b_r

    _, hs = lax.associative_scan(combine, (a, b), axis=1)
    return hs.astype(xc.dtype), hs[:, -1].astype(xc.dtype)


def _mla_attention(q_nope, q_pe, ckv, kpe, q_pos, k_pos, w_uk, w_uv):
    B, Sq = q_nope.shape[0], q_nope.shape[1]
    blk = min(Q_BLOCK, Sq)
    nb = Sq // blk
    k_chunk = k_pos // CHUNK

    def to_blocks(t):
        return t.reshape((B, nb, blk) + t.shape[2:]).swapaxes(0, 1)

    def attend(args):
        qn, qp, qpos = args
        q_lat = jnp.einsum('bqhn,rhn->bqhr', qn, w_uk)
        s = jnp.einsum('bqhr,bkr->bhqk', q_lat, ckv) + jnp.einsum('bqhp,bkp->bhqk', qp, kpe)
        s = s.astype(jnp.float32) * SOFTMAX_SCALE
        mask = k_chunk[None, :] <= (qpos // CHUNK)[:, None]
        s = jnp.where(mask, s, NEG_INF)
        pr = jax.nn.softmax(s, axis=-1).astype(ckv.dtype)
        o_lat = jnp.einsum('bhqk,bkr->bqhr', pr, ckv)
        return jnp.einsum('bqhr,rhv->bqhv', o_lat, w_uv)

    out = lax.map(attend, (to_blocks(q_nope), to_blocks(q_pe), q_pos.reshape(nb, blk)))
    return out.swapaxes(0, 1).reshape(B, Sq, MLA_HEADS, V_DIM)


def _layer(x, c, pos, k_pos_past, ckv_past, kpe_past, conv_state, h0, lp):
    B, S, _ = x.shape
    mod = jnp.einsum('bd,de->be', jax.nn.silu(c), lp['ada_w']) + lp['ada_b']
    shift, scale, gate = jnp.split(mod[:, None, :], 3, axis=-1)
    h = _rmsnorm(x, lp['pre_norm']) * (1.0 + scale) + shift
    z = jnp.einsum('bsd,de->bse', h, lp['w_in'])
    xa, ga, cq, ckv, kpe, gb, ua, ub = _split_cols(z)

    conv_in = jnp.concatenate([conv_state.astype(xa.dtype), xa], axis=1)
    xc = lp['conv_b'] + conv_in[:, 0:S] * lp['conv_w'][0]
    for k in range(1, CONV_W):
        xc = xc + conv_in[:, k:k + S] * lp['conv_w'][k]
    new_conv = conv_in[:, S:]
    y_lru, h_last = _rglru(xc, h0, lp['lru_wa'], lp['lru_ba'], lp['lru_wx'], lp['lru_bx'], lp['lru_lambda'])
    ya = y_lru * jax.nn.silu(ga)

    q = jnp.einsum('bsr,re->bse', _rmsnorm(cq, lp['q_norm']), lp['w_q_up'])
    q = q.reshape(B, S, MLA_HEADS, QK_NOPE + QK_ROPE)
    q_nope = q[..., :QK_NOPE]
    q_pe = _rope(q[..., QK_NOPE:], pos)
    ckv = _rmsnorm(ckv, lp['kv_norm'])
    kpe = _rope(kpe[:, :, None, :], pos)[:, :, 0, :]
    if ckv_past is None:
        ckv_all, kpe_all, k_pos = ckv, kpe, pos
    else:
        ckv_all = jnp.concatenate([ckv_past.astype(ckv.dtype), ckv], axis=1)
        kpe_all = jnp.concatenate([kpe_past.astype(kpe.dtype), kpe], axis=1)
        k_pos = jnp.concatenate([k_pos_past, pos])
    attn = _mla_attention(q_nope, q_pe, ckv_all, kpe_all, pos, k_pos, lp['w_uk'], lp['w_uv'])
    yb = attn.reshape(B, S, MLA_WIDTH) * jax.nn.silu(gb)

    merged = (jax.nn.sigmoid(ua) * jnp.einsum('bsw,wd->bsd', ya, lp['w_branch_a'])
              + jax.nn.sigmoid(ub) * jnp.einsum('bsw,wd->bsd', yb, lp['w_branch_b']))
    o = jnp.einsum('bsd,de->bse', merged, lp['w_out'])
    x = x + gate * _rmsnorm(o, lp['post_norm'])
    return x, ckv, kpe, new_conv, h_last


def setup_inputs(seed: int = 0) -> dict:
    key = jax.random.key(seed)
    ks = jax.random.split(key, 32)
    f32 = jnp.float32

    def nrm(k, shape, scale):
        return jax.random.normal(k, shape, f32) * scale

    a0 = jax.random.uniform(ks[31], (DEPTH, LRU_WIDTH), f32, 0.9, 0.999)
    p = a0 ** (1.0 / LRU_C)
    lru_lambda = jnp.log(p) - jnp.log1p(-p)

    return {
        'x_prompt': nrm(ks[0], (BATCH, SEQ, D_MODEL), 1.0),
        'x_sample': nrm(ks[1], (DEC_BATCH, DEC_SEQ, D_MODEL), 1.0),
        'c_prompt': nrm(ks[2], (BATCH, D_MODEL), 1.0),
        'c_sample': nrm(ks[3], (DEC_BATCH, D_MODEL), 1.0),
        'cache_ckv': nrm(ks[4], (DEPTH, DEC_BATCH, PAST_LEN, KV_RANK), 1.0),
        'cache_kpe': nrm(ks[5], (DEPTH, DEC_BATCH, PAST_LEN, QK_ROPE), 1.0),
        'state_conv': nrm(ks[6], (DEPTH, DEC_BATCH, CONV_W - 1, LRU_WIDTH), 1.0),
        'state_lru': nrm(ks[7], (DEPTH, DEC_BATCH, LRU_WIDTH), 0.5),
        'ada_w': nrm(ks[8], (DEPTH, D_MODEL, 3 * D_MODEL), 0.5 * D_MODEL ** -0.5),
        'ada_b': nrm(ks[9], (DEPTH, 3 * D_MODEL), 0.02),
        'pre_norm': 1.0 + nrm(ks[10], (DEPTH, D_MODEL), 0.05),
        'post_norm': 1.0 + nrm(ks[11], (DEPTH, D_MODEL), 0.05),
        'w_in': nrm(ks[12], (DEPTH, D_MODEL, IN_DIM), D_MODEL ** -0.5),
        'conv_w': nrm(ks[13], (DEPTH, CONV_W, LRU_WIDTH), 0.5),
        'conv_b': nrm(ks[14], (DEPTH, LRU_WIDTH), 0.02),
        'lru_wa': nrm(ks[15], (DEPTH, LRU_BLOCKS, LRU_BLOCK_W, LRU_BLOCK_W), LRU_BLOCK_W ** -0.5),
        'lru_ba': nrm(ks[16], (DEPTH, LRU_WIDTH), 0.02),
        'lru_wx': nrm(ks[17], (DEPTH, LRU_BLOCKS, LRU_BLOCK_W, LRU_BLOCK_W), LRU_BLOCK_W ** -0.5),
        'lru_bx': nrm(ks[18], (DEPTH, LRU_WIDTH), 0.02),
        'lru_lambda': lru_lambda,
        'q_norm': 1.0 + nrm(ks[19], (DEPTH, Q_RANK), 0.05),
        'w_q_up': nrm(ks[20], (DEPTH, Q_RANK, MLA_HEADS * (QK_NOPE + QK_ROPE)), Q_RANK ** -0.5),
        'kv_norm': 1.0 + nrm(ks[21], (DEPTH, KV_RANK), 0.05),
        'w_uk': nrm(ks[22], (DEPTH, KV_RANK, MLA_HEADS, QK_NOPE), KV_RANK ** -0.5),
        'w_uv': nrm(ks[23], (DEPTH, KV_RANK, MLA_HEADS, V_DIM), KV_RANK ** -0.5),
        'w_branch_a': nrm(ks[24], (DEPTH, LRU_WIDTH, D_MODEL), LRU_WIDTH ** -0.5),
        'w_branch_b': nrm(ks[25], (DEPTH, MLA_WIDTH, D_MODEL), MLA_WIDTH ** -0.5),
        'w_out': nrm(ks[26], (DEPTH, D_MODEL, D_MODEL), D_MODEL ** -0.5),
    }


def reference(x_prompt, x_sample, c_prompt, c_sample, cache_ckv, cache_kpe, state_conv, state_lru,
              ada_w, ada_b, pre_norm, post_norm, w_in, conv_w, conv_b, lru_wa, lru_ba, lru_wx, lru_bx,
              lru_lambda, q_norm, w_q_up, kv_norm, w_uk, w_uv, w_branch_a, w_branch_b, w_out):
    b_p, s_p = x_prompt.shape[0], x_prompt.shape[1]
    past_len = cache_ckv.shape[2]
    s_s = x_sample.shape[1]
    pos_p = jnp.arange(s_p, dtype=jnp.int32)
    pos_s = past_len + jnp.arange(s_s, dtype=jnp.int32)
    k_pos_past = jnp.arange(past_len, dtype=jnp.int32)
    conv0 = jnp.zeros((b_p, CONV_W - 1, LRU_WIDTH), x_prompt.dtype)
    h0 = jnp.zeros((b_p, LRU_WIDTH), x_prompt.dtype)

    yp, ys = x_prompt, x_sample
    ckv_p, kpe_p, conv_p, lru_p = [], [], [], []
    ckv_s, kpe_s, conv_s, lru_s = [], [], [], []
    for l in range(DEPTH):
        lp = {
            'ada_w': ada_w[l], 'ada_b': ada_b[l], 'pre_norm': pre_norm[l], 'post_norm': post_norm[l],
            'w_in': w_in[l], 'conv_w': conv_w[l], 'conv_b': conv_b[l],
            'lru_wa': lru_wa[l], 'lru_ba': lru_ba[l], 'lru_wx': lru_wx[l], 'lru_bx': lru_bx[l],
            'lru_lambda': lru_lambda[l], 'q_norm': q_norm[l], 'w_q_up': w_q_up[l], 'kv_norm': kv_norm[l],
            'w_uk': w_uk[l], 'w_uv': w_uv[l], 'w_branch_a': w_branch_a[l], 'w_branch_b': w_branch_b[l],
            'w_out': w_out[l],
        }
        yp, a1, a2, a3, a4 = _layer(yp, c_prompt, pos_p, None, None, None, conv0, h0, lp)
        ckv_p.append(a1); kpe_p.append(a2); conv_p.append(a3); lru_p.append(a4)
        ys, b1, b2, b3, b4 = _layer(ys, c_sample, pos_s, k_pos_past, cache_ckv[l], cache_kpe[l],
                                    state_conv[l], state_lru[l], lp)
        ckv_s.append(b1); kpe_s.append(b2); conv_s.append(b3); lru_s.append(b4)

    new_ckv_prompt = jnp.stack(ckv_p)
    new_kpe_prompt = jnp.stack(kpe_p)
    new_conv_prompt = jnp.stack(conv_p)
    new_lru_prompt = jnp.stack(lru_p)
    new_ckv_sample = jnp.stack(ckv_s)
    new_kpe_sample = jnp.stack(kpe_s)
    new_conv_sample = jnp.stack(conv_s)
    new_lru_sample = jnp.stack(lru_s)
    return (yp, ys, new_ckv_prompt, new_kpe_prompt, new_conv_prompt, new_lru_prompt,
            new_ckv_sample, new_kpe_sample, new_conv_sample, new_lru_sample)
```

```python
import functools

import jax
import jax.numpy as jnp
from jax import lax
from jax.experimental import pallas as pl
from jax.experimental.pallas import tpu as pltpu

F32 = jnp.float32
BF16 = jnp.bfloat16

EPS = 1e-6
NEG_INF = -1e30
CHUNK = 64
LRU_C = 8.0
ROPE_THETA = 10000.0

LANES = 128
SUBLANES = 8
VMEM_LIMIT_BYTES = 56 * 1024 * 1024

CONV_W = 4
N_HEADS = 8
QK_NOPE = 128
QK_ROPE = 64
V_DIM = 128
KV_RANK = 256
Q_RANK = 768
LRU_BLOCKS = 8
K_CAT = KV_RANK + LANES


def _sigmoid(x):
    return jax.nn.sigmoid(x)


def _silu(x):
    return x * _sigmoid(x)


def _rms(x, g):
    return x * lax.rsqrt(jnp.mean(x * x, axis=-1, keepdims=True) + EPS) * g


def _const_spec(shape):
    nd = len(shape)
    return pl.BlockSpec(shape, lambda *_: (0,) * nd, pipeline_mode=pl.Buffered(1))


def _mod_kernel(c_ref, w_ref, b_ref, o_ref):
    c = c_ref[...]
    o_ref[0] = jnp.dot(_silu(c), w_ref[0], preferred_element_type=F32,
                       precision=lax.Precision.HIGHEST) + b_ref[0]


def _modulation(c_all, ada_w, ada_b):
    depth, d, d3 = ada_w.shape
    n = c_all.shape[0]
    nblk = d3 // d
    return pl.pallas_call(
        _mod_kernel,
        grid=(depth, nblk),
        in_specs=[pl.BlockSpec((n, d), lambda l, j: (0, 0)),
                  pl.BlockSpec((1, d, d), lambda l, j: (l, 0, j)),
                  pl.BlockSpec((1, 1, d), lambda l, j: (l, 0, j))],
        out_specs=pl.BlockSpec((1, n, d), lambda l, j: (l, 0, j)),
        out_shape=jax.ShapeDtypeStruct((depth, n, d3), F32),
        name="adaln_mod",
    )(c_all, ada_w, ada_b.reshape(depth, 1, d3))


def _inproj_kernel(x_ref, mod_ref, conv0_ref, h0_ref, cos_ref, sin_ref,
                   w_in_ref, w_q_ref, w_uk_ref, w_gate_ref,
                   pre_norm_ref, conv_w_ref, conv_b_ref, ba_ref, bx_ref, lam_ref, qn_ref, kvn_ref,
                   ya_ref, sgb_ref, sua_ref, sub_ref, q_ref, ckv_ref, kpe_ref, kcat_ref,
                   convo_ref, hlast_ref,
                   buf_ref, hcar_ref):
    bb, t, d = x_ref.shape
    m = bb * t
    groups = t // SUBLANES
    gw = d // LRU_BLOCKS

    @pl.when(pl.program_id(1) == 0)
    def _():
        buf_ref[:, 0:SUBLANES, :] = conv0_ref[...]
        hcar_ref[...] = h0_ref[...]

    x = x_ref[...]
    mod = mod_ref[...]
    shift = mod[:, :, 0:d]
    scale = mod[:, :, d:2 * d]
    h = _rms(x, pre_norm_ref[...]) * (1.0 + scale) + shift
    hb = h.reshape(m, d).astype(BF16)

    col = [0]

    def proj(width):
        lo = col[0]
        col[0] = lo + width
        return jnp.dot(hb, w_in_ref[:, lo:lo + width], preferred_element_type=F32)

    xa = proj(d)
    buf_ref[:, SUBLANES:SUBLANES + t, :] = xa.reshape(bb, t, d)
    cw = conv_w_ref[...]
    xc = conv_b_ref[...] + buf_ref[:, SUBLANES - 3:SUBLANES - 3 + t, :] * cw[0:1]
    xc = xc + buf_ref[:, SUBLANES - 2:SUBLANES - 2 + t, :] * cw[1:2]
    xc = xc + buf_ref[:, SUBLANES - 1:SUBLANES - 1 + t, :] * cw[2:3]
    xc = xc + buf_ref[:, SUBLANES:SUBLANES + t, :] * cw[3:4]
    tail = buf_ref[:, t:t + SUBLANES, :]
    convo_ref[...] = tail
    buf_ref[:, 0:SUBLANES, :] = tail

    xc2 = xc.reshape(m, d)
    xcb = xc2.astype(BF16)
    r_parts, i_parts = [], []
    for n in range(LRU_BLOCKS):
        g = jnp.dot(xcb[:, n * gw:(n + 1) * gw], w_gate_ref[n], preferred_element_type=F32)
        r_parts.append(g[:, 0:gw])
        i_parts.append(g[:, gw:2 * gw])
    r = _sigmoid(jnp.concatenate(r_parts, axis=1) + ba_ref[...])
    ig = _sigmoid(jnp.concatenate(i_parts, axis=1) + bx_ref[...])
    nl = -lam_ref[...]
    softplus = jnp.maximum(nl, 0.0) + jnp.log1p(jnp.exp(-jnp.abs(nl)))
    log_a = (-LRU_C * r) * softplus
    a = jnp.exp(log_a)
    th = jnp.tanh(log_a)
    mult = jnp.sqrt(2.0 * th / (th - 1.0))
    b = mult * (ig * xc2)

    a4 = a.reshape(bb * groups, SUBLANES, d)
    b4 = b.reshape(bb * groups, SUBLANES, d)
    row = lax.broadcasted_iota(jnp.int32, a4.shape, 1)
    for sft in (1, 2, 4):
        keep = row >= sft
        a_sh = jnp.where(keep, pltpu.roll(a4, sft, axis=1), 1.0)
        b_sh = jnp.where(keep, pltpu.roll(b4, sft, axis=1), 0.0)
        b4 = b4 + a4 * b_sh
        a4 = a4 * a_sh
    a5 = a4.reshape(bb, groups, SUBLANES, d)
    b5 = b4.reshape(bb, groups, SUBLANES, d)
    hc = hcar_ref[...]
    outs = []
    for g in range(groups):
        hs = a5[:, g] * hc + b5[:, g]
        outs.append(hs)
        hc = hs[:, SUBLANES - 1:SUBLANES, :]
    hcar_ref[...] = hc
    hlast_ref[...] = hc
    y_lru = jnp.concatenate(outs, axis=1).reshape(m, d)

    ga = proj(d)
    ya_ref[...] = (y_lru * _silu(ga)).astype(BF16).reshape(bb, t, d)

    cos = cos_ref[...][None]
    sin = sin_ref[...][None]
    cq = proj(Q_RANK)
    cqn = _rms(cq, qn_ref[...]).astype(BF16)
    nq = N_HEADS * QK_NOPE
    npe = N_HEADS * LANES
    q_nope = jnp.dot(cqn, w_q_ref[:, 0:nq], preferred_element_type=F32).astype(BF16)
    q_pe = jnp.dot(cqn, w_q_ref[:, nq:nq + npe], preferred_element_type=F32)
    q_rot = jnp.dot(cqn, w_q_ref[:, nq + npe:nq + 2 * npe], preferred_element_type=F32)
    for hd in range(N_HEADS):
        q_lat = jnp.dot(q_nope[:, hd * QK_NOPE:(hd + 1) * QK_NOPE], w_uk_ref[hd],
                        preferred_element_type=F32)
        q_ref[:, hd, :, 0:KV_RANK] = q_lat.astype(BF16).reshape(bb, t, KV_RANK)
        sl = slice(hd * LANES, (hd + 1) * LANES)
        qr = q_pe[:, sl].reshape(bb, t, LANES) * cos + q_rot[:, sl].reshape(bb, t, LANES) * sin
        q_ref[:, hd, :, KV_RANK:K_CAT] = qr.astype(BF16)

    ckv = _rms(proj(KV_RANK), kvn_ref[...])
    ckv_ref[...] = ckv.reshape(bb, t, KV_RANK)
    kcat_ref[:, :, 0:KV_RANK] = ckv.astype(BF16).reshape(bb, t, KV_RANK)
    kp = proj(LANES).reshape(bb, t, LANES)
    kr = proj(LANES).reshape(bb, t, LANES)
    kroped = kp * cos + kr * sin
    kpe_ref[...] = kroped[:, :, 0:QK_ROPE]
    kcat_ref[:, :, KV_RANK:K_CAT] = kroped.astype(BF16)

    sgb_ref[...] = _silu(proj(d)).astype(BF16).reshape(bb, t, d)
    sua_ref[...] = _sigmoid(proj(d)).astype(BF16).reshape(bb, t, d)
    sub_ref[...] = _sigmoid(proj(d)).astype(BF16).reshape(bb, t, d)


def _inproj(x, mod, conv0, h0, cos, sin, lw, *, bb, t):
    b, s, d = x.shape
    grid = (b // bb, s // t)
    tok = lambda w: pl.BlockSpec((bb, t, w), lambda i, j: (i, j, 0))
    per_b = lambda r, w: pl.BlockSpec((bb, r, w), lambda i, j: (i, 0, 0))
    weights = [lw["w_in"], lw["w_q"], lw["w_uk"], lw["w_gate"], lw["pre_norm"], lw["conv_w"],
               lw["conv_b"], lw["lru_ba"], lw["lru_bx"], lw["lru_lambda"], lw["q_norm"], lw["kv_norm"]]
    in_specs = ([tok(d), per_b(1, 3 * d), per_b(SUBLANES, d), per_b(1, d),
                 pl.BlockSpec((t, LANES), lambda i, j: (j, 0)),
                 pl.BlockSpec((t, LANES), lambda i, j: (j, 0))]
                + [_const_spec(w.shape) for w in weights])
    out_shape = [jax.ShapeDtypeStruct((b, s, d), BF16)] * 4 + [
        jax.ShapeDtypeStruct((b, N_HEADS, s, K_CAT), BF16),
        jax.ShapeDtypeStruct((b, s, KV_RANK), F32),
        jax.ShapeDtypeStruct((b, s, QK_ROPE), F32),
        jax.ShapeDtypeStruct((b, s, K_CAT), BF16),
        jax.ShapeDtypeStruct((b, SUBLANES, d), F32),
        jax.ShapeDtypeStruct((b, 1, d), F32),
    ]
    out_specs = [tok(d)] * 4 + [
        pl.BlockSpec((bb, N_HEADS, t, K_CAT), lambda i, j: (i, 0, j, 0)),
        tok(KV_RANK), tok(QK_ROPE), tok(K_CAT), per_b(SUBLANES, d), per_b(1, d),
    ]
    return pl.pallas_call(
        _inproj_kernel,
        grid=grid,
        in_specs=in_specs,
        out_specs=out_specs,
        out_shape=out_shape,
        scratch_shapes=[pltpu.VMEM((bb, t + SUBLANES, d), F32), pltpu.VMEM((bb, 1, d), F32)],
        compiler_params=pltpu.CompilerParams(
            dimension_semantics=("parallel", "arbitrary"), vmem_limit_bytes=VMEM_LIMIT_BYTES),
        name="inproj_lru",
    )(x, mod, conv0, h0, cos, sin, *weights)


def _attn_kernel(q_ref, k_ref, sgb_ref, w_uv_ref, yb_ref, m_scr, l_scr, acc_scr,
                 *, q_off, sk_valid, tq, tk, sm_scale):
    m = N_HEADS * tq
    q = q_ref[0].reshape(m, K_CAT)
    q_first = q_off + pl.program_id(1) * tq
    vis_first = jnp.minimum((q_first // CHUNK + 1) * CHUNK, sk_valid)
    vis_last = jnp.minimum(((q_first + tq - 1) // CHUNK + 1) * CHUNK, sk_valid)
    n_full = vis_first // tk
    n_tiles = (vis_last + tk - 1) // tk

    m_scr[...] = jnp.full(m_scr.shape, NEG_INF, F32)
    l_scr[...] = jnp.zeros(l_scr.shape, F32)
    acc_scr[...] = jnp.zeros(acc_scr.shape, F32)

    def step(j, masked):
        k = k_ref[0, pl.ds(pl.multiple_of(j * tk, tk), tk), :]
        s = lax.dot_general(q, k, (((1,), (1,)), ((), ())), preferred_element_type=F32) * sm_scale
        if masked:
            tok = lax.broadcasted_iota(jnp.int32, (m, tk), 0) & (tq - 1)
            k_lim = jnp.minimum(((q_first + tok) // CHUNK + 1) * CHUNK, sk_valid)
            k_pos = j * tk + lax.broadcasted_iota(jnp.int32, (m, tk), 1)
            s = jnp.where(k_pos < k_lim, s, NEG_INF)
        m_prev = m_scr[...]
        m_next = jnp.maximum(m_prev, jnp.max(s, axis=-1, keepdims=True))
        p = jnp.exp(s - jnp.concatenate([m_next] * (tk // LANES), axis=1))
        alpha = jnp.exp(m_prev - m_next)
        l_scr[...] = alpha * l_scr[...] + jnp.sum(p, axis=-1, keepdims=True)
        m_scr[...] = m_next
        pv = jnp.dot(p.astype(BF16), k[:, 0:KV_RANK], preferred_element_type=F32)
        acc_scr[...] = acc_scr[...] * jnp.concatenate([alpha] * (KV_RANK // LANES), axis=1) + pv

    def full_body(j, c):
        step(j, False)
        return c

    def masked_body(j, c):
        step(j, True)
        return c

    lax.fori_loop(0, n_full, full_body, 0)
    lax.fori_loop(n_full, n_tiles, masked_body, 0)

    inv_l = 1.0 / l_scr[...]
    o = (acc_scr[...] * jnp.concatenate([inv_l] * (KV_RANK // LANES), axis=1)).astype(BF16)
    sgb = sgb_ref[0]
    for hd in range(N_HEADS):
        att = jnp.dot(o[hd * tq:(hd + 1) * tq], w_uv_ref[hd], preferred_element_type=F32)
        sl = slice(hd * V_DIM, (hd + 1) * V_DIM)
        yb_ref[0, :, sl] = (att * sgb[:, sl]).astype(BF16)


def _attention(q, kcat, sgb, w_uv, *, q_off, sk_valid, tq, tk):
    b, _, sq, _ = q.shape
    skp = kcat.shape[1]
    d = sgb.shape[-1]
    m = N_HEADS * tq
    assert tq & (tq - 1) == 0 and sq % tq == 0 and skp % tk == 0 and tk % CHUNK == 0
    kern = functools.partial(_attn_kernel, q_off=q_off, sk_valid=sk_valid, tq=tq, tk=tk,
                             sm_scale=(QK_NOPE + QK_ROPE) ** -0.5)
    return pl.pallas_call(
        kern,
        grid=(b, sq // tq),
        in_specs=[pl.BlockSpec((1, N_HEADS, tq, K_CAT), lambda i, j: (i, 0, j, 0)),
                  pl.BlockSpec((1, skp, K_CAT), lambda i, j: (i, 0, 0)),
                  pl.BlockSpec((1, tq, d), lambda i, j: (i, j, 0)),
                  _const_spec(w_uv.shape)],
        out_specs=pl.BlockSpec((1, tq, d), lambda i, j: (i, j, 0)),
        out_shape=jax.ShapeDtypeStruct((b, sq, d), BF16),
        scratch_shapes=[pltpu.VMEM((m, LANES), F32), pltpu.VMEM((m, LANES), F32),
                        pltpu.VMEM((m, KV_RANK), F32)],
        compiler_params=pltpu.CompilerParams(
            dimension_semantics=("parallel", "arbitrary"), vmem_limit_bytes=VMEM_LIMIT_BYTES),
        name="mla_attention",
    )(q, kcat, sgb, w_uv)


def _outproj_kernel(x_ref, mod_ref, ya_ref, yb_ref, sua_ref, sub_ref,
                    w_a_ref, w_b_ref, w_o_ref, post_norm_ref, y_ref):
    bb, t, d = x_ref.shape
    m = bb * t
    ya = ya_ref[...].reshape(m, d)
    yb = yb_ref[...].reshape(m, d)
    pa = jnp.dot(ya, w_a_ref[...], preferred_element_type=F32)
    pb = jnp.dot(yb, w_b_ref[...], preferred_element_type=F32)
    merged = sua_ref[...].reshape(m, d).astype(F32) * pa + sub_ref[...].reshape(m, d).astype(F32) * pb
    o = jnp.dot(merged.astype(BF16), w_o_ref[...], preferred_element_type=F32)
    gate = mod_ref[...][:, :, 2 * d:3 * d]
    y_ref[...] = x_ref[...] + gate * _rms(o, post_norm_ref[...]).reshape(bb, t, d)


def _outproj(x, mod, ya, yb, sua, sub, lw, *, bb, t):
    b, s, d = x.shape
    tok = pl.BlockSpec((bb, t, d), lambda i, j: (i, j, 0))
    weights = [lw["w_branch_a"], lw["w_branch_b"], lw["w_out"], lw["post_norm"]]
    return pl.pallas_call(
        _outproj_kernel,
        grid=(b // bb, s // t),
        in_specs=[tok, pl.BlockSpec((bb, 1, 3 * d), lambda i, j: (i, 0, 0)), tok, tok, tok, tok]
        + [_const_spec(w.shape) for w in weights],
        out_specs=tok,
        out_shape=jax.ShapeDtypeStruct((b, s, d), F32),
        compiler_params=pltpu.CompilerParams(
            dimension_semantics=("parallel", "parallel"), vmem_limit_bytes=VMEM_LIMIT_BYTES),
        name="outproj",
    )(x, mod, ya, yb, sua, sub, *weights)


def _rot_half(w):
    half = w.shape[-1] // 2
    return jnp.concatenate([-w[..., half:], w[..., :half]], axis=-1)


def _pad_lanes(w):
    return jnp.concatenate([w, jnp.zeros(w.shape[:-1] + (LANES - w.shape[-1],), w.dtype)], axis=-1)


def _layer_weights(l, p):
    d = p["w_in"].shape[1]
    w_in = p["w_in"][l]
    o = 0
    seg = {}
    for name, width in (("xa", d), ("ga", d), ("cq", Q_RANK), ("ckv", KV_RANK), ("kpe", QK_ROPE),
                        ("gb", d), ("ua", d), ("ub", d)):
        seg[name] = w_in[:, o:o + width]
        o += width
    w_in_ext = jnp.concatenate(
        [seg["xa"], seg["ga"], seg["cq"], seg["ckv"], _pad_lanes(seg["kpe"]),
         _pad_lanes(_rot_half(seg["kpe"])), seg["gb"], seg["ua"], seg["ub"]], axis=1)
    wq = p["w_q_up"][l].reshape(Q_RANK, N_HEADS, QK_NOPE + QK_ROPE)
    wq_pe = wq[:, :, QK_NOPE:]
    w_q = jnp.concatenate(
        [wq[:, :, :QK_NOPE].reshape(Q_RANK, -1), _pad_lanes(wq_pe).reshape(Q_RANK, -1),
         _pad_lanes(_rot_half(wq_pe)).reshape(Q_RANK, -1)], axis=1)
    row = lambda v: v.reshape(1, -1)
    return {
        "w_in": w_in_ext.astype(BF16),
        "w_q": w_q.astype(BF16),
        "w_uk": jnp.transpose(p["w_uk"][l], (1, 2, 0)).astype(BF16),
        "w_uv": jnp.transpose(p["w_uv"][l], (1, 0, 2)).astype(BF16),
        "w_gate": jnp.concatenate([p["lru_wa"][l], p["lru_wx"][l]], axis=-1).astype(BF16),
        "pre_norm": row(p["pre_norm"][l]), "post_norm": row(p["post_norm"][l]),
        "conv_w": p["conv_w"][l], "conv_b": row(p["conv_b"][l]),
        "lru_ba": row(p["lru_ba"][l]), "lru_bx": row(p["lru_bx"][l]),
        "lru_lambda": row(p["lru_lambda"][l]),
        "q_norm": row(p["q_norm"][l]), "kv_norm": row(p["kv_norm"][l]),
        "w_branch_a": p["w_branch_a"][l].astype(BF16), "w_branch_b": p["w_branch_b"][l].astype(BF16),
        "w_out": p["w_out"][l].astype(BF16),
    }


def _rope_tables(pos):
    half = QK_ROPE // 2
    inv = ROPE_THETA ** (-jnp.arange(half, dtype=F32) / half)
    ang = pos.astype(F32)[:, None] * inv[None, :]
    cos, sin = jnp.cos(ang), jnp.sin(ang)
    return _pad_lanes(jnp.concatenate([cos, cos], axis=1)), _pad_lanes(jnp.concatenate([sin, sin], axis=1))


def _pick_tile(n, target):
    t = min(n, target)
    while n % t:
        t //= 2
    return t


def _group_layer(x, mod, conv0, h0, cos, sin, kpast, lw, *, q_off, bb, t_in, t_out, tq, tk):
    ya, sgb, sua, sub, q, ckv, kpe, kcat, convo, hlast = _inproj(
        x, mod, conv0, h0, cos, sin, lw, bb=bb, t=t_in)
    if kpast is not None:
        kcat = jnp.concatenate([kpast, kcat], axis=1)
    sk = kcat.shape[1]
    skp = -(-sk // tk) * tk
    if skp != sk:
        kcat = jnp.pad(kcat, ((0, 0), (0, skp - sk), (0, 0)))
    yb = _attention(q, kcat, sgb, lw["w_uv"], q_off=q_off, sk_valid=sk, tq=tq, tk=tk)
    y = _outproj(x, mod, ya, yb, sua, sub, lw, bb=bb, t=t_out)
    return y, ckv, kpe, convo[:, SUBLANES - (CONV_W - 1):, :], hlast[:, 0, :]


def kernel(x_prompt, x_sample, c_prompt, c_sample, cache_ckv, cache_kpe, state_conv, state_lru, ada_w, ada_b, pre_norm, post_norm, w_in, conv_w, conv_b, lru_wa, lru_ba, lru_wx, lru_bx, lru_lambda, q_norm, w_q_up, kv_norm, w_uk, w_uv, w_branch_a, w_branch_b, w_out):
    p = dict(ada_w=ada_w, ada_b=ada_b, pre_norm=pre_norm, post_norm=post_norm, w_in=w_in,
             conv_w=conv_w, conv_b=conv_b, lru_wa=lru_wa, lru_ba=lru_ba, lru_wx=lru_wx, lru_bx=lru_bx,
             lru_lambda=lru_lambda, q_norm=q_norm, w_q_up=w_q_up, kv_norm=kv_norm, w_uk=w_uk,
             w_uv=w_uv, w_branch_a=w_branch_a, w_branch_b=w_branch_b, w_out=w_out)
    depth = w_in.shape[0]
    b_p, s_p, d = x_prompt.shape
    b_s, s_s, _ = x_sample.shape
    past_len = cache_ckv.shape[2]
    assert s_p % SUBLANES == 0 and s_s % SUBLANES == 0 and s_p >= SUBLANES and s_s >= SUBLANES

    mods = _modulation(jnp.concatenate([c_prompt, c_sample], axis=0), ada_w, ada_b)
    cos_p, sin_p = _rope_tables(jnp.arange(s_p, dtype=jnp.int32))
    cos_s, sin_s = _rope_tables(past_len + jnp.arange(s_s, dtype=jnp.int32))
    conv0_p = jnp.zeros((b_p, SUBLANES, d), F32)
    h0_p = jnp.zeros((b_p, 1, d), F32)

    t_in_p = _pick_tile(s_p, 256)
    t_out_p = _pick_tile(s_p, 512)
    tq_p = _pick_tile(s_p, 128)
    tk_p = 256
    tk_s = 256

    yp, ys = x_prompt, x_sample
    outs_p, outs_s = [], []
    for l in range(depth):
        lw = _layer_weights(l, p)
        mod_p = mods[l, :b_p, None, :]
        mod_s = mods[l, b_p:, None, :]
        yp, *rest_p = _group_layer(yp, mod_p, conv0_p, h0_p, cos_p, sin_p, None, lw, q_off=0,
                                   bb=1, t_in=t_in_p, t_out=t_out_p, tq=tq_p, tk=tk_p)
        outs_p.append(rest_p)
        conv0_s = jnp.pad(state_conv[l], ((0, 0), (SUBLANES - (CONV_W - 1), 0), (0, 0)))
        kpast = jnp.concatenate(
            [cache_ckv[l].astype(BF16), cache_kpe[l].astype(BF16),
             jnp.zeros((b_s, past_len, LANES - QK_ROPE), BF16)], axis=-1)
        ys, *rest_s = _group_layer(ys, mod_s, conv0_s, state_lru[l][:, None, :], cos_s, sin_s, kpast,
                                   lw, q_off=past_len, bb=b_s, t_in=s_s, t_out=s_s, tq=s_s, tk=tk_s)
        outs_s.append(rest_s)

    stack = lambda outs, k: jnp.stack([o[k] for o in outs])
    return (yp, ys,
            stack(outs_p, 0), stack(outs_p, 1), stack(outs_p, 2), stack(outs_p, 3),
            stack(outs_s, 0), stack(outs_s, 1), stack(outs_s, 2), stack(outs_s, 3))
```

```python
import functools
import math

import jax
import jax.numpy as jnp
from jax import lax
from jax.experimental import pallas as pl
from jax.experimental.pallas import tpu as pltpu

F32 = jnp.float32
BF16 = jnp.bfloat16

EPS = 1e-6
NEG_INF = -1e30
CHUNK = 64
LRU_C = 8.0
ROPE_THETA = 10000.0

LANES = 128
SUBLANES = 8
MXU_DIM = 256
VMEM_LIMIT_BYTES = 56 * 1024 * 1024

CONV_W = 4
N_HEADS = 8
QK_NOPE = 128
QK_ROPE = 64
V_DIM = 128
KV_RANK = 256
Q_RANK = 768
LRU_BLOCKS = 8
K_CAT = KV_RANK + LANES
Q_SCALE = (QK_NOPE + QK_ROPE) ** -0.5 * math.log2(math.e)


def _sigmoid(x):
    return jax.nn.sigmoid(x)


def _silu(x):
    return x * _sigmoid(x)


def _sqrt_nonneg(x):
    return jnp.where(x > 0.0, x * lax.rsqrt(x), 0.0)


def _rms(x, g):
    return x * lax.rsqrt(jnp.mean(x * x, axis=-1, keepdims=True) + EPS) * g


def _const_spec(shape):
    nd = len(shape)
    return pl.BlockSpec(shape, lambda *_: (0,) * nd, pipeline_mode=pl.Buffered(1))


def _mod_kernel(c_ref, w_ref, b_ref, o_ref):
    c = c_ref[...]
    o_ref[0] = jnp.dot(_silu(c), w_ref[0], preferred_element_type=F32,
                       precision=lax.Precision.HIGHEST) + b_ref[0]


def _modulation(c_all, ada_w, ada_b):
    depth, d, d3 = ada_w.shape
    n = c_all.shape[0]
    nblk = d3 // d
    return pl.pallas_call(
        _mod_kernel,
        grid=(depth, nblk),
        in_specs=[pl.BlockSpec((n, d), lambda l, j: (0, 0)),
                  pl.BlockSpec((1, d, d), lambda l, j: (l, 0, j)),
                  pl.BlockSpec((1, 1, d), lambda l, j: (l, 0, j))],
        out_specs=pl.BlockSpec((1, n, d), lambda l, j: (l, 0, j)),
        out_shape=jax.ShapeDtypeStruct((depth, n, d3), F32),
        name="adaln_mod",
    )(c_all, ada_w, ada_b.reshape(depth, 1, d3))


def _inproj_kernel(x_ref, mod_ref, conv0_ref, h0_ref, cos_ref, sin_ref,
                   w_in_ref, w_q_ref, w_uk_ref, w_gate_ref,
                   pre_norm_ref, conv_w_ref, conv_b_ref, ba_ref, bx_ref, lam_ref, qn_ref, kvn_ref,
                   ya_ref, sgb_ref, sua_ref, sub_ref, q_ref, ckv_ref, kpe_ref, kcat_ref,
                   convo_ref, hlast_ref, *rest, emit_vt):
    if emit_vt:
        vt_ref, buf_ref, hcar_ref, hb_ref, cqn_ref = rest
    else:
        buf_ref, hcar_ref, hb_ref, cqn_ref = rest
    bb, t, d = x_ref.shape
    m = bb * t
    groups = t // SUBLANES
    gw = d // LRU_BLOCKS
    cw_ = MXU_DIM
    n_chunks = d // cw_
    blocks_per_chunk = cw_ // gw
    nq = N_HEADS * QK_NOPE
    npe = N_HEADS * QK_ROPE
    o_xa, o_ga, o_cq = 0, d, 2 * d
    o_ckv = o_cq + Q_RANK
    o_kpe = o_ckv + KV_RANK
    o_krot = o_kpe + LANES
    o_gb = o_krot + LANES
    o_ua = o_gb + d
    o_ub = o_ua + d

    @pl.when(pl.program_id(1) == 0)
    def _():
        buf_ref[:, 0:SUBLANES, :] = conv0_ref[...]
        hcar_ref[...] = h0_ref[...]

    x = x_ref[...]
    mod = mod_ref[...]
    shift = mod[:, :, 0:d]
    scale = mod[:, :, d:2 * d]
    h = _rms(x, pre_norm_ref[...]) * (1.0 + scale) + shift
    hb_ref[...] = h.reshape(m, d).astype(BF16)

    def proj(lo, width):
        return jnp.dot(hb_ref[...], w_in_ref[:, lo:lo + width], preferred_element_type=F32)

    cos = cos_ref[...][None]
    sin = sin_ref[...][None]

    def conv_phase(c):
        cs = slice(c * cw_, (c + 1) * cw_)
        xa = proj(o_xa + c * cw_, cw_)
        buf_ref[:, SUBLANES:SUBLANES + t, cs] = xa.reshape(bb, t, cw_)
        cw = conv_w_ref[:, cs]
        xc = conv_b_ref[:, cs] + buf_ref[:, SUBLANES - 3:SUBLANES - 3 + t, cs] * cw[0:1]
        xc = xc + buf_ref[:, SUBLANES - 2:SUBLANES - 2 + t, cs] * cw[1:2]
        xc = xc + buf_ref[:, SUBLANES - 1:SUBLANES - 1 + t, cs] * cw[2:3]
        xc = xc + buf_ref[:, SUBLANES:SUBLANES + t, cs] * cw[3:4]
        tail = buf_ref[:, t:t + SUBLANES, cs]
        convo_ref[:, :, cs] = tail
        buf_ref[:, 0:SUBLANES, cs] = tail
        return xc.reshape(m, cw_)

    def gate_phase(c, xc2):
        cs = slice(c * cw_, (c + 1) * cw_)
        xcb = xc2.astype(BF16)
        r_parts, i_parts = [], []
        for n in range(blocks_per_chunk):
            g = jnp.dot(xcb[:, n * gw:(n + 1) * gw], w_gate_ref[c * blocks_per_chunk + n],
                        preferred_element_type=F32)
            r_parts.append(g[:, 0:gw])
            i_parts.append(g[:, gw:2 * gw])
        r = _sigmoid(jnp.concatenate(r_parts, axis=1) + ba_ref[:, cs])
        ig = _sigmoid(jnp.concatenate(i_parts, axis=1) + bx_ref[:, cs])
        nl = -lam_ref[:, cs]
        softplus = jnp.maximum(nl, 0.0) + jnp.log1p(jnp.exp(-jnp.abs(nl)))
        th = jnp.tanh((-0.5 * LRU_C * r) * softplus)
        inv = 1.0 / (1.0 - th)
        a = (1.0 + th) * inv
        mult = (2.0 * _sqrt_nonneg(-th)) * inv
        return a, mult * (ig * xc2)

    def scan_phase(c, a, b):
        cs = slice(c * cw_, (c + 1) * cw_)
        a4 = a.reshape(bb * groups, SUBLANES, cw_)
        b4 = b.reshape(bb * groups, SUBLANES, cw_)
        row = lax.broadcasted_iota(jnp.int32, a4.shape, 1)
        for sft in (1, 2, 4):
            keep = row >= sft
            a_sh = jnp.where(keep, pltpu.roll(a4, sft, axis=1), 1.0)
            b_sh = jnp.where(keep, pltpu.roll(b4, sft, axis=1), 0.0)
            b4 = b4 + a4 * b_sh
            a4 = a4 * a_sh
        a5 = a4.reshape(bb, groups, SUBLANES, cw_)
        b5 = b4.reshape(bb, groups, SUBLANES, cw_)
        hc = hcar_ref[:, :, cs]
        outs = []
        for g in range(groups):
            hs = a5[:, g] * hc + b5[:, g]
            outs.append(hs)
            hc = hs[:, SUBLANES - 1:SUBLANES, :]
        hcar_ref[:, :, cs] = hc
        hlast_ref[:, :, cs] = hc
        return jnp.concatenate(outs, axis=1).reshape(m, cw_)

    def out_phase(c, y_lru):
        cs = slice(c * cw_, (c + 1) * cw_)
        ga = proj(o_ga + c * cw_, cw_)
        ya_ref[:, :, cs] = (y_lru * _silu(ga)).astype(BF16).reshape(bb, t, cw_)

    def gate_out(ref, fn, off, c):
        cs = slice(c * cw_, (c + 1) * cw_)
        ref[:, :, cs] = fn(proj(off + c * cw_, cw_)).astype(BF16).reshape(bb, t, cw_)

    def q_latent(heads):
        q_nope = jnp.dot(cqn_ref[...], w_q_ref[:, heads[0] * QK_NOPE:(heads[-1] + 1) * QK_NOPE],
                         preferred_element_type=F32).astype(BF16)
        for k, hd in enumerate(heads):
            q_lat = jnp.dot(q_nope[:, k * QK_NOPE:(k + 1) * QK_NOPE], w_uk_ref[hd],
                            preferred_element_type=F32)
            q_ref[:, hd, :, 0:KV_RANK] = q_lat.astype(BF16).reshape(bb, t, KV_RANK)

    def q_rotary():
        q_pe = jnp.dot(cqn_ref[...], w_q_ref[:, nq:nq + npe], preferred_element_type=F32)
        q_rot = jnp.dot(cqn_ref[...], w_q_ref[:, nq + npe:nq + 2 * npe], preferred_element_type=F32)
        lane = lax.broadcasted_iota(jnp.int32, (1, 1, LANES), 2)
        for pair in range(N_HEADS // 2):
            sl = slice(pair * LANES, (pair + 1) * LANES)
            both = q_pe[:, sl].reshape(bb, t, LANES) * cos + q_rot[:, sl].reshape(bb, t, LANES) * sin
            q_ref[:, 2 * pair, :, KV_RANK:K_CAT] = jnp.where(lane < QK_ROPE, both, 0.0).astype(BF16)
            q_ref[:, 2 * pair + 1, :, KV_RANK:K_CAT] = jnp.where(lane < QK_ROPE, 0.0, both).astype(BF16)

    def key_latents():
        ckv = _rms(proj(o_ckv, KV_RANK), kvn_ref[...])
        ckv_ref[...] = ckv.reshape(bb, t, KV_RANK)
        kcat_ref[:, :, 0:KV_RANK] = ckv.astype(BF16).reshape(bb, t, KV_RANK)
        if emit_vt:
            vt_ref[0, 0] = ckv.T.astype(BF16)
        kp = proj(o_kpe, LANES).reshape(bb, t, LANES)
        kr = proj(o_krot, LANES).reshape(bb, t, LANES)
        kroped = kp * cos + kr * sin
        kpe_ref[...] = kroped[:, :, 0:QK_ROPE]
        kcat_ref[:, :, KV_RANK:K_CAT] = kroped.astype(BF16)

    cqn_ref[...] = _rms(proj(o_cq, Q_RANK), qn_ref[...]).astype(BF16)
    other = [lambda: q_latent((0, 1, 2, 3)), lambda: q_latent((4, 5, 6, 7)), q_rotary, key_latents]
    for c in range(n_chunks):
        xc2 = conv_phase(c)
        gate_out(sgb_ref, _silu, o_gb, c)
        a, b = gate_phase(c, xc2)
        gate_out(sua_ref, _sigmoid, o_ua, c)
        y_lru = scan_phase(c, a, b)
        gate_out(sub_ref, _sigmoid, o_ub, c)
        out_phase(c, y_lru)
        if c < len(other):
            other[c]()
    for f in other[n_chunks:]:
        f()


def _inproj(x, mod, conv0, h0, cos, sin, lw, *, bb, t, emit_vt):
    b, s, d = x.shape
    grid = (b // bb, s // t)
    tok = lambda w: pl.BlockSpec((bb, t, w), lambda i, j: (i, j, 0))
    per_b = lambda r, w: pl.BlockSpec((bb, r, w), lambda i, j: (i, 0, 0))
    weights = [lw["w_in"], lw["w_q"], lw["w_uk"], lw["w_gate"], lw["pre_norm"], lw["conv_w"],
               lw["conv_b"], lw["lru_ba"], lw["lru_bx"], lw["lru_lambda"], lw["q_norm"], lw["kv_norm"]]
    in_specs = ([tok(d), per_b(1, 3 * d), per_b(SUBLANES, d), per_b(1, d),
                 pl.BlockSpec((t, LANES), lambda i, j: (j, 0)),
                 pl.BlockSpec((t, LANES), lambda i, j: (j, 0))]
                + [_const_spec(w.shape) for w in weights])
    out_shape = [jax.ShapeDtypeStruct((b, s, d), BF16)] * 4 + [
        jax.ShapeDtypeStruct((b, N_HEADS, s, K_CAT), BF16),
        jax.ShapeDtypeStruct((b, s, KV_RANK), F32),
        jax.ShapeDtypeStruct((b, s, QK_ROPE), F32),
        jax.ShapeDtypeStruct((b, s, K_CAT), BF16),
        jax.ShapeDtypeStruct((b, SUBLANES, d), F32),
        jax.ShapeDtypeStruct((b, 1, d), F32),
    ]
    out_specs = [tok(d)] * 4 + [
        pl.BlockSpec((bb, N_HEADS, t, K_CAT), lambda i, j: (i, 0, j, 0)),
        tok(KV_RANK), tok(QK_ROPE), tok(K_CAT), per_b(SUBLANES, d), per_b(1, d),
    ]
    if emit_vt:
        assert bb == 1
        out_shape.append(jax.ShapeDtypeStruct((b, s // t, KV_RANK, t), BF16))
        out_specs.append(pl.BlockSpec((1, 1, KV_RANK, t), lambda i, j: (i, j, 0, 0)))
    return pl.pallas_call(
        functools.partial(_inproj_kernel, emit_vt=emit_vt),
        grid=grid,
        in_specs=in_specs,
        out_specs=out_specs,
        out_shape=out_shape,
        scratch_shapes=[pltpu.VMEM((bb, t + SUBLANES, d), F32), pltpu.VMEM((bb, 1, d), F32),
                        pltpu.VMEM((bb * t, d), BF16), pltpu.VMEM((bb * t, Q_RANK), BF16)],
        compiler_params=pltpu.CompilerParams(
            dimension_semantics=("parallel", "arbitrary"), vmem_limit_bytes=VMEM_LIMIT_BYTES),
        name="inproj_lru",
    )(x, mod, conv0, h0, cos, sin, *weights)


def _visible_tiles(q_first, tq, tk, sk_valid):
    vis_first = jnp.minimum((q_first // CHUNK + 1) * CHUNK, sk_valid)
    vis_last = jnp.minimum(((q_first + tq - 1) // CHUNK + 1) * CHUNK, sk_valid)
    return vis_first // tk, (vis_last + tk - 1) // tk


def _attn_t_kernel(q_ref, k_ref, vt_ref, sgb_ref, w_uv_ref, yb_ref, m_scr, l_scr, acc_scr,
                   *, q_off, sk_valid, tq, tk):
    m = N_HEADS * tq
    gran = vt_ref.shape[-1]
    gcols = 2 * MXU_DIM
    q_first = q_off + pl.program_id(1) * tq
    n_full, n_tiles = _visible_tiles(q_first, tq, tk, sk_valid)

    m_scr[...] = jnp.full(m_scr.shape, NEG_INF, F32)
    l_scr[...] = jnp.zeros(l_scr.shape, F32)
    acc_scr[...] = jnp.zeros(acc_scr.shape, F32)

    def step(j, masked):
        k = k_ref[0, pl.ds(pl.multiple_of(j * tk, tk), tk), :]
        vt = jnp.concatenate([vt_ref[0, j * (tk // gran) + g] for g in range(tk // gran)], axis=1)
        scores = [lax.dot_general(k, q_ref[0, c0 // tq:(c0 + gcols) // tq].reshape(gcols, K_CAT),
                                  (((1,), (1,)), ((), ())), preferred_element_type=F32)
                  for c0 in range(0, m, gcols)]
        for g, s in enumerate(scores):
            cs = slice(g * gcols, (g + 1) * gcols)
            if masked:
                tok = lax.broadcasted_iota(jnp.int32, (1, gcols), 1) & (tq - 1)
                k_lim = jnp.minimum(((q_first + tok) // CHUNK + 1) * CHUNK, sk_valid)
                k_pos = j * tk + lax.broadcasted_iota(jnp.int32, (tk, gcols), 0)
                s = jnp.where(k_pos < k_lim, s, NEG_INF)
            m_prev = m_scr[:, cs]
            m_next = jnp.maximum(m_prev, jnp.max(s, axis=0, keepdims=True))
            p = jnp.exp2(s - m_next)
            alpha = jnp.exp2(m_prev - m_next)
            l_scr[:, cs] = alpha * l_scr[:, cs] + jnp.sum(p, axis=0, keepdims=True)
            m_scr[:, cs] = m_next
            pv = jnp.dot(vt, p.astype(BF16), preferred_element_type=F32)
            acc_scr[:, cs] = acc_scr[:, cs] * alpha + pv

    def full_body(j, c):
        step(j, False)
        return c

    def masked_body(j, c):
        step(j, True)
        return c

    lax.fori_loop(0, n_full, full_body, 0)
    lax.fori_loop(n_full, n_tiles, masked_body, 0)

    o = acc_scr[...] * (1.0 / l_scr[...])
    sgb = sgb_ref[0]
    for hd in range(N_HEADS):
        o_h = o[:, hd * tq:(hd + 1) * tq].T.astype(BF16)
        att = jnp.dot(o_h, w_uv_ref[hd], preferred_element_type=F32)
        sl = slice(hd * V_DIM, (hd + 1) * V_DIM)
        yb_ref[0, :, sl] = (att * sgb[:, sl]).astype(BF16)


def _attention_t(q, kcat, vt, sgb, w_uv, *, q_off, sk_valid, tq, tk):
    b, _, sq, _ = q.shape
    skp = kcat.shape[1]
    d = sgb.shape[-1]
    m = N_HEADS * tq
    gran = vt.shape[-1]
    assert tq % LANES == 0 and tq & (tq - 1) == 0 and sq % tq == 0 and (2 * MXU_DIM) % tq == 0
    assert skp % tk == 0 and tk % CHUNK == 0 and tk % gran == 0 and vt.shape[1] * gran == skp
    kern = functools.partial(_attn_t_kernel, q_off=q_off, sk_valid=sk_valid, tq=tq, tk=tk)
    return pl.pallas_call(
        kern,
        grid=(b, sq // tq),
        in_specs=[pl.BlockSpec((1, N_HEADS, tq, K_CAT), lambda i, j: (i, 0, j, 0)),
                  pl.BlockSpec((1, skp, K_CAT), lambda i, j: (i, 0, 0)),
                  pl.BlockSpec((1,) + vt.shape[1:], lambda i, j: (i, 0, 0, 0)),
                  pl.BlockSpec((1, tq, d), lambda i, j: (i, j, 0)),
                  _const_spec(w_uv.shape)],
        out_specs=pl.BlockSpec((1, tq, d), lambda i, j: (i, j, 0)),
        out_shape=jax.ShapeDtypeStruct((b, sq, d), BF16),
        scratch_shapes=[pltpu.VMEM((1, m), F32), pltpu.VMEM((1, m), F32),
                        pltpu.VMEM((KV_RANK, m), F32)],
        compiler_params=pltpu.CompilerParams(
            dimension_semantics=("parallel", "arbitrary"), vmem_limit_bytes=VMEM_LIMIT_BYTES),
        name="mla_attention_t",
    )(q, kcat, vt, sgb, w_uv)


def _attn_kernel(q_ref, k_ref, sgb_ref, w_uv_ref, yb_ref, m_scr, l_scr, acc_scr,
                 *, q_off, sk_valid, tq, tk):
    m = N_HEADS * tq
    q = q_ref[0].reshape(m, K_CAT)
    q_first = q_off + pl.program_id(1) * tq
    n_full, n_tiles = _visible_tiles(q_first, tq, tk, sk_valid)

    m_scr[...] = jnp.full(m_scr.shape, NEG_INF, F32)
    l_scr[...] = jnp.zeros(l_scr.shape, F32)
    acc_scr[...] = jnp.zeros(acc_scr.shape, F32)

    def step(j, masked):
        k = k_ref[0, pl.ds(pl.multiple_of(j * tk, tk), tk), :]
        s = lax.dot_general(q, k, (((1,), (1,)), ((), ())), preferred_element_type=F32)
        if masked:
            tok = lax.broadcasted_iota(jnp.int32, (m, tk), 0) & (tq - 1)
            k_lim = jnp.minimum(((q_first + tok) // CHUNK + 1) * CHUNK, sk_valid)
            k_pos = j * tk + lax.broadcasted_iota(jnp.int32, (m, tk), 1)
            s = jnp.where(k_pos < k_lim, s, NEG_INF)
        m_prev = m_scr[...]
        m_next = jnp.maximum(m_prev, jnp.max(s, axis=-1, keepdims=True))
        p = jnp.exp2(s - jnp.concatenate([m_next] * (tk // LANES), axis=1))
        alpha = jnp.exp2(m_prev - m_next)
        l_scr[...] = alpha * l_scr[...] + jnp.sum(p, axis=-1, keepdims=True)
        m_scr[...] = m_next
        pv = jnp.dot(p.astype(BF16), k[:, 0:KV_RANK], preferred_element_type=F32)
        acc_scr[...] = acc_scr[...] * jnp.concatenate([alpha] * (KV_RANK // LANES), axis=1) + pv

    def full_body(j, c):
        step(j, False)
        return c

    def masked_body(j, c):
        step(j, True)
        return c

    lax.fori_loop(0, n_full, full_body, 0)
    lax.fori_loop(n_full, n_tiles, masked_body, 0)

    inv_l = 1.0 / l_scr[...]
    o = (acc_scr[...] * jnp.concatenate([inv_l] * (KV_RANK // LANES), axis=1)).astype(BF16)
    sgb = sgb_ref[0]
    for hd in range(N_HEADS):
        att = jnp.dot(o[hd * tq:(hd + 1) * tq], w_uv_ref[hd], preferred_element_type=F32)
        sl = slice(hd * V_DIM, (hd + 1) * V_DIM)
        yb_ref[0, :, sl] = (att * sgb[:, sl]).astype(BF16)


def _attention(q, kcat, sgb, w_uv, *, q_off, sk_valid, tq, tk):
    b, _, sq, _ = q.shape
    skp = kcat.shape[1]
    d = sgb.shape[-1]
    m = N_HEADS * tq
    assert tq & (tq - 1) == 0 and sq % tq == 0 and skp % tk == 0 and tk % CHUNK == 0
    kern = functools.partial(_attn_kernel, q_off=q_off, sk_valid=sk_valid, tq=tq, tk=tk)
    return pl.pallas_call(
        kern,
        grid=(b, sq // tq),
        in_specs=[pl.BlockSpec((1, N_HEADS, tq, K_CAT), lambda i, j: (i, 0, j, 0)),
                  pl.BlockSpec((1, skp, K_CAT), lambda i, j: (i, 0, 0)),
                  pl.BlockSpec((1, tq, d), lambda i, j: (i, j, 0)),
                  _const_spec(w_uv.shape)],
        out_specs=pl.BlockSpec((1, tq, d), lambda i, j: (i, j, 0)),
        out_shape=jax.ShapeDtypeStruct((b, sq, d), BF16),
        scratch_shapes=[pltpu.VMEM((m, LANES), F32), pltpu.VMEM((m, LANES), F32),
                        pltpu.VMEM((m, KV_RANK), F32)],
        compiler_params=pltpu.CompilerParams(
            dimension_semantics=("parallel", "arbitrary"), vmem_limit_bytes=VMEM_LIMIT_BYTES),
        name="mla_attention",
    )(q, kcat, sgb, w_uv)


def _outproj_kernel(x_ref, mod_ref, ya_ref, yb_ref, sua_ref, sub_ref,
                    w_a_ref, w_b_ref, w_o_ref, post_norm_ref, y_ref):
    bb, t, d = x_ref.shape
    m = bb * t
    ya = ya_ref[...].reshape(m, d)
    yb = yb_ref[...].reshape(m, d)
    pa = jnp.dot(ya, w_a_ref[...], preferred_element_type=F32)
    pb = jnp.dot(yb, w_b_ref[...], preferred_element_type=F32)
    merged = sua_ref[...].reshape(m, d).astype(F32) * pa + sub_ref[...].reshape(m, d).astype(F32) * pb
    o = jnp.dot(merged.astype(BF16), w_o_ref[...], preferred_element_type=F32)
    gate = mod_ref[...][:, :, 2 * d:3 * d]
    y_ref[...] = x_ref[...] + gate * _rms(o, post_norm_ref[...]).reshape(bb, t, d)


def _outproj(x, mod, ya, yb, sua, sub, lw, *, bb, t):
    b, s, d = x.shape
    tok = pl.BlockSpec((bb, t, d), lambda i, j: (i, j, 0))
    weights = [lw["w_branch_a"], lw["w_branch_b"], lw["w_out"], lw["post_norm"]]
    return pl.pallas_call(
        _outproj_kernel,
        grid=(b // bb, s // t),
        in_specs=[tok, pl.BlockSpec((bb, 1, 3 * d), lambda i, j: (i, 0, 0)), tok, tok, tok, tok]
        + [_const_spec(w.shape) for w in weights],
        out_specs=tok,
        out_shape=jax.ShapeDtypeStruct((b, s, d), F32),
        compiler_params=pltpu.CompilerParams(
            dimension_semantics=("parallel", "parallel"), vmem_limit_bytes=VMEM_LIMIT_BYTES),
        name="outproj",
    )(x, mod, ya, yb, sua, sub, *weights)


def _rot_half(w):
    half = w.shape[-1] // 2
    return jnp.concatenate([-w[..., half:], w[..., :half]], axis=-1)


def _twice(w):
    return jnp.concatenate([w, w], axis=-1)


def _layer_weights(l, p):
    d = p["w_in"].shape[1]
    w_in = p["w_in"][l]
    o = 0
    seg = {}
    for name, width in (("xa", d), ("ga", d), ("cq", Q_RANK), ("ckv", KV_RANK), ("kpe", QK_ROPE),
                        ("gb", d), ("ua", d), ("ub", d)):
        seg[name] = w_in[:, o:o + width]
        o += width
    w_in_ext = jnp.concatenate(
        [seg["xa"], seg["ga"], seg["cq"], seg["ckv"], _twice(seg["kpe"]),
         _twice(_rot_half(seg["kpe"])), seg["gb"], seg["ua"], seg["ub"]], axis=1)
    wq = p["w_q_up"][l].reshape(Q_RANK, N_HEADS, QK_NOPE + QK_ROPE)
    wq_pe = wq[:, :, QK_NOPE:]
    w_q = jnp.concatenate(
        [wq[:, :, :QK_NOPE].reshape(Q_RANK, -1), wq_pe.reshape(Q_RANK, -1),
         _rot_half(wq_pe).reshape(Q_RANK, -1)], axis=1)
    row = lambda v: v.reshape(1, -1)
    return {
        "w_in": w_in_ext.astype(BF16),
        "w_q": w_q.astype(BF16),
        "w_uk": jnp.transpose(p["w_uk"][l], (1, 2, 0)).astype(BF16),
        "w_uv": jnp.transpose(p["w_uv"][l], (1, 0, 2)).astype(BF16),
        "w_gate": jnp.concatenate([p["lru_wa"][l], p["lru_wx"][l]], axis=-1).astype(BF16),
        "pre_norm": row(p["pre_norm"][l]), "post_norm": row(p["post_norm"][l]),
        "conv_w": p["conv_w"][l], "conv_b": row(p["conv_b"][l]),
        "lru_ba": row(p["lru_ba"][l]), "lru_bx": row(p["lru_bx"][l]),
        "lru_lambda": row(p["lru_lambda"][l]),
        "q_norm": row(p["q_norm"][l]) * Q_SCALE, "kv_norm": row(p["kv_norm"][l]),
        "w_branch_a": p["w_branch_a"][l].astype(BF16), "w_branch_b": p["w_branch_b"][l].astype(BF16),
        "w_out": p["w_out"][l].astype(BF16),
    }


def _rope_tables(pos):
    half = QK_ROPE // 2
    inv = ROPE_THETA ** (-jnp.arange(half, dtype=F32) / half)
    ang = pos.astype(F32)[:, None] * inv[None, :]
    cos, sin = jnp.cos(ang), jnp.sin(ang)
    return jnp.concatenate([cos] * (LANES // half), axis=1), jnp.concatenate([sin] * (LANES // half), axis=1)


def _pick_tile(n, target):
    t = min(n, target)
    while n % t:
        t //= 2
    return t


def kernel(x_prompt, x_sample, c_prompt, c_sample, cache_ckv, cache_kpe, state_conv, state_lru, ada_w, ada_b, pre_norm, post_norm, w_in, conv_w, conv_b, lru_wa, lru_ba, lru_wx, lru_bx, lru_lambda, q_norm, w_q_up, kv_norm, w_uk, w_uv, w_branch_a, w_branch_b, w_out):
    p = dict(ada_w=ada_w, ada_b=ada_b, pre_norm=pre_norm, post_norm=post_norm, w_in=w_in,
             conv_w=conv_w, conv_b=conv_b, lru_wa=lru_wa, lru_ba=lru_ba, lru_wx=lru_wx, lru_bx=lru_bx,
             lru_lambda=lru_lambda, q_norm=q_norm, w_q_up=w_q_up, kv_norm=kv_norm, w_uk=w_uk,
             w_uv=w_uv, w_branch_a=w_branch_a, w_branch_b=w_branch_b, w_out=w_out)
    depth = w_in.shape[0]
    b_p, s_p, d = x_prompt.shape
    b_s, s_s, _ = x_sample.shape
    past_len = cache_ckv.shape[2]
    assert s_p % SUBLANES == 0 and s_s % SUBLANES == 0 and s_p >= SUBLANES and s_s >= SUBLANES

    mods = _modulation(jnp.concatenate([c_prompt, c_sample], axis=0), ada_w, ada_b)
    cos_p, sin_p = _rope_tables(jnp.arange(s_p, dtype=jnp.int32))
    cos_s, sin_s = _rope_tables(past_len + jnp.arange(s_s, dtype=jnp.int32))
    conv0_p = jnp.zeros((b_p, SUBLANES, d), F32)
    h0_p = jnp.zeros((b_p, 1, d), F32)

    t_in_p = _pick_tile(s_p, 256)
    t_out_p = _pick_tile(s_p, 512)
    tq_p = _pick_tile(s_p, 128)
    tk_p = _pick_tile(s_p, 512)
    tk_s = 256
    sk_s = past_len + s_s
    skp_s = -(-sk_s // tk_s) * tk_s

    yp, ys = x_prompt, x_sample
    outs_p, outs_s = [], []
    for l in range(depth):
        lw = _layer_weights(l, p)

        mod_p = mods[l, :b_p, None, :]
        ya, sgb, sua, sub, q, ckv, kpe, kcat, convo, hlast, vt = _inproj(
            yp, mod_p, conv0_p, h0_p, cos_p, sin_p, lw, bb=1, t=t_in_p, emit_vt=True)
        yb = _attention_t(q, kcat, vt, sgb, lw["w_uv"], q_off=0, sk_valid=s_p, tq=tq_p, tk=tk_p)
        yp = _outproj(yp, mod_p, ya, yb, sua, sub, lw, bb=1, t=t_out_p)
        outs_p.append((ckv, kpe, convo[:, SUBLANES - (CONV_W - 1):, :], hlast[:, 0, :]))

        mod_s = mods[l, b_p:, None, :]
        conv0_s = jnp.pad(state_conv[l], ((0, 0), (SUBLANES - (CONV_W - 1), 0), (0, 0)))
        ya, sgb, sua, sub, q, ckv, kpe, kcat, convo, hlast = _inproj(
            ys, mod_s, conv0_s, state_lru[l][:, None, :], cos_s, sin_s, lw, bb=b_s, t=s_s,
            emit_vt=False)
        kpe_past = cache_kpe[l].astype(BF16)
        kall = jnp.concatenate(
            [jnp.concatenate([cache_ckv[l].astype(BF16), kpe_past, kpe_past], axis=-1), kcat,
             jnp.zeros((b_s, skp_s - sk_s, K_CAT), BF16)], axis=1)
        yb = _attention(q, kall, sgb, lw["w_uv"], q_off=past_len, sk_valid=sk_s, tq=s_s, tk=tk_s)
        ys = _outproj(ys, mod_s, ya, yb, sua, sub, lw, bb=b_s, t=s_s)
        outs_s.append((ckv, kpe, convo[:, SUBLANES - (CONV_W - 1):, :], hlast[:, 0, :]))

    stack = lambda outs, k: jnp.stack([o[k] for o in outs])
    return (yp, ys,
            stack(outs_p, 0), stack(outs_p, 1), stack(outs_p, 2), stack(outs_p, 3),
            stack(outs_s, 0), stack(outs_s, 1), stack(outs_s, 2), stack(outs_s, 3))
```

```python
import functools
import math

import jax
import jax.numpy as jnp
from jax import lax
from jax.experimental import pallas as pl
from jax.experimental.pallas import tpu as pltpu

F32 = jnp.float32
BF16 = jnp.bfloat16

EPS = 1e-6
NEG_INF = -1e30
CHUNK = 64
LRU_C = 8.0
ROPE_THETA = 10000.0

LANES = 128
SUBLANES = 8
MXU_DIM = 256
VMEM_LIMIT_BYTES = 56 * 1024 * 1024

CONV_W = 4
N_HEADS = 8
QK_NOPE = 128
QK_ROPE = 64
V_DIM = 128
KV_RANK = 256
Q_RANK = 768
LRU_BLOCKS = 8
K_CAT = KV_RANK + LANES
Q_SCALE = (QK_NOPE + QK_ROPE) ** -0.5 * math.log2(math.e)


def _sigmoid(x):
    return jax.nn.sigmoid(x)


def _silu(x):
    return x * _sigmoid(x)


def _sqrt_nonneg(x):
    return jnp.where(x > 0.0, x * lax.rsqrt(x), 0.0)


def _rms(x, g):
    return x * lax.rsqrt(jnp.mean(x * x, axis=-1, keepdims=True) + EPS) * g


def _const_spec(shape):
    nd = len(shape)
    return pl.BlockSpec(shape, lambda *_: (0,) * nd, pipeline_mode=pl.Buffered(1))


def _mod_kernel(c_ref, w_ref, b_ref, o_ref):
    c = c_ref[...]
    o_ref[0] = jnp.dot(_silu(c), w_ref[0], preferred_element_type=F32,
                       precision=lax.Precision.HIGHEST) + b_ref[0]


def _modulation(c_all, ada_w, ada_b):
    depth, d, d3 = ada_w.shape
    n = c_all.shape[0]
    nblk = d3 // d
    return pl.pallas_call(
        _mod_kernel,
        grid=(depth, nblk),
        in_specs=[pl.BlockSpec((n, d), lambda l, j: (0, 0)),
                  pl.BlockSpec((1, d, d), lambda l, j: (l, 0, j)),
                  pl.BlockSpec((1, 1, d), lambda l, j: (l, 0, j))],
        out_specs=pl.BlockSpec((1, n, d), lambda l, j: (l, 0, j)),
        out_shape=jax.ShapeDtypeStruct((depth, n, d3), F32),
        name="adaln_mod",
    )(c_all, ada_w, ada_b.reshape(depth, 1, d3))


def _inproj_kernel(x_ref, mod_ref, conv0_ref, h0_ref, cos_ref, sin_ref,
                   w_in_ref, w_q_ref, w_uk_ref, w_gate_ref,
                   pre_norm_ref, conv_w_ref, conv_b_ref, ba_ref, bx_ref, lam_ref, qn_ref, kvn_ref,
                   ya_ref, sgb_ref, sua_ref, sub_ref, q_ref, ckv_ref, kpe_ref, kcat_ref,
                   convo_ref, hlast_ref, *rest, emit_vt):
    if emit_vt:
        vt_ref, buf_ref, hcar_ref, hb_ref, cqn_ref = rest
    else:
        buf_ref, hcar_ref, hb_ref, cqn_ref = rest
    bb, t, d = x_ref.shape
    m = bb * t
    groups = t // SUBLANES
    gw = d // LRU_BLOCKS
    cw_ = MXU_DIM
    n_chunks = d // cw_
    blocks_per_chunk = cw_ // gw
    nq = N_HEADS * QK_NOPE
    npe = N_HEADS * QK_ROPE
    o_xa, o_ga, o_cq = 0, d, 2 * d
    o_ckv = o_cq + Q_RANK
    o_kpe = o_ckv + KV_RANK
    o_krot = o_kpe + LANES
    o_gb = o_krot + LANES
    o_ua = o_gb + d
    o_ub = o_ua + d

    @pl.when(pl.program_id(1) == 0)
    def _():
        buf_ref[:, 0:SUBLANES, :] = conv0_ref[...]
        hcar_ref[...] = h0_ref[...]

    x = x_ref[...]
    mod = mod_ref[...]
    shift = mod[:, :, 0:d]
    scale = mod[:, :, d:2 * d]
    h = _rms(x, pre_norm_ref[...]) * (1.0 + scale) + shift
    hb_ref[...] = h.reshape(m, d).astype(BF16)

    def proj(lo, width):
        return jnp.dot(hb_ref[...], w_in_ref[:, lo:lo + width], preferred_element_type=F32)

    cos = cos_ref[...][None]
    sin = sin_ref[...][None]

    def conv_phase(c):
        cs = slice(c * cw_, (c + 1) * cw_)
        xa = proj(o_xa + c * cw_, cw_)
        buf_ref[:, SUBLANES:SUBLANES + t, cs] = xa.reshape(bb, t, cw_)
        cw = conv_w_ref[:, cs]
        xc = conv_b_ref[:, cs] + buf_ref[:, SUBLANES - 3:SUBLANES - 3 + t, cs] * cw[0:1]
        xc = xc + buf_ref[:, SUBLANES - 2:SUBLANES - 2 + t, cs] * cw[1:2]
        xc = xc + buf_ref[:, SUBLANES - 1:SUBLANES - 1 + t, cs] * cw[2:3]
        xc = xc + buf_ref[:, SUBLANES:SUBLANES + t, cs] * cw[3:4]
        tail = buf_ref[:, t:t + SUBLANES, cs]
        convo_ref[:, :, cs] = tail
        buf_ref[:, 0:SUBLANES, cs] = tail
        return xc.reshape(m, cw_)

    def gate_phase(c, xc2):
        cs = slice(c * cw_, (c + 1) * cw_)
        xcb = xc2.astype(BF16)
        r_parts, i_parts = [], []
        for n in range(blocks_per_chunk):
            g = jnp.dot(xcb[:, n * gw:(n + 1) * gw], w_gate_ref[c * blocks_per_chunk + n],
                        preferred_element_type=F32)
            r_parts.append(g[:, 0:gw])
            i_parts.append(g[:, gw:2 * gw])
        r = _sigmoid(jnp.concatenate(r_parts, axis=1) + ba_ref[:, cs])
        ig = _sigmoid(jnp.concatenate(i_parts, axis=1) + bx_ref[:, cs])
        nl = -lam_ref[:, cs]
        softplus = jnp.maximum(nl, 0.0) + jnp.log1p(jnp.exp(-jnp.abs(nl)))
        th = jnp.tanh((-0.5 * LRU_C * r) * softplus)
        inv = 1.0 / (1.0 - th)
        a = (1.0 + th) * inv
        mult = (2.0 * _sqrt_nonneg(-th)) * inv
        return a, mult * (ig * xc2)

    def scan_phase(c, a, b):
        cs = slice(c * cw_, (c + 1) * cw_)
        a4 = a.reshape(bb * groups, SUBLANES, cw_)
        b4 = b.reshape(bb * groups, SUBLANES, cw_)
        row = lax.broadcasted_iota(jnp.int32, a4.shape, 1)
        for sft in (1, 2, 4):
            keep = row >= sft
            a_sh = jnp.where(keep, pltpu.roll(a4, sft, axis=1), 1.0)
            b_sh = jnp.where(keep, pltpu.roll(b4, sft, axis=1), 0.0)
            b4 = b4 + a4 * b_sh
            a4 = a4 * a_sh
        a5 = a4.reshape(bb, groups, SUBLANES, cw_)
        b5 = b4.reshape(bb, groups, SUBLANES, cw_)
        hc = hcar_ref[:, :, cs]
        outs = []
        for g in range(groups):
            hs = a5[:, g] * hc + b5[:, g]
            outs.append(hs)
            hc = hs[:, SUBLANES - 1:SUBLANES, :]
        hcar_ref[:, :, cs] = hc
        hlast_ref[:, :, cs] = hc
        return jnp.concatenate(outs, axis=1).reshape(m, cw_)

    def out_phase(c, y_lru):
        cs = slice(c * cw_, (c + 1) * cw_)
        ga = proj(o_ga + c * cw_, cw_)
        ya_ref[:, :, cs] = (y_lru * _silu(ga)).astype(BF16).reshape(bb, t, cw_)

    def gate_out(ref, fn, off, c):
        cs = slice(c * cw_, (c + 1) * cw_)
        ref[:, :, cs] = fn(proj(off + c * cw_, cw_)).astype(BF16).reshape(bb, t, cw_)

    def q_latent(heads):
        q_nope = jnp.dot(cqn_ref[...], w_q_ref[:, heads[0] * QK_NOPE:(heads[-1] + 1) * QK_NOPE],
                         preferred_element_type=F32).astype(BF16)
        for k, hd in enumerate(heads):
            q_lat = jnp.dot(q_nope[:, k * QK_NOPE:(k + 1) * QK_NOPE], w_uk_ref[hd],
                            preferred_element_type=F32)
            q_ref[:, hd, :, 0:KV_RANK] = q_lat.astype(BF16).reshape(bb, t, KV_RANK)

    def q_rotary():
        q_pe = jnp.dot(cqn_ref[...], w_q_ref[:, nq:nq + npe], preferred_element_type=F32)
        q_rot = jnp.dot(cqn_ref[...], w_q_ref[:, nq + npe:nq + 2 * npe], preferred_element_type=F32)
        lane = lax.broadcasted_iota(jnp.int32, (1, 1, LANES), 2)
        for pair in range(N_HEADS // 2):
            sl = slice(pair * LANES, (pair + 1) * LANES)
            both = q_pe[:, sl].reshape(bb, t, LANES) * cos + q_rot[:, sl].reshape(bb, t, LANES) * sin
            q_ref[:, 2 * pair, :, KV_RANK:K_CAT] = jnp.where(lane < QK_ROPE, both, 0.0).astype(BF16)
            q_ref[:, 2 * pair + 1, :, KV_RANK:K_CAT] = jnp.where(lane < QK_ROPE, 0.0, both).astype(BF16)

    def key_latents():
        ckv = _rms(proj(o_ckv, KV_RANK), kvn_ref[...])
        ckv_ref[...] = ckv.reshape(bb, t, KV_RANK)
        kcat_ref[:, :, 0:KV_RANK] = ckv.astype(BF16).reshape(bb, t, KV_RANK)
        if emit_vt:
            vt_ref[0, 0] = ckv.T.astype(BF16)
        kp = proj(o_kpe, LANES).reshape(bb, t, LANES)
        kr = proj(o_krot, LANES).reshape(bb, t, LANES)
        kroped = kp * cos + kr * sin
        kpe_ref[...] = kroped[:, :, 0:QK_ROPE]
        kcat_ref[:, :, KV_RANK:K_CAT] = kroped.astype(BF16)

    cqn_ref[...] = _rms(proj(o_cq, Q_RANK), qn_ref[...]).astype(BF16)
    other = [lambda: q_latent((0, 1, 2, 3)), lambda: q_latent((4, 5, 6, 7)), q_rotary, key_latents]
    for c in range(n_chunks):
        xc2 = conv_phase(c)
        gate_out(sgb_ref, _silu, o_gb, c)
        a, b = gate_phase(c, xc2)
        gate_out(sua_ref, _sigmoid, o_ua, c)
        y_lru = scan_phase(c, a, b)
        gate_out(sub_ref, _sigmoid, o_ub, c)
        out_phase(c, y_lru)
        if c < len(other):
            other[c]()
    for f in other[n_chunks:]:
        f()


def _inproj(x, mod, conv0, h0, cos, sin, lw, *, bb, t, emit_vt):
    b, s, d = x.shape
    grid = (b // bb, s // t)
    tok = lambda w: pl.BlockSpec((bb, t, w), lambda i, j: (i, j, 0))
    per_b = lambda r, w: pl.BlockSpec((bb, r, w), lambda i, j: (i, 0, 0))
    weights = [lw["w_in"], lw["w_q"], lw["w_uk"], lw["w_gate"], lw["pre_norm"], lw["conv_w"],
               lw["conv_b"], lw["lru_ba"], lw["lru_bx"], lw["lru_lambda"], lw["q_norm"], lw["kv_norm"]]
    in_specs = ([tok(d), per_b(1, 3 * d), per_b(SUBLANES, d), per_b(1, d),
                 pl.BlockSpec((t, LANES), lambda i, j: (j, 0)),
                 pl.BlockSpec((t, LANES), lambda i, j: (j, 0))]
                + [_const_spec(w.shape) for w in weights])
    out_shape = [jax.ShapeDtypeStruct((b, s, d), BF16)] * 4 + [
        jax.ShapeDtypeStruct((b, N_HEADS, s, K_CAT), BF16),
        jax.ShapeDtypeStruct((b, s, KV_RANK), F32),
        jax.ShapeDtypeStruct((b, s, QK_ROPE), F32),
        jax.ShapeDtypeStruct((b, s, K_CAT), BF16),
        jax.ShapeDtypeStruct((b, SUBLANES, d), F32),
        jax.ShapeDtypeStruct((b, 1, d), F32),
    ]
    out_specs = [tok(d)] * 4 + [
        pl.BlockSpec((bb, N_HEADS, t, K_CAT), lambda i, j: (i, 0, j, 0)),
        tok(KV_RANK), tok(QK_ROPE), tok(K_CAT), per_b(SUBLANES, d), per_b(1, d),
    ]
    if emit_vt:
        assert bb == 1
        out_shape.append(jax.ShapeDtypeStruct((b, s // t, KV_RANK, t), BF16))
        out_specs.append(pl.BlockSpec((1, 1, KV_RANK, t), lambda i, j: (i, j, 0, 0)))
    return pl.pallas_call(
        functools.partial(_inproj_kernel, emit_vt=emit_vt),
        grid=grid,
        in_specs=in_specs,
        out_specs=out_specs,
        out_shape=out_shape,
        scratch_shapes=[pltpu.VMEM((bb, t + SUBLANES, d), F32), pltpu.VMEM((bb, 1, d), F32),
                        pltpu.VMEM((bb * t, d), BF16), pltpu.VMEM((bb * t, Q_RANK), BF16)],
        compiler_params=pltpu.CompilerParams(
            dimension_semantics=("parallel", "arbitrary"), vmem_limit_bytes=VMEM_LIMIT_BYTES),
        name="inproj_lru",
    )(x, mod, conv0, h0, cos, sin, *weights)


def _visible_tiles(q_first, tq, tk, sk_valid):
    vis_first = jnp.minimum((q_first // CHUNK + 1) * CHUNK, sk_valid)
    vis_last = jnp.minimum(((q_first + tq - 1) // CHUNK + 1) * CHUNK, sk_valid)
    return vis_first // tk, (vis_last + tk - 1) // tk


def _attn_t_kernel(q_ref, k_ref, vt_ref, sgb_ref, w_uv_ref, yb_ref, m_scr, l_scr, acc_scr, s_scr,
                   *, q_off, sk_valid, tq, tk):
    m = N_HEADS * tq
    gran = vt_ref.shape[-1]
    gcols = 2 * MXU_DIM
    q_first = q_off + pl.program_id(1) * tq
    _, n_tiles = _visible_tiles(q_first, tq, tk, sk_valid)

    m_scr[...] = jnp.full(m_scr.shape, NEG_INF, F32)
    l_scr[...] = jnp.zeros(l_scr.shape, F32)
    acc_scr[...] = jnp.zeros(acc_scr.shape, F32)

    def qk(j, g, rows):
        k = k_ref[0, pl.ds(pl.multiple_of(j * tk, tk), rows), :]
        q_g = q_ref[0, g * gcols // tq:(g + 1) * gcols // tq].reshape(gcols, K_CAT)
        return lax.dot_general(k, q_g, (((1,), (1,)), ((), ())), preferred_element_type=F32)

    n_groups = m // gcols
    s_scr[0] = qk(0, 0, tk)

    def step(j, masked, last_tile, rows=tk):
        vt = jnp.concatenate([vt_ref[0, j * (tk // gran) + g][:, 0:min(gran, rows - g * gran)]
                              for g in range(-(-rows // gran))], axis=1)
        for g in range(n_groups):
            cs = slice(g * gcols, (g + 1) * gcols)
            if g + 1 < n_groups:
                s_scr[g + 1, 0:rows] = qk(j, g + 1, rows)
            elif not last_tile:
                s_scr[0] = qk(j + 1, 0, tk)
            s = s_scr[g, 0:rows]
            if masked:
                tok = lax.broadcasted_iota(jnp.int32, (1, gcols), 1) & (tq - 1)
                k_lim = jnp.minimum(((q_first + tok) // CHUNK + 1) * CHUNK, sk_valid)
                k_pos = j * tk + lax.broadcasted_iota(jnp.int32, (rows, gcols), 0)
                s = jnp.where(k_pos < k_lim, s, NEG_INF)
            m_prev = m_scr[:, cs]
            m_next = jnp.maximum(m_prev, jnp.max(s, axis=0, keepdims=True))
            p = jnp.exp2(s - m_next)
            alpha = jnp.exp2(m_prev - m_next)
            l_scr[:, cs] = alpha * l_scr[:, cs] + jnp.sum(p, axis=0, keepdims=True)
            m_scr[:, cs] = m_next
            pv = jnp.dot(vt, p.astype(BF16), preferred_element_type=F32)
            acc_scr[:, cs] = acc_scr[:, cs] * alpha + pv

    last = n_tiles - 1

    def body(jj, c):
        step(2 * jj, False, False)
        step(2 * jj + 1, False, False)
        return c

    lax.fori_loop(0, last // 2, body, 0)

    @pl.when(last % 2 == 1)
    def _():
        step(last - 1, False, False)

    for r in range(tk // tq):
        @pl.when((q_first % tk) // tq == r)
        def _():
            step(last, True, True, rows=(r + 1) * tq)

    o = acc_scr[...] * (1.0 / l_scr[...])
    sgb = sgb_ref[0]
    for hd in range(N_HEADS):
        o_h = o[:, hd * tq:(hd + 1) * tq].T.astype(BF16)
        att = jnp.dot(o_h, w_uv_ref[hd], preferred_element_type=F32)
        sl = slice(hd * V_DIM, (hd + 1) * V_DIM)
        yb_ref[0, :, sl] = (att * sgb[:, sl]).astype(BF16)


def _attention_t(q, kcat, vt, sgb, w_uv, *, q_off, sk_valid, tq, tk):
    b, _, sq, _ = q.shape
    skp = kcat.shape[1]
    d = sgb.shape[-1]
    m = N_HEADS * tq
    gran = vt.shape[-1]
    assert tq % LANES == 0 and tq & (tq - 1) == 0 and sq % tq == 0 and (2 * MXU_DIM) % tq == 0
    assert skp % tk == 0 and tk % CHUNK == 0 and tk % gran == 0 and vt.shape[1] * gran == skp
    assert q_off % tq == 0 and tk % tq == 0 and tq % CHUNK == 0 and sk_valid == skp
    kern = functools.partial(_attn_t_kernel, q_off=q_off, sk_valid=sk_valid, tq=tq, tk=tk)
    return pl.pallas_call(
        kern,
        grid=(b, sq // tq),
        in_specs=[pl.BlockSpec((1, N_HEADS, tq, K_CAT), lambda i, j: (i, 0, j, 0)),
                  pl.BlockSpec((1, skp, K_CAT), lambda i, j: (i, 0, 0)),
                  pl.BlockSpec((1,) + vt.shape[1:], lambda i, j: (i, 0, 0, 0)),
                  pl.BlockSpec((1, tq, d), lambda i, j: (i, j, 0)),
                  _const_spec(w_uv.shape)],
        out_specs=pl.BlockSpec((1, tq, d), lambda i, j: (i, j, 0)),
        out_shape=jax.ShapeDtypeStruct((b, sq, d), BF16),
        scratch_shapes=[pltpu.VMEM((1, m), F32), pltpu.VMEM((1, m), F32),
                        pltpu.VMEM((KV_RANK, m), F32),
                        pltpu.VMEM((m // (2 * MXU_DIM), tk, 2 * MXU_DIM), F32)],
        compiler_params=pltpu.CompilerParams(
            dimension_semantics=("parallel", "arbitrary"), vmem_limit_bytes=VMEM_LIMIT_BYTES),
        name="mla_attention_t",
    )(q, kcat, vt, sgb, w_uv)


def _attn_cached_kernel(q_ref, ckv_ref, kpe_ref, knew_ref, sgb_ref, w_uv_ref, yb_ref,
                        m_scr, l_scr, acc_scr, *, past_len, sq, tk):
    m = N_HEADS * sq
    q = q_ref[0].reshape(m, K_CAT)
    q_lat = q[:, 0:KV_RANK]
    q_rope = q[:, KV_RANK:K_CAT].astype(F32)
    q_rope = (q_rope[:, 0:QK_ROPE] + q_rope[:, QK_ROPE:LANES]).astype(BF16)
    q_pos = past_len + (lax.broadcasted_iota(jnp.int32, (m, 1), 0) & (sq - 1))
    k_lim = (q_pos // CHUNK + 1) * CHUNK

    m_scr[...] = jnp.full(m_scr.shape, NEG_INF, F32)
    l_scr[...] = jnp.zeros(l_scr.shape, F32)
    acc_scr[...] = jnp.zeros(acc_scr.shape, F32)
    nt = (((1,), (1,)), ((), ()))

    def update(s, visible, v):
        s = jnp.where(visible, s, NEG_INF)
        m_prev = m_scr[...]
        m_next = jnp.maximum(m_prev, jnp.max(s, axis=-1, keepdims=True))
        p = jnp.exp2(s - jnp.concatenate([m_next] * (s.shape[1] // LANES), axis=1))
        alpha = jnp.exp2(m_prev - m_next)
        l_scr[...] = alpha * l_scr[...] + jnp.sum(p, axis=-1, keepdims=True)
        m_scr[...] = m_next
        pv = jnp.dot(p.astype(BF16), v, preferred_element_type=F32)
        acc_scr[...] = acc_scr[...] * jnp.concatenate([alpha] * (KV_RANK // LANES), axis=1) + pv

    def past_body(j, c):
        off = pl.multiple_of(j * tk, tk)
        lat = ckv_ref[0, 0, pl.ds(off, tk), :].astype(BF16)
        pe = kpe_ref[0, 0, pl.ds(off, tk), :].astype(BF16)
        s = (lax.dot_general(q_lat, lat, nt, preferred_element_type=F32)
             + lax.dot_general(q_rope, pe, nt, preferred_element_type=F32))
        k_pos = off + lax.broadcasted_iota(jnp.int32, (m, tk), 1)
        update(s, k_pos < k_lim, lat)
        return c

    lax.fori_loop(0, past_len // tk, past_body, 0)

    kn = knew_ref[0]
    s = lax.dot_general(q, kn, nt, preferred_element_type=F32)
    idx = lax.broadcasted_iota(jnp.int32, (m, kn.shape[0]), 1)
    update(s, jnp.where(idx < sq, past_len + idx, k_lim) < k_lim, kn[:, 0:KV_RANK])

    inv_l = 1.0 / l_scr[...]
    o = (acc_scr[...] * jnp.concatenate([inv_l] * (KV_RANK // LANES), axis=1)).astype(BF16)
    sgb = sgb_ref[0]
    for hd in range(N_HEADS):
        att = jnp.dot(o[hd * sq:(hd + 1) * sq], w_uv_ref[hd], preferred_element_type=F32)
        sl = slice(hd * V_DIM, (hd + 1) * V_DIM)
        yb_ref[0, :, sl] = (att * sgb[:, sl]).astype(BF16)


def _attention_cached(q, cache_ckv, cache_kpe, layer, knew, sgb, w_uv, *, tk):
    b, _, sq, _ = q.shape
    past_len = cache_ckv.shape[2]
    d = sgb.shape[-1]
    m = N_HEADS * sq
    assert sq & (sq - 1) == 0 and past_len % tk == 0 and tk % LANES == 0 and knew.shape[1] % LANES == 0
    kern = functools.partial(_attn_cached_kernel, past_len=past_len, sq=sq, tk=tk)
    return pl.pallas_call(
        kern,
        grid=(b,),
        in_specs=[pl.BlockSpec((1, N_HEADS, sq, K_CAT), lambda i: (i, 0, 0, 0)),
                  pl.BlockSpec((1, 1, past_len, KV_RANK), lambda i: (layer, i, 0, 0)),
                  pl.BlockSpec((1, 1, past_len, QK_ROPE), lambda i: (layer, i, 0, 0)),
                  pl.BlockSpec((1,) + knew.shape[1:], lambda i: (i, 0, 0)),
                  pl.BlockSpec((1, sq, d), lambda i: (i, 0, 0)),
                  _const_spec(w_uv.shape)],
        out_specs=pl.BlockSpec((1, sq, d), lambda i: (i, 0, 0)),
        out_shape=jax.ShapeDtypeStruct((b, sq, d), BF16),
        scratch_shapes=[pltpu.VMEM((m, LANES), F32), pltpu.VMEM((m, LANES), F32),
                        pltpu.VMEM((m, KV_RANK), F32)],
        compiler_params=pltpu.CompilerParams(
            dimension_semantics=("parallel",), vmem_limit_bytes=VMEM_LIMIT_BYTES),
        name="mla_attention_cached",
    )(q, cache_ckv, cache_kpe, knew, sgb, w_uv)


def _outproj_kernel(x_ref, mod_ref, ya_ref, yb_ref, sua_ref, sub_ref,
                    w_a_ref, w_b_ref, w_o_ref, post_norm_ref, y_ref):
    bb, t, d = x_ref.shape
    m = bb * t
    ya = ya_ref[...].reshape(m, d)
    yb = yb_ref[...].reshape(m, d)
    pa = jnp.dot(ya, w_a_ref[...], preferred_element_type=F32)
    pb = jnp.dot(yb, w_b_ref[...], preferred_element_type=F32)
    merged = sua_ref[...].reshape(m, d).astype(F32) * pa + sub_ref[...].reshape(m, d).astype(F32) * pb
    o = jnp.dot(merged.astype(BF16), w_o_ref[...], preferred_element_type=F32)
    gate = mod_ref[...][:, :, 2 * d:3 * d]
    y_ref[...] = x_ref[...] + gate * _rms(o, post_norm_ref[...]).reshape(bb, t, d)


def _outproj(x, mod, ya, yb, sua, sub, lw, *, bb, t):
    b, s, d = x.shape
    tok = pl.BlockSpec((bb, t, d), lambda i, j: (i, j, 0))
    weights = [lw["w_branch_a"], lw["w_branch_b"], lw["w_out"], lw["post_norm"]]
    return pl.pallas_call(
        _outproj_kernel,
        grid=(b // bb, s // t),
        in_specs=[tok, pl.BlockSpec((bb, 1, 3 * d), lambda i, j: (i, 0, 0)), tok, tok, tok, tok]
        + [_const_spec(w.shape) for w in weights],
        out_specs=tok,
        out_shape=jax.ShapeDtypeStruct((b, s, d), F32),
        compiler_params=pltpu.CompilerParams(
            dimension_semantics=("parallel", "parallel"), vmem_limit_bytes=VMEM_LIMIT_BYTES),
        name="outproj",
    )(x, mod, ya, yb, sua, sub, *weights)


def _rot_half(w):
    half = w.shape[-1] // 2
    return jnp.concatenate([-w[..., half:], w[..., :half]], axis=-1)


def _twice(w):
    return jnp.concatenate([w, w], axis=-1)


def _layer_weights(l, p):
    d = p["w_in"].shape[1]
    w_in = p["w_in"][l]
    o = 0
    seg = {}
    for name, width in (("xa", d), ("ga", d), ("cq", Q_RANK), ("ckv", KV_RANK), ("kpe", QK_ROPE),
                        ("gb", d), ("ua", d), ("ub", d)):
        seg[name] = w_in[:, o:o + width]
        o += width
    w_in_ext = jnp.concatenate(
        [seg["xa"], seg["ga"], seg["cq"], seg["ckv"], _twice(seg["kpe"]),
         _twice(_rot_half(seg["kpe"])), seg["gb"], seg["ua"], seg["ub"]], axis=1)
    wq = p["w_q_up"][l].reshape(Q_RANK, N_HEADS, QK_NOPE + QK_ROPE)
    wq_pe = wq[:, :, QK_NOPE:]
    w_q = jnp.concatenate(
        [wq[:, :, :QK_NOPE].reshape(Q_RANK, -1), wq_pe.reshape(Q_RANK, -1),
         _rot_half(wq_pe).reshape(Q_RANK, -1)], axis=1)
    row = lambda v: v.reshape(1, -1)
    return {
        "w_in": w_in_ext.astype(BF16),
        "w_q": w_q.astype(BF16),
        "w_uk": jnp.transpose(p["w_uk"][l], (1, 2, 0)).astype(BF16),
        "w_uv": jnp.transpose(p["w_uv"][l], (1, 0, 2)).astype(BF16),
        "w_gate": jnp.concatenate([p["lru_wa"][l], p["lru_wx"][l]], axis=-1).astype(BF16),
        "pre_norm": row(p["pre_norm"][l]), "post_norm": row(p["post_norm"][l]),
        "conv_w": p["conv_w"][l], "conv_b": row(p["conv_b"][l]),
        "lru_ba": row(p["lru_ba"][l]), "lru_bx": row(p["lru_bx"][l]),
        "lru_lambda": row(p["lru_lambda"][l]),
        "q_norm": row(p["q_norm"][l]) * Q_SCALE, "kv_norm": row(p["kv_norm"][l]),
        "w_branch_a": p["w_branch_a"][l].astype(BF16), "w_branch_b": p["w_branch_b"][l].astype(BF16),
        "w_out": p["w_out"][l].astype(BF16),
    }


def _rope_tables(pos):
    half = QK_ROPE // 2
    inv = ROPE_THETA ** (-jnp.arange(half, dtype=F32) / half)
    ang = pos.astype(F32)[:, None] * inv[None, :]
    cos, sin = jnp.cos(ang), jnp.sin(ang)
    return jnp.concatenate([cos] * (LANES // half), axis=1), jnp.concatenate([sin] * (LANES // half), axis=1)


def _pick_tile(n, target):
    t = min(n, target)
    while n % t:
        t //= 2
    return t


def kernel(x_prompt, x_sample, c_prompt, c_sample, cache_ckv, cache_kpe, state_conv, state_lru, ada_w, ada_b, pre_norm, post_norm, w_in, conv_w, conv_b, lru_wa, lru_ba, lru_wx, lru_bx, lru_lambda, q_norm, w_q_up, kv_norm, w_uk, w_uv, w_branch_a, w_branch_b, w_out):
    p = dict(ada_w=ada_w, ada_b=ada_b, pre_norm=pre_norm, post_norm=post_norm, w_in=w_in,
             conv_w=conv_w, conv_b=conv_b, lru_wa=lru_wa, lru_ba=lru_ba, lru_wx=lru_wx, lru_bx=lru_bx,
             lru_lambda=lru_lambda, q_norm=q_norm, w_q_up=w_q_up, kv_norm=kv_norm, w_uk=w_uk,
             w_uv=w_uv, w_branch_a=w_branch_a, w_branch_b=w_branch_b, w_out=w_out)
    depth = w_in.shape[0]
    b_p, s_p, d = x_prompt.shape
    b_s, s_s, _ = x_sample.shape
    past_len = cache_ckv.shape[2]
    assert s_p % SUBLANES == 0 and s_s % SUBLANES == 0 and s_p >= SUBLANES and s_s >= SUBLANES

    mods = _modulation(jnp.concatenate([c_prompt, c_sample], axis=0), ada_w, ada_b)
    cos_p, sin_p = _rope_tables(jnp.arange(s_p, dtype=jnp.int32))
    cos_s, sin_s = _rope_tables(past_len + jnp.arange(s_s, dtype=jnp.int32))
    conv0_p = jnp.zeros((b_p, SUBLANES, d), F32)
    h0_p = jnp.zeros((b_p, 1, d), F32)

    t_in_p = _pick_tile(s_p, 256)
    t_out_p = _pick_tile(s_p, 512)
    tq_p = _pick_tile(s_p, 128)
    tk_p = _pick_tile(s_p, 512)
    tk_s = _pick_tile(past_len, 256)

    yp, ys = x_prompt, x_sample
    outs_p, outs_s = [], []
    for l in range(depth):
        lw = _layer_weights(l, p)

        mod_p = mods[l, :b_p, None, :]
        ya, sgb, sua, sub, q, ckv, kpe, kcat, convo, hlast, vt = _inproj(
            yp, mod_p, conv0_p, h0_p, cos_p, sin_p, lw, bb=1, t=t_in_p, emit_vt=True)
        yb = _attention_t(q, kcat, vt, sgb, lw["w_uv"], q_off=0, sk_valid=s_p, tq=tq_p, tk=tk_p)
        yp = _outproj(yp, mod_p, ya, yb, sua, sub, lw, bb=1, t=t_out_p)
        outs_p.append((ckv, kpe, convo[:, SUBLANES - (CONV_W - 1):, :], hlast[:, 0, :]))

        mod_s = mods[l, b_p:, None, :]
        conv0_s = jnp.pad(state_conv[l], ((0, 0), (SUBLANES - (CONV_W - 1), 0), (0, 0)))
        ya, sgb, sua, sub, q, ckv, kpe, kcat, convo, hlast = _inproj(
            ys, mod_s, conv0_s, state_lru[l][:, None, :], cos_s, sin_s, lw, bb=b_s, t=s_s,
            emit_vt=False)
        knew = jnp.pad(kcat, ((0, 0), (0, -s_s % LANES), (0, 0)))
        yb = _attention_cached(q, cache_ckv, cache_kpe, l, knew, sgb, lw["w_uv"], tk=tk_s)
        ys = _outproj(ys, mod_s, ya, yb, sua, sub, lw, bb=b_s, t=s_s)
        outs_s.append((ckv, kpe, convo[:, SUBLANES - (CONV_W - 1):, :], hlast[:, 0, :]))

    stack = lambda outs, k: jnp.stack([o[k] for o in outs])
    return (yp, ys,
            stack(outs_p, 0), stack(outs_p, 1), stack(outs_p, 2), stack(outs_p, 3),
            stack(outs_s, 0), stack(outs_s, 1), stack(outs_s, 2), stack(outs_s, 3))
```

```python
import functools
import math

import jax
import jax.numpy as jnp
from jax import lax
from jax.experimental import pallas as pl
from jax.experimental.pallas import tpu as pltpu

F32 = jnp.float32
BF16 = jnp.bfloat16

EPS = 1e-6
NEG_INF = -1e30
CHUNK = 64
LRU_C = 8.0
ROPE_THETA = 10000.0

LANES = 128
SUBLANES = 8
MXU_DIM = 256
VMEM_LIMIT_BYTES = 56 * 1024 * 1024

CONV_W = 4
N_HEADS = 8
QK_NOPE = 128
QK_ROPE = 64
V_DIM = 128
KV_RANK = 256
Q_RANK = 768
LRU_BLOCKS = 8
K_CAT = KV_RANK + LANES
Q_SCALE = (QK_NOPE + QK_ROPE) ** -0.5 * math.log2(math.e)


def _sigmoid(x):
    return jax.nn.sigmoid(x)


def _silu(x):
    return x * _sigmoid(x)


def _sqrt_nonneg(x):
    return jnp.where(x > 0.0, x * lax.rsqrt(x), 0.0)


def _rms(x, g):
    return x * lax.rsqrt(jnp.mean(x * x, axis=-1, keepdims=True) + EPS) * g


def _const_spec(shape):
    nd = len(shape)
    return pl.BlockSpec(shape, lambda *_: (0,) * nd, pipeline_mode=pl.Buffered(1))


def _mod_kernel(c_ref, w_ref, b_ref, o_ref):
    c = c_ref[...]
    o_ref[0] = jnp.dot(_silu(c), w_ref[0], preferred_element_type=F32,
                       precision=lax.Precision.HIGHEST) + b_ref[0]


def _modulation(c_all, ada_w, ada_b):
    depth, d, d3 = ada_w.shape
    n = c_all.shape[0]
    nblk = d3 // d
    return pl.pallas_call(
        _mod_kernel,
        grid=(depth, nblk),
        in_specs=[pl.BlockSpec((n, d), lambda l, j: (0, 0)),
                  pl.BlockSpec((1, d, d), lambda l, j: (l, 0, j)),
                  pl.BlockSpec((1, 1, d), lambda l, j: (l, 0, j))],
        out_specs=pl.BlockSpec((1, n, d), lambda l, j: (l, 0, j)),
        out_shape=jax.ShapeDtypeStruct((depth, n, d3), F32),
        name="adaln_mod",
    )(c_all, ada_w, ada_b.reshape(depth, 1, d3))


def _inproj_kernel(x_ref, mod_ref, conv0_ref, h0_ref, cos_ref, sin_ref,
                   w_in_ref, w_q_ref, w_uk_ref, w_gate_ref,
                   pre_norm_ref, conv_w_ref, conv_b_ref, ba_ref, bx_ref, lam_ref, qn_ref, kvn_ref,
                   ya_ref, sgb_ref, sua_ref, sub_ref, q_ref, ckv_ref, kpe_ref, kcat_ref,
                   convo_ref, hlast_ref, *rest, per_head):
    if per_head:
        vt_ref, buf_ref, hcar_ref, hb_ref, cqn_ref, v_scr = rest
    else:
        buf_ref, hcar_ref, hb_ref, cqn_ref = rest
    bb, t, d = x_ref.shape
    q_main = QK_NOPE if per_head else KV_RANK
    m = bb * t
    groups = t // SUBLANES
    gw = d // LRU_BLOCKS
    cw_ = MXU_DIM
    n_chunks = d // cw_
    blocks_per_chunk = cw_ // gw
    nq = N_HEADS * QK_NOPE
    npe = N_HEADS * QK_ROPE
    o_xa, o_ga, o_cq = 0, d, 2 * d
    o_ckv = o_cq + Q_RANK
    o_kpe = o_ckv + KV_RANK
    o_krot = o_kpe + LANES
    o_gb = o_krot + LANES
    o_ua = o_gb + d
    o_ub = o_ua + d

    @pl.when(pl.program_id(1) == 0)
    def _():
        buf_ref[:, 0:SUBLANES, :] = conv0_ref[...]
        hcar_ref[...] = h0_ref[...]

    x = x_ref[...]
    mod = mod_ref[...]
    shift = mod[:, :, 0:d]
    scale = mod[:, :, d:2 * d]
    h = _rms(x, pre_norm_ref[...]) * (1.0 + scale) + shift
    hb_ref[...] = h.reshape(m, d).astype(BF16)

    def proj(lo, width):
        return jnp.dot(hb_ref[...], w_in_ref[:, lo:lo + width], preferred_element_type=F32)

    cos = cos_ref[...][None]
    sin = sin_ref[...][None]

    def conv_phase(c):
        cs = slice(c * cw_, (c + 1) * cw_)
        xa = proj(o_xa + c * cw_, cw_)
        buf_ref[:, SUBLANES:SUBLANES + t, cs] = xa.reshape(bb, t, cw_)
        cw = conv_w_ref[:, cs]
        xc = conv_b_ref[:, cs] + buf_ref[:, SUBLANES - 3:SUBLANES - 3 + t, cs] * cw[0:1]
        xc = xc + buf_ref[:, SUBLANES - 2:SUBLANES - 2 + t, cs] * cw[1:2]
        xc = xc + buf_ref[:, SUBLANES - 1:SUBLANES - 1 + t, cs] * cw[2:3]
        xc = xc + buf_ref[:, SUBLANES:SUBLANES + t, cs] * cw[3:4]
        tail = buf_ref[:, t:t + SUBLANES, cs]
        convo_ref[:, :, cs] = tail
        buf_ref[:, 0:SUBLANES, cs] = tail
        return xc.reshape(m, cw_)

    def gate_phase(c, xc2):
        cs = slice(c * cw_, (c + 1) * cw_)
        xcb = xc2.astype(BF16)
        r_parts, i_parts = [], []
        for n in range(blocks_per_chunk):
            g = jnp.dot(xcb[:, n * gw:(n + 1) * gw], w_gate_ref[c * blocks_per_chunk + n],
                        preferred_element_type=F32)
            r_parts.append(g[:, 0:gw])
            i_parts.append(g[:, gw:2 * gw])
        r = _sigmoid(jnp.concatenate(r_parts, axis=1) + ba_ref[:, cs])
        ig = _sigmoid(jnp.concatenate(i_parts, axis=1) + bx_ref[:, cs])
        nl = -lam_ref[:, cs]
        softplus = jnp.maximum(nl, 0.0) + jnp.log1p(jnp.exp(-jnp.abs(nl)))
        th = jnp.tanh((-0.5 * LRU_C * r) * softplus)
        inv = 1.0 / (1.0 - th)
        a = (1.0 + th) * inv
        mult = (2.0 * _sqrt_nonneg(-th)) * inv
        return a, mult * (ig * xc2)

    def scan_phase(c, a, b):
        cs = slice(c * cw_, (c + 1) * cw_)
        a4 = a.reshape(bb * groups, SUBLANES, cw_)
        b4 = b.reshape(bb * groups, SUBLANES, cw_)
        row = lax.broadcasted_iota(jnp.int32, a4.shape, 1)
        for sft in (1, 2, 4):
            keep = row >= sft
            a_sh = jnp.where(keep, pltpu.roll(a4, sft, axis=1), 1.0)
            b_sh = jnp.where(keep, pltpu.roll(b4, sft, axis=1), 0.0)
            b4 = b4 + a4 * b_sh
            a4 = a4 * a_sh
        a5 = a4.reshape(bb, groups, SUBLANES, cw_)
        b5 = b4.reshape(bb, groups, SUBLANES, cw_)
        hc = hcar_ref[:, :, cs]
        outs = []
        for g in range(groups):
            hs = a5[:, g] * hc + b5[:, g]
            outs.append(hs)
            hc = hs[:, SUBLANES - 1:SUBLANES, :]
        hcar_ref[:, :, cs] = hc
        hlast_ref[:, :, cs] = hc
        return jnp.concatenate(outs, axis=1).reshape(m, cw_)

    def out_phase(c, y_lru):
        cs = slice(c * cw_, (c + 1) * cw_)
        ga = proj(o_ga + c * cw_, cw_)
        ya_ref[:, :, cs] = (y_lru * _silu(ga)).astype(BF16).reshape(bb, t, cw_)

    def gate_out(ref, fn, off, c):
        cs = slice(c * cw_, (c + 1) * cw_)
        ref[:, :, cs] = fn(proj(off + c * cw_, cw_)).astype(BF16).reshape(bb, t, cw_)

    def q_latent(heads):
        q_nope = jnp.dot(cqn_ref[...], w_q_ref[:, heads[0] * QK_NOPE:(heads[-1] + 1) * QK_NOPE],
                         preferred_element_type=F32).astype(BF16)
        for k, hd in enumerate(heads):
            q_h = q_nope[:, k * QK_NOPE:(k + 1) * QK_NOPE]
            if not per_head:
                q_h = jnp.dot(q_h, w_uk_ref[hd], preferred_element_type=F32).astype(BF16)
            q_ref[:, hd, :, 0:q_main] = q_h.reshape(bb, t, q_main)

    def q_rotary():
        q_pe = jnp.dot(cqn_ref[...], w_q_ref[:, nq:nq + npe], preferred_element_type=F32)
        q_rot = jnp.dot(cqn_ref[...], w_q_ref[:, nq + npe:nq + 2 * npe], preferred_element_type=F32)
        lane = lax.broadcasted_iota(jnp.int32, (1, 1, LANES), 2)
        for pair in range(N_HEADS // 2):
            sl = slice(pair * LANES, (pair + 1) * LANES)
            both = q_pe[:, sl].reshape(bb, t, LANES) * cos + q_rot[:, sl].reshape(bb, t, LANES) * sin
            rs = slice(q_main, q_main + LANES)
            q_ref[:, 2 * pair, :, rs] = jnp.where(lane < QK_ROPE, both, 0.0).astype(BF16)
            q_ref[:, 2 * pair + 1, :, rs] = jnp.where(lane < QK_ROPE, 0.0, both).astype(BF16)

    def key_latents():
        ckv = _rms(proj(o_ckv, KV_RANK), kvn_ref[...])
        ckv_ref[...] = ckv.reshape(bb, t, KV_RANK)
        ckvb = ckv.astype(BF16)
        kp = proj(o_kpe, LANES).reshape(bb, t, LANES)
        kr = proj(o_krot, LANES).reshape(bb, t, LANES)
        kroped = kp * cos + kr * sin
        kpe_ref[...] = kroped[:, :, 0:QK_ROPE]
        if per_head:
            nk = N_HEADS * QK_NOPE
            k_nope = jnp.dot(ckvb, w_uk_ref[:, 0:nk], preferred_element_type=F32).astype(BF16)
            v_scr[...] = jnp.dot(ckvb, w_uk_ref[:, nk:nk + N_HEADS * V_DIM], preferred_element_type=F32)
            v_t = v_scr[...].T
            for hd in range(N_HEADS):
                kcat_ref[:, hd, :, 0:QK_NOPE] = k_nope[:, hd * QK_NOPE:(hd + 1) * QK_NOPE].reshape(
                    bb, t, QK_NOPE)
                kcat_ref[:, hd, :, QK_NOPE:QK_NOPE + LANES] = kroped.astype(BF16)
                vt_ref[0, hd, 0] = v_t[hd * V_DIM:(hd + 1) * V_DIM].astype(BF16)
        else:
            kcat_ref[:, :, 0:KV_RANK] = ckvb.reshape(bb, t, KV_RANK)
            kcat_ref[:, :, KV_RANK:K_CAT] = kroped.astype(BF16)

    cqn_ref[...] = _rms(proj(o_cq, Q_RANK), qn_ref[...]).astype(BF16)
    other = [lambda: q_latent((0, 1, 2, 3)), lambda: q_latent((4, 5, 6, 7)), q_rotary, key_latents]
    for c in range(n_chunks):
        xc2 = conv_phase(c)
        gate_out(sgb_ref, _silu, o_gb, c)
        a, b = gate_phase(c, xc2)
        gate_out(sua_ref, _sigmoid, o_ua, c)
        y_lru = scan_phase(c, a, b)
        gate_out(sub_ref, _sigmoid, o_ub, c)
        out_phase(c, y_lru)
        if c < len(other):
            other[c]()
    for f in other[n_chunks:]:
        f()


def _inproj(x, mod, conv0, h0, cos, sin, lw, *, bb, t, per_head):
    b, s, d = x.shape
    grid = (b // bb, s // t)
    tok = lambda w: pl.BlockSpec((bb, t, w), lambda i, j: (i, j, 0))
    per_b = lambda r, w: pl.BlockSpec((bb, r, w), lambda i, j: (i, 0, 0))
    heads = lambda w: pl.BlockSpec((bb, N_HEADS, t, w), lambda i, j: (i, 0, j, 0))
    q_w = (QK_NOPE if per_head else KV_RANK) + LANES
    weights = [lw["w_in"], lw["w_q"], lw["w_ukv"] if per_head else lw["w_uk"], lw["w_gate"],
               lw["pre_norm"], lw["conv_w"],
               lw["conv_b"], lw["lru_ba"], lw["lru_bx"], lw["lru_lambda"], lw["q_norm"], lw["kv_norm"]]
    in_specs = ([tok(d), per_b(1, 3 * d), per_b(SUBLANES, d), per_b(1, d),
                 pl.BlockSpec((t, LANES), lambda i, j: (j, 0)),
                 pl.BlockSpec((t, LANES), lambda i, j: (j, 0))]
                + [_const_spec(w.shape) for w in weights])
    out_shape = [jax.ShapeDtypeStruct((b, s, d), BF16)] * 4 + [
        jax.ShapeDtypeStruct((b, N_HEADS, s, q_w), BF16),
        jax.ShapeDtypeStruct((b, s, KV_RANK), F32),
        jax.ShapeDtypeStruct((b, s, QK_ROPE), F32),
        jax.ShapeDtypeStruct((b, N_HEADS, s, q_w) if per_head else (b, s, K_CAT), BF16),
        jax.ShapeDtypeStruct((b, SUBLANES, d), F32),
        jax.ShapeDtypeStruct((b, 1, d), F32),
    ]
    out_specs = [tok(d)] * 4 + [
        heads(q_w), tok(KV_RANK), tok(QK_ROPE), heads(q_w) if per_head else tok(K_CAT),
        per_b(SUBLANES, d), per_b(1, d),
    ]
    if per_head:
        assert bb == 1
        out_shape.append(jax.ShapeDtypeStruct((b, N_HEADS, s // t, V_DIM, t), BF16))
        out_specs.append(pl.BlockSpec((1, N_HEADS, 1, V_DIM, t), lambda i, j: (i, 0, j, 0, 0)))
    return pl.pallas_call(
        functools.partial(_inproj_kernel, per_head=per_head),
        grid=grid,
        in_specs=in_specs,
        out_specs=out_specs,
        out_shape=out_shape,
        scratch_shapes=[pltpu.VMEM((bb, t + SUBLANES, d), F32), pltpu.VMEM((bb, 1, d), F32),
                        pltpu.VMEM((bb * t, d), BF16), pltpu.VMEM((bb * t, Q_RANK), BF16)]
        + ([pltpu.VMEM((bb * t, N_HEADS * V_DIM), F32)] if per_head else []),
        compiler_params=pltpu.CompilerParams(
            dimension_semantics=("parallel", "arbitrary"), vmem_limit_bytes=VMEM_LIMIT_BYTES),
        name="inproj_lru",
    )(x, mod, conv0, h0, cos, sin, *weights)


def _visible_tiles(q_first, tq, tk, sk_valid):
    vis_first = jnp.minimum((q_first // CHUNK + 1) * CHUNK, sk_valid)
    vis_last = jnp.minimum(((q_first + tq - 1) // CHUNK + 1) * CHUNK, sk_valid)
    return vis_first // tk, (vis_last + tk - 1) // tk


def _attn_heads_kernel(q_ref, k_ref, vt_ref, sgb_ref, yb_ref, m_scr, l_scr, acc_scr, s_scr,
                       *, sk_valid, tq, tk):
    n_groups = q_ref.shape[1]
    gran = vt_ref.shape[-1]
    q_first = pl.program_id(2) * tq
    _, n_tiles = _visible_tiles(q_first, tq, tk, sk_valid)

    m_scr[...] = jnp.full(m_scr.shape, NEG_INF, F32)
    l_scr[...] = jnp.zeros(l_scr.shape, F32)
    acc_scr[...] = jnp.zeros(acc_scr.shape, F32)

    def qk(j, g):
        k = k_ref[0, g, pl.ds(pl.multiple_of(j * tk, tk), tk), :]
        return lax.dot_general(k, q_ref[0, g], (((1,), (1,)), ((), ())),
                               preferred_element_type=F32)

    s_scr[0] = qk(0, 0)

    def step(j, masked, last_tile):
        for g in range(n_groups):
            if g + 1 < n_groups:
                s_scr[g + 1] = qk(j, g + 1)
            elif not last_tile:
                s_scr[0] = qk(j + 1, 0)
            s = s_scr[g]
            if masked:
                q_pos = q_first + lax.broadcasted_iota(jnp.int32, (1, tq), 1)
                k_lim = jnp.minimum((q_pos // CHUNK + 1) * CHUNK, sk_valid)
                k_pos = j * tk + lax.broadcasted_iota(jnp.int32, (tk, tq), 0)
                s = jnp.where(k_pos < k_lim, s, NEG_INF)
            m_prev = m_scr[g]
            m_next = jnp.maximum(m_prev, jnp.max(s, axis=0, keepdims=True))
            p = jnp.exp2(s - m_next)
            alpha = jnp.exp2(m_prev - m_next)
            l_scr[g] = alpha * l_scr[g] + jnp.sum(p, axis=0, keepdims=True)
            m_scr[g] = m_next
            vt = jnp.concatenate([vt_ref[0, g, j * (tk // gran) + i] for i in range(tk // gran)], axis=1)
            pv = jnp.dot(vt, p.astype(BF16), preferred_element_type=F32)
            acc_scr[g] = acc_scr[g] * alpha + pv

    last = n_tiles - 1

    def body(jj, c):
        step(2 * jj, False, False)
        step(2 * jj + 1, False, False)
        return c

    lax.fori_loop(0, last // 2, body, 0)

    @pl.when(last % 2 == 1)
    def _():
        step(last - 1, False, False)

    step(last, True, True)

    sgb = sgb_ref[0]
    for g in range(n_groups):
        o = (acc_scr[g] * (1.0 / l_scr[g])).T
        sl = slice(g * V_DIM, (g + 1) * V_DIM)
        yb_ref[0, :, sl] = (o * sgb[:, sl]).astype(BF16)


def _attention_heads(q, k, vt, sgb, *, tq, tk, heads_per_step=2):
    b, n_heads, sq, qw = q.shape
    sk = k.shape[2]
    d = sgb.shape[-1]
    gran = vt.shape[-1]
    hps = heads_per_step
    assert n_heads % hps == 0 and sq % tq == 0 and sk % tk == 0 and tk % gran == 0
    assert vt.shape[2] * gran == sk and (hps * V_DIM) % LANES == 0
    assert tk % tq == 0 and tq % CHUNK == 0 and sq == sk
    kern = functools.partial(_attn_heads_kernel, sk_valid=sk, tq=tq, tk=tk)
    return pl.pallas_call(
        kern,
        grid=(b, n_heads // hps, sq // tq),
        in_specs=[pl.BlockSpec((1, hps, tq, qw), lambda i, h, j: (i, h, j, 0)),
                  pl.BlockSpec((1, hps, sk, qw), lambda i, h, j: (i, h, 0, 0)),
                  pl.BlockSpec((1, hps) + vt.shape[2:], lambda i, h, j: (i, h, 0, 0, 0)),
                  pl.BlockSpec((1, tq, hps * V_DIM), lambda i, h, j: (i, j, h))],
        out_specs=pl.BlockSpec((1, tq, hps * V_DIM), lambda i, h, j: (i, j, h)),
        out_shape=jax.ShapeDtypeStruct((b, sq, d), BF16),
        scratch_shapes=[pltpu.VMEM((hps, 1, tq), F32), pltpu.VMEM((hps, 1, tq), F32),
                        pltpu.VMEM((hps, V_DIM, tq), F32), pltpu.VMEM((hps, tk, tq), F32)],
        compiler_params=pltpu.CompilerParams(
            dimension_semantics=("parallel", "parallel", "arbitrary"),
            vmem_limit_bytes=VMEM_LIMIT_BYTES),
        name="mla_attention_heads",
    )(q, k, vt, sgb)


def _attn_cached_kernel(q_ref, ckv_ref, kpe_ref, knew_ref, sgb_ref, w_uv_ref, yb_ref,
                        m_scr, l_scr, acc_scr, *, past_len, sq, tk):
    m = N_HEADS * sq
    q = q_ref[0].reshape(m, K_CAT)
    q_lat = q[:, 0:KV_RANK]
    q_rope = q[:, KV_RANK:K_CAT].astype(F32)
    q_rope = (q_rope[:, 0:QK_ROPE] + q_rope[:, QK_ROPE:LANES]).astype(BF16)
    q_pos = past_len + (lax.broadcasted_iota(jnp.int32, (m, 1), 0) & (sq - 1))
    k_lim = (q_pos // CHUNK + 1) * CHUNK

    m_scr[...] = jnp.full(m_scr.shape, NEG_INF, F32)
    l_scr[...] = jnp.zeros(l_scr.shape, F32)
    acc_scr[...] = jnp.zeros(acc_scr.shape, F32)
    nt = (((1,), (1,)), ((), ()))

    def update(s, visible, v):
        s = jnp.where(visible, s, NEG_INF)
        m_prev = m_scr[...]
        m_next = jnp.maximum(m_prev, jnp.max(s, axis=-1, keepdims=True))
        p = jnp.exp2(s - jnp.concatenate([m_next] * (s.shape[1] // LANES), axis=1))
        alpha = jnp.exp2(m_prev - m_next)
        l_scr[...] = alpha * l_scr[...] + jnp.sum(p, axis=-1, keepdims=True)
        m_scr[...] = m_next
        pv = jnp.dot(p.astype(BF16), v, preferred_element_type=F32)
        acc_scr[...] = acc_scr[...] * jnp.concatenate([alpha] * (KV_RANK // LANES), axis=1) + pv

    def past_body(j, c):
        off = pl.multiple_of(j * tk, tk)
        lat = ckv_ref[0, 0, pl.ds(off, tk), :].astype(BF16)
        pe = kpe_ref[0, 0, pl.ds(off, tk), :].astype(BF16)
        s = (lax.dot_general(q_lat, lat, nt, preferred_element_type=F32)
             + lax.dot_general(q_rope, pe, nt, preferred_element_type=F32))
        k_pos = off + lax.broadcasted_iota(jnp.int32, (m, tk), 1)
        update(s, k_pos < k_lim, lat)
        return c

    lax.fori_loop(0, past_len // tk, past_body, 0)

    kn = knew_ref[0]
    s = lax.dot_general(q, kn, nt, preferred_element_type=F32)
    idx = lax.broadcasted_iota(jnp.int32, (m, kn.shape[0]), 1)
    update(s, jnp.where(idx < sq, past_len + idx, k_lim) < k_lim, kn[:, 0:KV_RANK])

    inv_l = 1.0 / l_scr[...]
    o = (acc_scr[...] * jnp.concatenate([inv_l] * (KV_RANK // LANES), axis=1)).astype(BF16)
    sgb = sgb_ref[0]
    for hd in range(N_HEADS):
        att = jnp.dot(o[hd * sq:(hd + 1) * sq], w_uv_ref[hd], preferred_element_type=F32)
        sl = slice(hd * V_DIM, (hd + 1) * V_DIM)
        yb_ref[0, :, sl] = (att * sgb[:, sl]).astype(BF16)


def _attention_cached(q, cache_ckv, cache_kpe, layer, knew, sgb, w_uv, *, tk):
    b, _, sq, _ = q.shape
    past_len = cache_ckv.shape[2]
    d = sgb.shape[-1]
    m = N_HEADS * sq
    assert sq & (sq - 1) == 0 and past_len % tk == 0 and tk % LANES == 0 and knew.shape[1] % LANES == 0
    kern = functools.partial(_attn_cached_kernel, past_len=past_len, sq=sq, tk=tk)
    return pl.pallas_call(
        kern,
        grid=(b,),
        in_specs=[pl.BlockSpec((1, N_HEADS, sq, K_CAT), lambda i: (i, 0, 0, 0)),
                  pl.BlockSpec((1, 1, past_len, KV_RANK), lambda i: (layer, i, 0, 0)),
                  pl.BlockSpec((1, 1, past_len, QK_ROPE), lambda i: (layer, i, 0, 0)),
                  pl.BlockSpec((1,) + knew.shape[1:], lambda i: (i, 0, 0)),
                  pl.BlockSpec((1, sq, d), lambda i: (i, 0, 0)),
                  _const_spec(w_uv.shape)],
        out_specs=pl.BlockSpec((1, sq, d), lambda i: (i, 0, 0)),
        out_shape=jax.ShapeDtypeStruct((b, sq, d), BF16),
        scratch_shapes=[pltpu.VMEM((m, LANES), F32), pltpu.VMEM((m, LANES), F32),
                        pltpu.VMEM((m, KV_RANK), F32)],
        compiler_params=pltpu.CompilerParams(
            dimension_semantics=("parallel",), vmem_limit_bytes=VMEM_LIMIT_BYTES),
        name="mla_attention_cached",
    )(q, cache_ckv, cache_kpe, knew, sgb, w_uv)


def _outproj_kernel(x_ref, mod_ref, ya_ref, yb_ref, sua_ref, sub_ref,
                    w_a_ref, w_b_ref, w_o_ref, post_norm_ref, y_ref):
    bb, t, d = x_ref.shape
    m = bb * t
    ya = ya_ref[...].reshape(m, d)
    yb = yb_ref[...].reshape(m, d)
    pa = jnp.dot(ya, w_a_ref[...], preferred_element_type=F32)
    pb = jnp.dot(yb, w_b_ref[...], preferred_element_type=F32)
    merged = sua_ref[...].reshape(m, d).astype(F32) * pa + sub_ref[...].reshape(m, d).astype(F32) * pb
    o = jnp.dot(merged.astype(BF16), w_o_ref[...], preferred_element_type=F32)
    gate = mod_ref[...][:, :, 2 * d:3 * d]
    y_ref[...] = x_ref[...] + gate * _rms(o, post_norm_ref[...]).reshape(bb, t, d)


def _outproj(x, mod, ya, yb, sua, sub, lw, *, bb, t):
    b, s, d = x.shape
    tok = pl.BlockSpec((bb, t, d), lambda i, j: (i, j, 0))
    weights = [lw["w_branch_a"], lw["w_branch_b"], lw["w_out"], lw["post_norm"]]
    return pl.pallas_call(
        _outproj_kernel,
        grid=(b // bb, s // t),
        in_specs=[tok, pl.BlockSpec((bb, 1, 3 * d), lambda i, j: (i, 0, 0)), tok, tok, tok, tok]
        + [_const_spec(w.shape) for w in weights],
        out_specs=tok,
        out_shape=jax.ShapeDtypeStruct((b, s, d), F32),
        compiler_params=pltpu.CompilerParams(
            dimension_semantics=("parallel", "parallel"), vmem_limit_bytes=VMEM_LIMIT_BYTES),
        name="outproj",
    )(x, mod, ya, yb, sua, sub, *weights)


def _rot_half(w):
    half = w.shape[-1] // 2
    return jnp.concatenate([-w[..., half:], w[..., :half]], axis=-1)


def _twice(w):
    return jnp.concatenate([w, w], axis=-1)


def _layer_weights(l, p):
    d = p["w_in"].shape[1]
    w_in = p["w_in"][l]
    o = 0
    seg = {}
    for name, width in (("xa", d), ("ga", d), ("cq", Q_RANK), ("ckv", KV_RANK), ("kpe", QK_ROPE),
                        ("gb", d), ("ua", d), ("ub", d)):
        seg[name] = w_in[:, o:o + width]
        o += width
    w_in_ext = jnp.concatenate(
        [seg["xa"], seg["ga"], seg["cq"], seg["ckv"], _twice(seg["kpe"]),
         _twice(_rot_half(seg["kpe"])), seg["gb"], seg["ua"], seg["ub"]], axis=1)
    wq = p["w_q_up"][l].reshape(Q_RANK, N_HEADS, QK_NOPE + QK_ROPE)
    wq_pe = wq[:, :, QK_NOPE:]
    w_q = jnp.concatenate(
        [wq[:, :, :QK_NOPE].reshape(Q_RANK, -1), wq_pe.reshape(Q_RANK, -1),
         _rot_half(wq_pe).reshape(Q_RANK, -1)], axis=1)
    row = lambda v: v.reshape(1, -1)
    return {
        "w_in": w_in_ext.astype(BF16),
        "w_q": w_q.astype(BF16),
        "w_uk": jnp.transpose(p["w_uk"][l], (1, 2, 0)).astype(BF16),
        "w_ukv": jnp.concatenate([p["w_uk"][l].reshape(KV_RANK, -1), p["w_uv"][l].reshape(KV_RANK, -1)],
                                 axis=1).astype(BF16),
        "w_uv": jnp.transpose(p["w_uv"][l], (1, 0, 2)).astype(BF16),
        "w_gate": jnp.concatenate([p["lru_wa"][l], p["lru_wx"][l]], axis=-1).astype(BF16),
        "pre_norm": row(p["pre_norm"][l]), "post_norm": row(p["post_norm"][l]),
        "conv_w": p["conv_w"][l], "conv_b": row(p["conv_b"][l]),
        "lru_ba": row(p["lru_ba"][l]), "lru_bx": row(p["lru_bx"][l]),
        "lru_lambda": row(p["lru_lambda"][l]),
        "q_norm": row(p["q_norm"][l]) * Q_SCALE, "kv_norm": row(p["kv_norm"][l]),
        "w_branch_a": p["w_branch_a"][l].astype(BF16), "w_branch_b": p["w_branch_b"][l].astype(BF16),
        "w_out": p["w_out"][l].astype(BF16),
    }


def _rope_tables(pos):
    half = QK_ROPE // 2
    inv = ROPE_THETA ** (-jnp.arange(half, dtype=F32) / half)
    ang = pos.astype(F32)[:, None] * inv[None, :]
    cos, sin = jnp.cos(ang), jnp.sin(ang)
    return jnp.concatenate([cos] * (LANES // half), axis=1), jnp.concatenate([sin] * (LANES // half), axis=1)


def _pick_tile(n, target):
    t = min(n, target)
    while n % t:
        t //= 2
    return t


def kernel(x_prompt, x_sample, c_prompt, c_sample, cache_ckv, cache_kpe, state_conv, state_lru, ada_w, ada_b, pre_norm, post_norm, w_in, conv_w, conv_b, lru_wa, lru_ba, lru_wx, lru_bx, lru_lambda, q_norm, w_q_up, kv_norm, w_uk, w_uv, w_branch_a, w_branch_b, w_out):
    p = dict(ada_w=ada_w, ada_b=ada_b, pre_norm=pre_norm, post_norm=post_norm, w_in=w_in,
             conv_w=conv_w, conv_b=conv_b, lru_wa=lru_wa, lru_ba=lru_ba, lru_wx=lru_wx, lru_bx=lru_bx,
             lru_lambda=lru_lambda, q_norm=q_norm, w_q_up=w_q_up, kv_norm=kv_norm, w_uk=w_uk,
             w_uv=w_uv, w_branch_a=w_branch_a, w_branch_b=w_branch_b, w_out=w_out)
    depth = w_in.shape[0]
    b_p, s_p, d = x_prompt.shape
    b_s, s_s, _ = x_sample.shape
    past_len = cache_ckv.shape[2]
    assert s_p % SUBLANES == 0 and s_s % SUBLANES == 0 and s_p >= SUBLANES and s_s >= SUBLANES

    mods = _modulation(jnp.concatenate([c_prompt, c_sample], axis=0), ada_w, ada_b)
    cos_p, sin_p = _rope_tables(jnp.arange(s_p, dtype=jnp.int32))
    cos_s, sin_s = _rope_tables(past_len + jnp.arange(s_s, dtype=jnp.int32))
    conv0_p = jnp.zeros((b_p, SUBLANES, d), F32)
    h0_p = jnp.zeros((b_p, 1, d), F32)

    t_in_p = _pick_tile(s_p, 256)
    t_out_p = _pick_tile(s_p, 512)
    tq_p = _pick_tile(s_p, 512)
    tk_p = _pick_tile(s_p, 512)
    tk_s = _pick_tile(past_len, 256)

    yp, ys = x_prompt, x_sample
    outs_p, outs_s = [], []
    for l in range(depth):
        lw = _layer_weights(l, p)

        mod_p = mods[l, :b_p, None, :]
        ya, sgb, sua, sub, q, ckv, kpe, k_heads, convo, hlast, vt = _inproj(
            yp, mod_p, conv0_p, h0_p, cos_p, sin_p, lw, bb=1, t=t_in_p, per_head=True)
        yb = _attention_heads(q, k_heads, vt, sgb, tq=tq_p, tk=tk_p)
        yp = _outproj(yp, mod_p, ya, yb, sua, sub, lw, bb=1, t=t_out_p)
        outs_p.append((ckv, kpe, convo[:, SUBLANES - (CONV_W - 1):, :], hlast[:, 0, :]))

        mod_s = mods[l, b_p:, None, :]
        conv0_s = jnp.pad(state_conv[l], ((0, 0), (SUBLANES - (CONV_W - 1), 0), (0, 0)))
        ya, sgb, sua, sub, q, ckv, kpe, kcat, convo, hlast = _inproj(
            ys, mod_s, conv0_s, state_lru[l][:, None, :], cos_s, sin_s, lw, bb=b_s, t=s_s,
            per_head=False)
        knew = jnp.pad(kcat, ((0, 0), (0, -s_s % LANES), (0, 0)))
        yb = _attention_cached(q, cache_ckv, cache_kpe, l, knew, sgb, lw["w_uv"], tk=tk_s)
        ys = _outproj(ys, mod_s, ya, yb, sua, sub, lw, bb=b_s, t=s_s)
        outs_s.append((ckv, kpe, convo[:, SUBLANES - (CONV_W - 1):, :], hlast[:, 0, :]))

    stack = lambda outs, k: jnp.stack([o[k] for o in outs])
    return (yp, ys,
            stack(outs_p, 0), stack(outs_p, 1), stack(outs_p, 2), stack(outs_p, 3),
            stack(outs_s, 0), stack(outs_s, 1), stack(outs_s, 2), stack(outs_s, 3))
```

```python
import functools
import math

import jax
import jax.numpy as jnp
from jax import lax
from jax.experimental import pallas as pl
from jax.experimental.pallas import tpu as pltpu

F32 = jnp.float32
BF16 = jnp.bfloat16

EPS = 1e-6
NEG_INF = -1e30
CHUNK = 64
LRU_C = 8.0
ROPE_THETA = 10000.0

LANES = 128
SUBLANES = 8
MXU_DIM = 256
VMEM_LIMIT_BYTES = 56 * 1024 * 1024

CONV_W = 4
N_HEADS = 8
QK_NOPE = 128
QK_ROPE = 64
V_DIM = 128
KV_RANK = 256
Q_RANK = 768
LRU_BLOCKS = 8
K_CAT = KV_RANK + LANES
Q_SCALE = (QK_NOPE + QK_ROPE) ** -0.5 * math.log2(math.e)


def _sigmoid(x):
    return jax.nn.sigmoid(x)


def _silu(x):
    return x * _sigmoid(x)


def _sqrt_nonneg(x):
    return jnp.where(x > 0.0, x * lax.rsqrt(x), 0.0)


def _rms(x, g):
    return x * lax.rsqrt(jnp.mean(x * x, axis=-1, keepdims=True) + EPS) * g


def _const_spec(shape):
    nd = len(shape)
    return pl.BlockSpec(shape, lambda *_: (0,) * nd, pipeline_mode=pl.Buffered(1))


def _mod_kernel(c_ref, w_ref, b_ref, o_ref):
    c = c_ref[...]
    o_ref[0] = jnp.dot(_silu(c), w_ref[0], preferred_element_type=F32,
                       precision=lax.Precision.HIGHEST) + b_ref[0]


def _modulation(c_all, ada_w, ada_b):
    depth, d, d3 = ada_w.shape
    n = c_all.shape[0]
    nblk = d3 // d
    return pl.pallas_call(
        _mod_kernel,
        grid=(depth, nblk),
        in_specs=[pl.BlockSpec((n, d), lambda l, j: (0, 0)),
                  pl.BlockSpec((1, d, d), lambda l, j: (l, 0, j)),
                  pl.BlockSpec((1, 1, d), lambda l, j: (l, 0, j))],
        out_specs=pl.BlockSpec((1, n, d), lambda l, j: (l, 0, j)),
        out_shape=jax.ShapeDtypeStruct((depth, n, d3), F32),
        name="adaln_mod",
    )(c_all, ada_w, ada_b.reshape(depth, 1, d3))


def _inproj_kernel(x_ref, mod_ref, conv0_ref, h0_ref, cos_ref, sin_ref,
                   w_a_ref, w_k_ref, w_g_ref, w_q_ref, w_uk_ref, w_gate_ref,
                   pre_norm_ref, conv_w_ref, conv_b_ref, ba_ref, bx_ref, lam_ref, qn_ref, kvn_ref,
                   ya_ref, sgb_ref, sua_ref, sub_ref, q_ref, ckv_ref, kpe_ref, kcat_ref,
                   convo_ref, hlast_ref, *rest, per_head):
    if per_head:
        vt_ref, buf_ref, hcar_ref, hb_ref, cqn_ref, v_scr = rest
    else:
        buf_ref, hcar_ref, hb_ref, cqn_ref = rest
    bb, t, d = x_ref.shape
    q_main = QK_NOPE if per_head else KV_RANK
    m = bb * t
    groups = t // SUBLANES
    gw = d // LRU_BLOCKS
    cw_ = MXU_DIM
    n_chunks = d // cw_
    blocks_per_chunk = cw_ // gw
    nq = N_HEADS * QK_NOPE
    npe = N_HEADS * QK_ROPE
    o_xa, o_ga, o_cq, o_ckv = (w_a_ref, 0), (w_a_ref, d), (w_a_ref, 2 * d), (w_a_ref, 2 * d + Q_RANK)
    o_kpe, o_krot = (w_k_ref, 0), (w_k_ref, LANES)
    o_gb, o_ua, o_ub = (w_g_ref, 0), (w_g_ref, d), (w_g_ref, 2 * d)

    @pl.when(pl.program_id(1) == 0)
    def _():
        buf_ref[:, 0:SUBLANES, :] = conv0_ref[...]
        hcar_ref[...] = h0_ref[...]

    x = x_ref[...]
    mod = mod_ref[...]
    shift = mod[:, :, 0:d]
    scale = mod[:, :, d:2 * d]
    h = _rms(x, pre_norm_ref[...]) * (1.0 + scale) + shift
    hb_ref[...] = h.reshape(m, d).astype(BF16)

    def proj(seg, width, shift=0):
        w_ref, lo = seg
        return jnp.dot(hb_ref[...], w_ref[:, lo + shift:lo + shift + width], preferred_element_type=F32)

    cos = cos_ref[...][None]
    sin = sin_ref[...][None]

    def conv_phase(c):
        cs = slice(c * cw_, (c + 1) * cw_)
        xa = proj(o_xa, cw_, c * cw_)
        buf_ref[:, SUBLANES:SUBLANES + t, cs] = xa.reshape(bb, t, cw_)
        cw = conv_w_ref[:, cs]
        xc = conv_b_ref[:, cs] + buf_ref[:, SUBLANES - 3:SUBLANES - 3 + t, cs] * cw[0:1]
        xc = xc + buf_ref[:, SUBLANES - 2:SUBLANES - 2 + t, cs] * cw[1:2]
        xc = xc + buf_ref[:, SUBLANES - 1:SUBLANES - 1 + t, cs] * cw[2:3]
        xc = xc + buf_ref[:, SUBLANES:SUBLANES + t, cs] * cw[3:4]
        tail = buf_ref[:, t:t + SUBLANES, cs]
        convo_ref[:, :, cs] = tail
        buf_ref[:, 0:SUBLANES, cs] = tail
        return xc.reshape(m, cw_)

    def gate_phase(c, xc2):
        cs = slice(c * cw_, (c + 1) * cw_)
        xcb = xc2.astype(BF16)
        r_parts, i_parts = [], []
        for n in range(blocks_per_chunk):
            g = jnp.dot(xcb[:, n * gw:(n + 1) * gw], w_gate_ref[c * blocks_per_chunk + n],
                        preferred_element_type=F32)
            r_parts.append(g[:, 0:gw])
            i_parts.append(g[:, gw:2 * gw])
        r = _sigmoid(jnp.concatenate(r_parts, axis=1) + ba_ref[:, cs])
        ig = _sigmoid(jnp.concatenate(i_parts, axis=1) + bx_ref[:, cs])
        nl = -lam_ref[:, cs]
        softplus = jnp.maximum(nl, 0.0) + jnp.log1p(jnp.exp(-jnp.abs(nl)))
        th = jnp.tanh((-0.5 * LRU_C * r) * softplus)
        inv = 1.0 / (1.0 - th)
        a = (1.0 + th) * inv
        mult = (2.0 * _sqrt_nonneg(-th)) * inv
        return a, mult * (ig * xc2)

    def scan_phase(c, a, b):
        cs = slice(c * cw_, (c + 1) * cw_)
        a4 = a.reshape(bb * groups, SUBLANES, cw_)
        b4 = b.reshape(bb * groups, SUBLANES, cw_)
        row = lax.broadcasted_iota(jnp.int32, a4.shape, 1)
        for sft in (1, 2, 4):
            keep = row >= sft
            a_sh = jnp.where(keep, pltpu.roll(a4, sft, axis=1), 1.0)
            b_sh = jnp.where(keep, pltpu.roll(b4, sft, axis=1), 0.0)
            b4 = b4 + a4 * b_sh
            a4 = a4 * a_sh
        a5 = a4.reshape(bb, groups, SUBLANES, cw_)
        b5 = b4.reshape(bb, groups, SUBLANES, cw_)
        hc = hcar_ref[:, :, cs]
        outs = []
        for g in range(groups):
            hs = a5[:, g] * hc + b5[:, g]
            outs.append(hs)
            hc = hs[:, SUBLANES - 1:SUBLANES, :]
        hcar_ref[:, :, cs] = hc
        hlast_ref[:, :, cs] = hc
        return jnp.concatenate(outs, axis=1).reshape(m, cw_)

    def out_phase(c, y_lru):
        cs = slice(c * cw_, (c + 1) * cw_)
        ga = proj(o_ga, cw_, c * cw_)
        ya_ref[:, :, cs] = (y_lru * _silu(ga)).astype(BF16).reshape(bb, t, cw_)

    def gate_out(ref, fn, off, c):
        cs = slice(c * cw_, (c + 1) * cw_)
        ref[:, :, cs] = fn(proj(off, cw_, c * cw_)).astype(BF16).reshape(bb, t, cw_)

    def q_latent(heads):
        q_nope = jnp.dot(cqn_ref[...], w_q_ref[:, heads[0] * QK_NOPE:(heads[-1] + 1) * QK_NOPE],
                         preferred_element_type=F32).astype(BF16)
        for k, hd in enumerate(heads):
            q_h = q_nope[:, k * QK_NOPE:(k + 1) * QK_NOPE]
            if not per_head:
                q_h = jnp.dot(q_h, w_uk_ref[hd], preferred_element_type=F32).astype(BF16)
            q_ref[:, hd, :, 0:q_main] = q_h.reshape(bb, t, q_main)

    def q_rotary():
        q_pe = jnp.dot(cqn_ref[...], w_q_ref[:, nq:nq + npe], preferred_element_type=F32)
        q_rot = jnp.dot(cqn_ref[...], w_q_ref[:, nq + npe:nq + 2 * npe], preferred_element_type=F32)
        lane = lax.broadcasted_iota(jnp.int32, (1, 1, LANES), 2)
        for pair in range(N_HEADS // 2):
            sl = slice(pair * LANES, (pair + 1) * LANES)
            both = q_pe[:, sl].reshape(bb, t, LANES) * cos + q_rot[:, sl].reshape(bb, t, LANES) * sin
            rs = slice(q_main, q_main + LANES)
            q_ref[:, 2 * pair, :, rs] = jnp.where(lane < QK_ROPE, both, 0.0).astype(BF16)
            q_ref[:, 2 * pair + 1, :, rs] = jnp.where(lane < QK_ROPE, 0.0, both).astype(BF16)

    def key_latents():
        ckv = _rms(proj(o_ckv, KV_RANK), kvn_ref[...])
        ckv_ref[...] = ckv.reshape(bb, t, KV_RANK)
        ckvb = ckv.astype(BF16)
        kp = proj(o_kpe, LANES).reshape(bb, t, LANES)
        kr = proj(o_krot, LANES).reshape(bb, t, LANES)
        kroped = kp * cos + kr * sin
        kpe_ref[...] = kroped[:, :, 0:QK_ROPE]
        if per_head:
            nk = N_HEADS * QK_NOPE
            k_nope = jnp.dot(ckvb, w_uk_ref[:, 0:nk], preferred_element_type=F32).astype(BF16)
            v_scr[...] = jnp.dot(ckvb, w_uk_ref[:, nk:nk + N_HEADS * V_DIM], preferred_element_type=F32)
            v_t = v_scr[...].T
            for hd in range(N_HEADS):
                kcat_ref[:, hd, :, 0:QK_NOPE] = k_nope[:, hd * QK_NOPE:(hd + 1) * QK_NOPE].reshape(
                    bb, t, QK_NOPE)
                kcat_ref[:, hd, :, QK_NOPE:QK_NOPE + LANES] = kroped.astype(BF16)
                vt_ref[0, hd, 0] = v_t[hd * V_DIM:(hd + 1) * V_DIM].astype(BF16)
        else:
            kcat_ref[:, :, 0:KV_RANK] = ckvb.reshape(bb, t, KV_RANK)
            kcat_ref[:, :, KV_RANK:K_CAT] = kroped.astype(BF16)

    cqn_ref[...] = _rms(proj(o_cq, Q_RANK), qn_ref[...]).astype(BF16)
    other = [lambda: q_latent((0, 1, 2, 3)), lambda: q_latent((4, 5, 6, 7)), q_rotary, key_latents]
    for c in range(n_chunks):
        xc2 = conv_phase(c)
        gate_out(sgb_ref, _silu, o_gb, c)
        a, b = gate_phase(c, xc2)
        gate_out(sua_ref, _sigmoid, o_ua, c)
        y_lru = scan_phase(c, a, b)
        gate_out(sub_ref, _sigmoid, o_ub, c)
        out_phase(c, y_lru)
        if c < len(other):
            other[c]()
    for f in other[n_chunks:]:
        f()


def _inproj(x, mod, conv0, h0, cos, sin, lw, *, bb, t, per_head):
    b, s, d = x.shape
    grid = (b // bb, s // t)
    tok = lambda w: pl.BlockSpec((bb, t, w), lambda i, j: (i, j, 0))
    per_b = lambda r, w: pl.BlockSpec((bb, r, w), lambda i, j: (i, 0, 0))
    heads = lambda w: pl.BlockSpec((bb, N_HEADS, t, w), lambda i, j: (i, 0, j, 0))
    q_w = (QK_NOPE if per_head else KV_RANK) + LANES
    weights = [lw["w_in_a"], lw["w_in_k"], lw["w_in_g"], lw["w_q"],
               lw["w_ukv"] if per_head else lw["w_uk"], lw["w_gate"],
               lw["pre_norm"], lw["conv_w"],
               lw["conv_b"], lw["lru_ba"], lw["lru_bx"], lw["lru_lambda"], lw["q_norm"], lw["kv_norm"]]
    in_specs = ([tok(d), per_b(1, 3 * d), per_b(SUBLANES, d), per_b(1, d),
                 pl.BlockSpec((t, LANES), lambda i, j: (j, 0)),
                 pl.BlockSpec((t, LANES), lambda i, j: (j, 0))]
                + [_const_spec(w.shape) for w in weights])
    out_shape = [jax.ShapeDtypeStruct((b, s, d), BF16)] * 4 + [
        jax.ShapeDtypeStruct((b, N_HEADS, s, q_w), BF16),
        jax.ShapeDtypeStruct((b, s, KV_RANK), F32),
        jax.ShapeDtypeStruct((b, s, QK_ROPE), F32),
        jax.ShapeDtypeStruct((b, N_HEADS, s, q_w) if per_head else (b, s, K_CAT), BF16),
        jax.ShapeDtypeStruct((b, SUBLANES, d), F32),
        jax.ShapeDtypeStruct((b, 1, d), F32),
    ]
    out_specs = [tok(d)] * 4 + [
        heads(q_w), tok(KV_RANK), tok(QK_ROPE), heads(q_w) if per_head else tok(K_CAT),
        per_b(SUBLANES, d), per_b(1, d),
    ]
    if per_head:
        assert bb == 1
        out_shape.append(jax.ShapeDtypeStruct((b, N_HEADS, s // t, V_DIM, t), BF16))
        out_specs.append(pl.BlockSpec((1, N_HEADS, 1, V_DIM, t), lambda i, j: (i, 0, j, 0, 0)))
    return pl.pallas_call(
        functools.partial(_inproj_kernel, per_head=per_head),
        grid=grid,
        in_specs=in_specs,
        out_specs=out_specs,
        out_shape=out_shape,
        scratch_shapes=[pltpu.VMEM((bb, t + SUBLANES, d), F32), pltpu.VMEM((bb, 1, d), F32),
                        pltpu.VMEM((bb * t, d), BF16), pltpu.VMEM((bb * t, Q_RANK), BF16)]
        + ([pltpu.VMEM((bb * t, N_HEADS * V_DIM), F32)] if per_head else []),
        compiler_params=pltpu.CompilerParams(
            dimension_semantics=("parallel", "arbitrary"), vmem_limit_bytes=VMEM_LIMIT_BYTES),
        name="inproj_lru",
    )(x, mod, conv0, h0, cos, sin, *weights)


def _attn_heads_kernel(q_ref, k_ref, vt_ref, sgb_ref, bias_ref, yb_ref, m_scr, l_scr, acc_scr, s_scr,
                       *, tk):
    n_groups = q_ref.shape[1]
    gran = vt_ref.shape[-1]
    n_tiles = pl.program_id(2) + 1

    m_scr[...] = jnp.full(m_scr.shape, NEG_INF, F32)
    l_scr[...] = jnp.zeros(l_scr.shape, F32)
    acc_scr[...] = jnp.zeros(acc_scr.shape, F32)

    def qk(j, g):
        k = k_ref[0, g, pl.ds(pl.multiple_of(j * tk, tk), tk), :]
        return lax.dot_general(k, q_ref[0, g], (((1,), (1,)), ((), ())),
                               preferred_element_type=F32)

    s_scr[0] = qk(0, 0)

    def step(j, masked, last_tile):
        for g in range(n_groups):
            if g + 1 < n_groups:
                s_scr[g + 1] = qk(j, g + 1)
            elif not last_tile:
                s_scr[0] = qk(j + 1, 0)
            s = s_scr[g]
            if masked:
                s = s + bias_ref[...]
            m_prev = m_scr[g]
            m_next = jnp.maximum(m_prev, jnp.max(s, axis=0, keepdims=True))
            p = jnp.exp2(s - m_next)
            alpha = jnp.exp2(m_prev - m_next)
            l_scr[g] = alpha * l_scr[g] + jnp.sum(p, axis=0, keepdims=True)
            m_scr[g] = m_next
            vt = jnp.concatenate([vt_ref[0, g, j * (tk // gran) + i] for i in range(tk // gran)], axis=1)
            pv = jnp.dot(vt, p.astype(BF16), preferred_element_type=F32)
            acc_scr[g] = acc_scr[g] * alpha + pv

    last = n_tiles - 1

    def body(j, c):
        step(j, False, False)
        return c

    lax.fori_loop(0, last, body, 0)
    step(last, True, True)

    sgb = sgb_ref[0]
    for g in range(n_groups):
        o = (acc_scr[g] * (1.0 / l_scr[g])).T
        sl = slice(g * V_DIM, (g + 1) * V_DIM)
        yb_ref[0, :, sl] = (o * sgb[:, sl]).astype(BF16)


def _attention_heads(q, k, vt, sgb, *, tq, tk, heads_per_step=4):
    b, n_heads, sq, qw = q.shape
    sk = k.shape[2]
    d = sgb.shape[-1]
    gran = vt.shape[-1]
    hps = heads_per_step
    assert n_heads % hps == 0 and sq % tq == 0 and sk % tk == 0 and tk % gran == 0
    assert vt.shape[2] * gran == sk and (hps * V_DIM) % LANES == 0
    assert tk == tq and tq % CHUNK == 0 and sq == sk
    k_idx = lax.broadcasted_iota(jnp.int32, (tk, tq), 0)
    q_idx = lax.broadcasted_iota(jnp.int32, (tk, tq), 1)
    diag_bias = jnp.where(k_idx // CHUNK <= q_idx // CHUNK, 0.0, NEG_INF).astype(F32)
    kern = functools.partial(_attn_heads_kernel, tk=tk)
    return pl.pallas_call(
        kern,
        grid=(b, n_heads // hps, sq // tq),
        in_specs=[pl.BlockSpec((1, hps, tq, qw), lambda i, h, j: (i, h, j, 0)),
                  pl.BlockSpec((1, hps, sk, qw), lambda i, h, j: (i, h, 0, 0)),
                  pl.BlockSpec((1, hps) + vt.shape[2:], lambda i, h, j: (i, h, 0, 0, 0)),
                  pl.BlockSpec((1, tq, hps * V_DIM), lambda i, h, j: (i, j, h)),
                  _const_spec((tk, tq))],
        out_specs=pl.BlockSpec((1, tq, hps * V_DIM), lambda i, h, j: (i, j, h)),
        out_shape=jax.ShapeDtypeStruct((b, sq, d), BF16),
        scratch_shapes=[pltpu.VMEM((hps, 1, tq), F32), pltpu.VMEM((hps, 1, tq), F32),
                        pltpu.VMEM((hps, V_DIM, tq), F32), pltpu.VMEM((hps, tk, tq), F32)],
        compiler_params=pltpu.CompilerParams(
            dimension_semantics=("parallel", "parallel", "arbitrary"),
            vmem_limit_bytes=VMEM_LIMIT_BYTES),
        name="mla_attention_heads",
    )(q, k, vt, sgb, diag_bias)


def _attn_cached_kernel(q_ref, ckv_ref, kpe_ref, knew_ref, sgb_ref, w_uv_ref, yb_ref,
                        m_scr, l_scr, acc_scr, *, past_len, sq, tk):
    m = N_HEADS * sq
    q = q_ref[0].reshape(m, K_CAT)
    q_lat = q[:, 0:KV_RANK]
    q_rope = q[:, KV_RANK:K_CAT].astype(F32)
    q_rope = (q_rope[:, 0:QK_ROPE] + q_rope[:, QK_ROPE:LANES]).astype(BF16)
    q_pos = past_len + (lax.broadcasted_iota(jnp.int32, (m, 1), 0) & (sq - 1))
    k_lim = (q_pos // CHUNK + 1) * CHUNK

    m_scr[...] = jnp.full(m_scr.shape, NEG_INF, F32)
    l_scr[...] = jnp.zeros(l_scr.shape, F32)
    acc_scr[...] = jnp.zeros(acc_scr.shape, F32)
    nt = (((1,), (1,)), ((), ()))

    def update(s, visible, v):
        s = jnp.where(visible, s, NEG_INF)
        m_prev = m_scr[...]
        m_next = jnp.maximum(m_prev, jnp.max(s, axis=-1, keepdims=True))
        p = jnp.exp2(s - jnp.concatenate([m_next] * (s.shape[1] // LANES), axis=1))
        alpha = jnp.exp2(m_prev - m_next)
        l_scr[...] = alpha * l_scr[...] + jnp.sum(p, axis=-1, keepdims=True)
        m_scr[...] = m_next
        pv = jnp.dot(p.astype(BF16), v, preferred_element_type=F32)
        acc_scr[...] = acc_scr[...] * jnp.concatenate([alpha] * (KV_RANK // LANES), axis=1) + pv

    def past_body(j, c):
        off = pl.multiple_of(j * tk, tk)
        lat = ckv_ref[0, 0, pl.ds(off, tk), :].astype(BF16)
        pe = kpe_ref[0, 0, pl.ds(off, tk), :].astype(BF16)
        s = (lax.dot_general(q_lat, lat, nt, preferred_element_type=F32)
             + lax.dot_general(q_rope, pe, nt, preferred_element_type=F32))
        k_pos = off + lax.broadcasted_iota(jnp.int32, (m, tk), 1)
        update(s, k_pos < k_lim, lat)
        return c

    lax.fori_loop(0, past_len // tk, past_body, 0)

    kn = knew_ref[0]
    s = lax.dot_general(q, kn, nt, preferred_element_type=F32)
    idx = lax.broadcasted_iota(jnp.int32, (m, kn.shape[0]), 1)
    update(s, jnp.where(idx < sq, past_len + idx, k_lim) < k_lim, kn[:, 0:KV_RANK])

    inv_l = 1.0 / l_scr[...]
    o = (acc_scr[...] * jnp.concatenate([inv_l] * (KV_RANK // LANES), axis=1)).astype(BF16)
    sgb = sgb_ref[0]
    for hd in range(N_HEADS):
        att = jnp.dot(o[hd * sq:(hd + 1) * sq], w_uv_ref[hd], preferred_element_type=F32)
        sl = slice(hd * V_DIM, (hd + 1) * V_DIM)
        yb_ref[0, :, sl] = (att * sgb[:, sl]).astype(BF16)


def _attention_cached(q, cache_ckv, cache_kpe, layer, knew, sgb, w_uv, *, tk):
    b, _, sq, _ = q.shape
    past_len = cache_ckv.shape[2]
    d = sgb.shape[-1]
    m = N_HEADS * sq
    assert sq & (sq - 1) == 0 and past_len % tk == 0 and tk % LANES == 0 and knew.shape[1] % LANES == 0
    kern = functools.partial(_attn_cached_kernel, past_len=past_len, sq=sq, tk=tk)
    return pl.pallas_call(
        kern,
        grid=(b,),
        in_specs=[pl.BlockSpec((1, N_HEADS, sq, K_CAT), lambda i: (i, 0, 0, 0)),
                  pl.BlockSpec((1, 1, past_len, KV_RANK), lambda i: (layer, i, 0, 0)),
                  pl.BlockSpec((1, 1, past_len, QK_ROPE), lambda i: (layer, i, 0, 0)),
                  pl.BlockSpec((1,) + knew.shape[1:], lambda i: (i, 0, 0)),
                  pl.BlockSpec((1, sq, d), lambda i: (i, 0, 0)),
                  _const_spec(w_uv.shape)],
        out_specs=pl.BlockSpec((1, sq, d), lambda i: (i, 0, 0)),
        out_shape=jax.ShapeDtypeStruct((b, sq, d), BF16),
        scratch_shapes=[pltpu.VMEM((m, LANES), F32), pltpu.VMEM((m, LANES), F32),
                        pltpu.VMEM((m, KV_RANK), F32)],
        compiler_params=pltpu.CompilerParams(
            dimension_semantics=("parallel",), vmem_limit_bytes=VMEM_LIMIT_BYTES),
        name="mla_attention_cached",
    )(q, cache_ckv, cache_kpe, knew, sgb, w_uv)


def _outproj_kernel(x_ref, mod_ref, ya_ref, yb_ref, sua_ref, sub_ref,
                    w_a_ref, w_b_ref, w_o_ref, post_norm_ref, y_ref):
    bb, t, d = x_ref.shape
    m = bb * t
    ya = ya_ref[...].reshape(m, d)
    yb = yb_ref[...].reshape(m, d)
    pa = jnp.dot(ya, w_a_ref[...], preferred_element_type=F32)
    pb = jnp.dot(yb, w_b_ref[...], preferred_element_type=F32)
    merged = sua_ref[...].reshape(m, d).astype(F32) * pa + sub_ref[...].reshape(m, d).astype(F32) * pb
    o = jnp.dot(merged.astype(BF16), w_o_ref[...], preferred_element_type=F32)
    gate = mod_ref[...][:, :, 2 * d:3 * d]
    y_ref[...] = x_ref[...] + gate * _rms(o, post_norm_ref[...]).reshape(bb, t, d)


def _outproj(x, mod, ya, yb, sua, sub, lw, *, bb, t):
    b, s, d = x.shape
    tok = pl.BlockSpec((bb, t, d), lambda i, j: (i, j, 0))
    weights = [lw["w_branch_a"], lw["w_branch_b"], lw["w_out"], lw["post_norm"]]
    return pl.pallas_call(
        _outproj_kernel,
        grid=(b // bb, s // t),
        in_specs=[tok, pl.BlockSpec((bb, 1, 3 * d), lambda i, j: (i, 0, 0)), tok, tok, tok, tok]
        + [_const_spec(w.shape) for w in weights],
        out_specs=tok,
        out_shape=jax.ShapeDtypeStruct((b, s, d), F32),
        compiler_params=pltpu.CompilerParams(
            dimension_semantics=("parallel", "parallel"), vmem_limit_bytes=VMEM_LIMIT_BYTES),
        name="outproj",
    )(x, mod, ya, yb, sua, sub, *weights)


def _rot_half(w):
    half = w.shape[-1] // 2
    return jnp.concatenate([-w[..., half:], w[..., :half]], axis=-1)


def _twice(w):
    return jnp.concatenate([w, w], axis=-1)


def _layer_weights(l, p):
    d = p["w_in"].shape[1]
    w_in = p["w_in"][l]
    n_a = 2 * d + Q_RANK + KV_RANK
    w_kpe = w_in[:, n_a:n_a + QK_ROPE]
    wq =p["w_q_up"][l].reshape(Q_RANK, N_HEADS, QK_NOPE + QK_ROPE)
    wq_pe = wq[:, :, QK_NOPE:]
    w_q = jnp.concatenate(
        [wq[:, :, :QK_NOPE].reshape(Q_RANK, -1), wq_pe.reshape(Q_RANK, -1),
         _rot_half(wq_pe).reshape(Q_RANK, -1)], axis=1)
    row = lambda v: v.reshape(1, -1)
    return {
        "w_in_a": w_in[:, :n_a].astype(BF16),
        "w_in_k": jnp.concatenate([_twice(w_kpe), _twice(_rot_half(w_kpe))], axis=1).astype(BF16),
        "w_in_g": w_in[:, n_a + QK_ROPE:].astype(BF16),
        "w_q": w_q.astype(BF16),
        "w_uk": jnp.transpose(p["w_uk"][l], (1, 2, 0)).astype(BF16),
        "w_ukv": jnp.concatenate([p["w_uk"][l].reshape(KV_RANK, -1), p["w_uv"][l].reshape(KV_RANK, -1)],
                                 axis=1).astype(BF16),
        "w_uv": jnp.transpose(p["w_uv"][l], (1, 0, 2)).astype(BF16),
        "w_gate": jnp.concatenate([p["lru_wa"][l], p["lru_wx"][l]], axis=-1).astype(BF16),
        "pre_norm": row(p["pre_norm"][l]), "post_norm": row(p["post_norm"][l]),
        "conv_w": p["conv_w"][l], "conv_b": row(p["conv_b"][l]),
        "lru_ba": row(p["lru_ba"][l]), "lru_bx": row(p["lru_bx"][l]),
        "lru_lambda": row(p["lru_lambda"][l]),
        "q_norm": row(p["q_norm"][l]) * Q_SCALE, "kv_norm": row(p["kv_norm"][l]),
        "w_branch_a": p["w_branch_a"][l].astype(BF16), "w_branch_b": p["w_branch_b"][l].astype(BF16),
        "w_out": p["w_out"][l].astype(BF16),
    }


def _rope_tables(pos):
    half = QK_ROPE // 2
    inv = ROPE_THETA ** (-jnp.arange(half, dtype=F32) / half)
    ang = pos.astype(F32)[:, None] * inv[None, :]
    cos, sin = jnp.cos(ang), jnp.sin(ang)
    return jnp.concatenate([cos] * (LANES // half), axis=1), jnp.concatenate([sin] * (LANES // half), axis=1)


def _pick_tile(n, target):
    t = min(n, target)
    while n % t:
        t //= 2
    return t


def kernel(x_prompt, x_sample, c_prompt, c_sample, cache_ckv, cache_kpe, state_conv, state_lru, ada_w, ada_b, pre_norm, post_norm, w_in, conv_w, conv_b, lru_wa, lru_ba, lru_wx, lru_bx, lru_lambda, q_norm, w_q_up, kv_norm, w_uk, w_uv, w_branch_a, w_branch_b, w_out):
    p = dict(ada_w=ada_w, ada_b=ada_b, pre_norm=pre_norm, post_norm=post_norm, w_in=w_in,
             conv_w=conv_w, conv_b=conv_b, lru_wa=lru_wa, lru_ba=lru_ba, lru_wx=lru_wx, lru_bx=lru_bx,
             lru_lambda=lru_lambda, q_norm=q_norm, w_q_up=w_q_up, kv_norm=kv_norm, w_uk=w_uk,
             w_uv=w_uv, w_branch_a=w_branch_a, w_branch_b=w_branch_b, w_out=w_out)
    depth = w_in.shape[0]
    b_p, s_p, d = x_prompt.shape
    b_s, s_s, _ = x_sample.shape
    past_len = cache_ckv.shape[2]
    assert s_p % SUBLANES == 0 and s_s % SUBLANES == 0 and s_p >= SUBLANES and s_s >= SUBLANES

    mods = _modulation(jnp.concatenate([c_prompt, c_sample], axis=0), ada_w, ada_b)
    cos_p, sin_p = _rope_tables(jnp.arange(s_p, dtype=jnp.int32))
    cos_s, sin_s = _rope_tables(past_len + jnp.arange(s_s, dtype=jnp.int32))
    conv0_p = jnp.zeros((b_p, SUBLANES, d), F32)
    h0_p = jnp.zeros((b_p, 1, d), F32)

    t_in_p = _pick_tile(s_p, 256)
    t_out_p = _pick_tile(s_p, 512)
    tq_p = _pick_tile(s_p, 512)
    tk_p = _pick_tile(s_p, 512)
    tk_s = _pick_tile(past_len, 256)

    yp, ys = x_prompt, x_sample
    outs_p, outs_s = [], []
    for l in range(depth):
        lw = _layer_weights(l, p)

        mod_p = mods[l, :b_p, None, :]
        ya, sgb, sua, sub, q, ckv, kpe, k_heads, convo, hlast, vt = _inproj(
            yp, mod_p, conv0_p, h0_p, cos_p, sin_p, lw, bb=1, t=t_in_p, per_head=True)
        yb = _attention_heads(q, k_heads, vt, sgb, tq=tq_p, tk=tk_p)
        yp = _outproj(yp, mod_p, ya, yb, sua, sub, lw, bb=1, t=t_out_p)
        outs_p.append((ckv, kpe, convo[:, SUBLANES - (CONV_W - 1):, :], hlast[:, 0, :]))

        mod_s = mods[l, b_p:, None, :]
        conv0_s = jnp.pad(state_conv[l], ((0, 0), (SUBLANES - (CONV_W - 1), 0), (0, 0)))
        ya, sgb, sua, sub, q, ckv, kpe, kcat, convo, hlast = _inproj(
            ys, mod_s, conv0_s, state_lru[l][:, None, :], cos_s, sin_s, lw, bb=b_s, t=s_s,
            per_head=False)
        knew = jnp.pad(kcat, ((0, 0), (0, -s_s % LANES), (0, 0)))
        yb = _attention_cached(q, cache_ckv, cache_kpe, l, knew, sgb, lw["w_uv"], tk=tk_s)
        ys = _outproj(ys, mod_s, ya, yb, sua, sub, lw, bb=b_s, t=s_s)
        outs_s.append((ckv, kpe, convo[:, SUBLANES - (CONV_W - 1):, :], hlast[:, 0, :]))

    stack = lambda outs, k: jnp.stack([o[k] for o in outs])
    return (yp, ys,
            stack(outs_p, 0), stack(outs_p, 1), stack(outs_p, 2), stack(outs_p, 3),
            stack(outs_s, 0), stack(outs_s, 1), stack(outs_s, 2), stack(outs_s, 3))
```

```python
import functools
import math

import jax
import jax.numpy as jnp
from jax import lax
from jax.experimental import pallas as pl
from jax.experimental.pallas import tpu as pltpu

F32 = jnp.float32
BF16 = jnp.bfloat16

EPS = 1e-6
NEG_INF = -1e30
CHUNK = 64
LRU_C = 8.0
ROPE_THETA = 10000.0

LANES = 128
SUBLANES = 8
MXU_DIM = 256
VMEM_LIMIT_BYTES = 56 * 1024 * 1024

CONV_W = 4
N_HEADS = 8
QK_NOPE = 128
QK_ROPE = 64
V_DIM = 128
KV_RANK = 256
Q_RANK = 768
LRU_BLOCKS = 8
K_CAT = KV_RANK + LANES
Q_SCALE = (QK_NOPE + QK_ROPE) ** -0.5 * math.log2(math.e)


def _sigmoid(x):
    return jax.nn.sigmoid(x)


def _silu(x):
    return x * _sigmoid(x)


def _sqrt_nonneg(x):
    return jnp.where(x > 0.0, x * lax.rsqrt(x), 0.0)


def _rms(x, g):
    return x * lax.rsqrt(jnp.mean(x * x, axis=-1, keepdims=True) + EPS) * g


def _const_spec(shape):
    nd = len(shape)
    return pl.BlockSpec(shape, lambda *_: (0,) * nd, pipeline_mode=pl.Buffered(1))


def _mod_kernel(c_ref, w_ref, b_ref, o_ref):
    c = c_ref[...]
    o_ref[0] = jnp.dot(_silu(c), w_ref[0], preferred_element_type=F32,
                       precision=lax.Precision.HIGHEST) + b_ref[0]


def _modulation(c_all, ada_w, ada_b):
    depth, d, d3 = ada_w.shape
    n = c_all.shape[0]
    nblk = d3 // d
    return pl.pallas_call(
        _mod_kernel,
        grid=(depth, nblk),
        in_specs=[pl.BlockSpec((n, d), lambda l, j: (0, 0)),
                  pl.BlockSpec((1, d, d), lambda l, j: (l, 0, j)),
                  pl.BlockSpec((1, 1, d), lambda l, j: (l, 0, j))],
        out_specs=pl.BlockSpec((1, n, d), lambda l, j: (l, 0, j)),
        out_shape=jax.ShapeDtypeStruct((depth, n, d3), F32),
        name="adaln_mod",
    )(c_all, ada_w, ada_b.reshape(depth, 1, d3))


def _inproj_kernel(x_ref, mod_ref, conv0_ref, h0_ref, cos_ref, sin_ref,
                   w_in_ref, w_q_ref, w_uk_ref, w_gate_ref,
                   pre_norm_ref, conv_w_ref, conv_b_ref, ba_ref, bx_ref, lam_ref, qn_ref, kvn_ref,
                   *rest, per_head, n_alias):
    (ya_ref, sgb_ref, sua_ref, sub_ref, q_ref, ckv_ref, kpe_ref, kcat_ref,
     convo_ref, hlast_ref, *rest) = rest[n_alias:]
    if per_head:
        vt_ref, buf_ref, hcar_ref, hb_ref, cqn_ref, v_scr = rest
    else:
        buf_ref, hcar_ref, hb_ref, cqn_ref = rest
    bb, t, d = x_ref.shape
    q_main = QK_NOPE if per_head else KV_RANK
    m = bb * t
    groups = t // SUBLANES
    gw = d // LRU_BLOCKS
    cw_ = MXU_DIM
    n_chunks = d // cw_
    blocks_per_chunk = cw_ // gw
    nq = N_HEADS * QK_NOPE
    npe = N_HEADS * QK_ROPE
    o_xa, o_ga, o_cq = 0, d, 2 * d
    o_ckv = o_cq + Q_RANK
    o_kpe = o_ckv + KV_RANK
    o_krot = o_kpe + LANES
    o_gb = o_krot + LANES
    o_ua = o_gb + d
    o_ub = o_ua + d

    @pl.when(pl.program_id(1) == 0)
    def _():
        buf_ref[:, 0:SUBLANES, :] = conv0_ref[...]
        hcar_ref[...] = h0_ref[...]

    x = x_ref[...]
    mod = mod_ref[...]
    shift = mod[:, :, 0:d]
    scale = mod[:, :, d:2 * d]
    h = _rms(x, pre_norm_ref[...]) * (1.0 + scale) + shift
    hb_ref[...] = h.reshape(m, d).astype(BF16)

    def proj(lo, width):
        return jnp.dot(hb_ref[...], w_in_ref[:, lo:lo + width], preferred_element_type=F32)

    cos = cos_ref[...][None]
    sin = sin_ref[...][None]

    def conv_phase(c):
        cs = slice(c * cw_, (c + 1) * cw_)
        xa = proj(o_xa + c * cw_, cw_)
        buf_ref[:, SUBLANES:SUBLANES + t, cs] = xa.reshape(bb, t, cw_)
        cw = conv_w_ref[:, cs]
        xc = conv_b_ref[:, cs] + buf_ref[:, SUBLANES - 3:SUBLANES - 3 + t, cs] * cw[0:1]
        xc = xc + buf_ref[:, SUBLANES - 2:SUBLANES - 2 + t, cs] * cw[1:2]
        xc = xc + buf_ref[:, SUBLANES - 1:SUBLANES - 1 + t, cs] * cw[2:3]
        xc = xc + buf_ref[:, SUBLANES:SUBLANES + t, cs] * cw[3:4]
        tail = buf_ref[:, t:t + SUBLANES, cs]
        convo_ref[:, :, cs] = tail
        buf_ref[:, 0:SUBLANES, cs] = tail
        return xc.reshape(m, cw_)

    def gate_phase(c, xc2):
        cs = slice(c * cw_, (c + 1) * cw_)
        xcb = xc2.astype(BF16)
        r_parts, i_parts = [], []
        for n in range(blocks_per_chunk):
            g = jnp.dot(xcb[:, n * gw:(n + 1) * gw], w_gate_ref[c * blocks_per_chunk + n],
                        preferred_element_type=F32)
            r_parts.append(g[:, 0:gw])
            i_parts.append(g[:, gw:2 * gw])
        r = _sigmoid(jnp.concatenate(r_parts, axis=1) + ba_ref[:, cs])
        ig = _sigmoid(jnp.concatenate(i_parts, axis=1) + bx_ref[:, cs])
        nl = -lam_ref[:, cs]
        softplus = jnp.maximum(nl, 0.0) + jnp.log1p(jnp.exp(-jnp.abs(nl)))
        th = jnp.tanh((-0.5 * LRU_C * r) * softplus)
        inv = 1.0 / (1.0 - th)
        a = (1.0 + th) * inv
        mult = (2.0 * _sqrt_nonneg(-th)) * inv
        return a, mult * (ig * xc2)

    def scan_phase(c, a, b):
        cs = slice(c * cw_, (c + 1) * cw_)
        a4 = a.reshape(bb * groups, SUBLANES, cw_)
        b4 = b.reshape(bb * groups, SUBLANES, cw_)
        row = lax.broadcasted_iota(jnp.int32, a4.shape, 1)
        for sft in (1, 2, 4):
            keep = row >= sft
            a_sh = jnp.where(keep, pltpu.roll(a4, sft, axis=1), 1.0)
            b_sh = jnp.where(keep, pltpu.roll(b4, sft, axis=1), 0.0)
            b4 = b4 + a4 * b_sh
            a4 = a4 * a_sh
        a5 = a4.reshape(bb, groups, SUBLANES, cw_)
        b5 = b4.reshape(bb, groups, SUBLANES, cw_)
        hc = hcar_ref[:, :, cs]
        outs = []
        for g in range(groups):
            hs = a5[:, g] * hc + b5[:, g]
            outs.append(hs)
            hc = hs[:, SUBLANES - 1:SUBLANES, :]
        hcar_ref[:, :, cs] = hc
        hlast_ref[:, :, cs] = hc
        return jnp.concatenate(outs, axis=1).reshape(m, cw_)

    def out_phase(c, y_lru):
        cs = slice(c * cw_, (c + 1) * cw_)
        ga = proj(o_ga + c * cw_, cw_)
        ya_ref[:, :, cs] = (y_lru * _silu(ga)).astype(BF16).reshape(bb, t, cw_)

    def gate_out(ref, fn, off, c):
        cs = slice(c * cw_, (c + 1) * cw_)
        ref[:, :, cs] = fn(proj(off + c * cw_, cw_)).astype(BF16).reshape(bb, t, cw_)

    def q_latent(heads):
        q_nope = jnp.dot(cqn_ref[...], w_q_ref[:, heads[0] * QK_NOPE:(heads[-1] + 1) * QK_NOPE],
                         preferred_element_type=F32).astype(BF16)
        for k, hd in enumerate(heads):
            q_h = q_nope[:, k * QK_NOPE:(k + 1) * QK_NOPE]
            if not per_head:
                q_h = jnp.dot(q_h, w_uk_ref[hd], preferred_element_type=F32).astype(BF16)
            q_ref[:, hd, :, 0:q_main] = q_h.reshape(bb, t, q_main)

    def q_rotary():
        q_pe = jnp.dot(cqn_ref[...], w_q_ref[:, nq:nq + npe], preferred_element_type=F32)
        q_rot = jnp.dot(cqn_ref[...], w_q_ref[:, nq + npe:nq + 2 * npe], preferred_element_type=F32)
        lane = lax.broadcasted_iota(jnp.int32, (1, 1, LANES), 2)
        for pair in range(N_HEADS // 2):
            sl = slice(pair * LANES, (pair + 1) * LANES)
            both = q_pe[:, sl].reshape(bb, t, LANES) * cos + q_rot[:, sl].reshape(bb, t, LANES) * sin
            rs = slice(q_main, q_main + LANES)
            q_ref[:, 2 * pair, :, rs] = jnp.where(lane < QK_ROPE, both, 0.0).astype(BF16)
            q_ref[:, 2 * pair + 1, :, rs] = jnp.where(lane < QK_ROPE, 0.0, both).astype(BF16)

    def key_latents():
        ckv = _rms(proj(o_ckv, KV_RANK), kvn_ref[...])
        ckv_ref[...] = ckv.reshape(bb, t, KV_RANK)
        ckvb = ckv.astype(BF16)
        kp = proj(o_kpe, LANES).reshape(bb, t, LANES)
        kr = proj(o_krot, LANES).reshape(bb, t, LANES)
        kroped = kp * cos + kr * sin
        kpe_ref[...] = kroped[:, :, 0:QK_ROPE]
        if per_head:
            nk = N_HEADS * QK_NOPE
            k_nope = jnp.dot(ckvb, w_uk_ref[:, 0:nk], preferred_element_type=F32).astype(BF16)
            v_scr[...] = jnp.dot(ckvb, w_uk_ref[:, nk:nk + N_HEADS * V_DIM], preferred_element_type=F32)
            v_t = v_scr[...].T
            for hd in range(N_HEADS):
                kcat_ref[:, hd, :, 0:QK_NOPE] = k_nope[:, hd * QK_NOPE:(hd + 1) * QK_NOPE].reshape(
                    bb, t, QK_NOPE)
                kcat_ref[:, hd, :, QK_NOPE:QK_NOPE + LANES] = kroped.astype(BF16)
                vt_ref[0, hd, 0] = v_t[hd * V_DIM:(hd + 1) * V_DIM].astype(BF16)
        else:
            kcat_ref[:, :, 0:KV_RANK] = ckvb.reshape(bb, t, KV_RANK)
            kcat_ref[:, :, KV_RANK:K_CAT] = kroped.astype(BF16)

    cqn_ref[...] = _rms(proj(o_cq, Q_RANK), qn_ref[...]).astype(BF16)
    other = [lambda: q_latent((0, 1, 2, 3)), lambda: q_latent((4, 5, 6, 7)), q_rotary, key_latents]
    for c in range(n_chunks):
        xc2 = conv_phase(c)
        gate_out(sgb_ref, _silu, o_gb, c)
        a, b = gate_phase(c, xc2)
        gate_out(sua_ref, _sigmoid, o_ua, c)
        y_lru = scan_phase(c, a, b)
        gate_out(sub_ref, _sigmoid, o_ub, c)
        out_phase(c, y_lru)
        if c < len(other):
            other[c]()
    for f in other[n_chunks:]:
        f()


def _inproj(x, mod, conv0, h0, cos, sin, lw, *, bb, t, per_head, layer, depth, stacked):
    b, s, d = x.shape
    grid = (b // bb, s // t)
    tok = lambda w: pl.BlockSpec((bb, t, w), lambda i, j: (i, j, 0))
    tok_l = lambda w: pl.BlockSpec((None, bb, t, w), lambda i, j: (layer, i, j, 0))
    per_b = lambda r, w: pl.BlockSpec((bb, r, w), lambda i, j: (i, 0, 0))
    heads = lambda w: pl.BlockSpec((bb, N_HEADS, t, w), lambda i, j: (i, 0, j, 0))
    q_w = (QK_NOPE if per_head else KV_RANK) + LANES
    weights = [lw["w_in"], lw["w_q"], lw["w_ukv"] if per_head else lw["w_uk"], lw["w_gate"],
               lw["pre_norm"], lw["conv_w"],
               lw["conv_b"], lw["lru_ba"], lw["lru_bx"], lw["lru_lambda"], lw["q_norm"], lw["kv_norm"]]
    in_specs = ([tok(d), per_b(1, 3 * d), per_b(SUBLANES, d), per_b(1, d),
                 pl.BlockSpec((t, LANES), lambda i, j: (j, 0)),
                 pl.BlockSpec((t, LANES), lambda i, j: (j, 0))]
                + [_const_spec(w.shape) for w in weights])
    out_shape = [jax.ShapeDtypeStruct((b, s, d), BF16)] * 4 + [
        jax.ShapeDtypeStruct((b, N_HEADS, s, q_w), BF16),
        jax.ShapeDtypeStruct((depth, b, s, KV_RANK), F32),
        jax.ShapeDtypeStruct((depth, b, s, QK_ROPE), F32),
        jax.ShapeDtypeStruct((b, N_HEADS, s, q_w) if per_head else (b, s, K_CAT), BF16),
        jax.ShapeDtypeStruct((b, SUBLANES, d), F32),
        jax.ShapeDtypeStruct((b, 1, d), F32),
    ]
    out_specs = [tok(d)] * 4 + [
        heads(q_w), tok_l(KV_RANK), tok_l(QK_ROPE), heads(q_w) if per_head else tok(K_CAT),
        per_b(SUBLANES, d), per_b(1, d),
    ]
    aliased = [] if stacked is None else list(stacked)
    n_in = len(in_specs)
    in_specs = in_specs + [pl.BlockSpec(memory_space=pl.ANY)] * len(aliased)
    aliases = {n_in + k: 5 + k for k in range(len(aliased))}
    if per_head:
        assert bb == 1
        out_shape.append(jax.ShapeDtypeStruct((b, N_HEADS, s // t, V_DIM, t), BF16))
        out_specs.append(pl.BlockSpec((1, N_HEADS, 1, V_DIM, t), lambda i, j: (i, 0, j, 0, 0)))
    return pl.pallas_call(
        functools.partial(_inproj_kernel, per_head=per_head, n_alias=len(aliased)),
        grid=grid,
        in_specs=in_specs,
        out_specs=out_specs,
        out_shape=out_shape,
        input_output_aliases=aliases,
        scratch_shapes=[pltpu.VMEM((bb, t + SUBLANES, d), F32), pltpu.VMEM((bb, 1, d), F32),
                        pltpu.VMEM((bb * t, d), BF16), pltpu.VMEM((bb * t, Q_RANK), BF16)]
        + ([pltpu.VMEM((bb * t, N_HEADS * V_DIM), F32)] if per_head else []),
        compiler_params=pltpu.CompilerParams(
            dimension_semantics=("parallel", "arbitrary"), vmem_limit_bytes=VMEM_LIMIT_BYTES),
        name="inproj_lru",
    )(x, mod, conv0, h0, cos, sin, *weights, *aliased)


def _attn_heads_kernel(q_ref, k_ref, vt_ref, sgb_ref, bias_ref, yb_ref, m_scr, l_scr, acc_scr, s_scr,
                       *, tk):
    n_groups = q_ref.shape[1]
    gran = vt_ref.shape[-1]
    n_tiles = pl.program_id(2) + 1

    m_scr[...] = jnp.full(m_scr.shape, NEG_INF, F32)
    l_scr[...] = jnp.zeros(l_scr.shape, F32)
    acc_scr[...] = jnp.zeros(acc_scr.shape, F32)

    def qk(j, g):
        k = k_ref[0, g, pl.ds(pl.multiple_of(j * tk, tk), tk), :]
        return lax.dot_general(k, q_ref[0, g], (((1,), (1,)), ((), ())),
                               preferred_element_type=F32)

    s_scr[0] = qk(0, 0)

    def step(j, masked, last_tile):
        for g in range(n_groups):
            if g + 1 < n_groups:
                s_scr[g + 1] = qk(j, g + 1)
            elif not last_tile:
                s_scr[0] = qk(j + 1, 0)
            s = s_scr[g]
            if masked:
                s = s + bias_ref[...]
            m_prev = m_scr[g]
            m_next = jnp.maximum(m_prev, jnp.max(s, axis=0, keepdims=True))
            p = jnp.exp2(s - m_next)
            alpha = jnp.exp2(m_prev - m_next)
            l_scr[g] = alpha * l_scr[g] + jnp.sum(p, axis=0, keepdims=True)
            m_scr[g] = m_next
            vt = jnp.concatenate([vt_ref[0, g, j * (tk // gran) + i] for i in range(tk // gran)], axis=1)
            pv = jnp.dot(vt, p.astype(BF16), preferred_element_type=F32)
            acc_scr[g] = acc_scr[g] * alpha + pv

    last = n_tiles - 1

    def body(j, c):
        step(j, False, False)
        return c

    lax.fori_loop(0, last, body, 0)
    step(last, True, True)

    sgb = sgb_ref[0]
    for g in range(n_groups):
        o = (acc_scr[g] * (1.0 / l_scr[g])).T
        sl = slice(g * V_DIM, (g + 1) * V_DIM)
        yb_ref[0, :, sl] = (o * sgb[:, sl]).astype(BF16)


def _attention_heads(q, k, vt, sgb, *, tq, tk, heads_per_step=4):
    b, n_heads, sq, qw = q.shape
    sk = k.shape[2]
    d = sgb.shape[-1]
    gran = vt.shape[-1]
    hps = heads_per_step
    assert n_heads % hps == 0 and sq % tq == 0 and sk % tk == 0 and tk % gran == 0
    assert vt.shape[2] * gran == sk and (hps * V_DIM) % LANES == 0
    assert tk == tq and tq % CHUNK == 0 and sq == sk
    k_idx = lax.broadcasted_iota(jnp.int32, (tk, tq), 0)
    q_idx = lax.broadcasted_iota(jnp.int32, (tk, tq), 1)
    diag_bias = jnp.where(k_idx // CHUNK <= q_idx // CHUNK, 0.0, NEG_INF).astype(F32)
    kern = functools.partial(_attn_heads_kernel, tk=tk)
    return pl.pallas_call(
        kern,
        grid=(b, n_heads // hps, sq // tq),
        in_specs=[pl.BlockSpec((1, hps, tq, qw), lambda i, h, j: (i, h, j, 0)),
                  pl.BlockSpec((1, hps, sk, qw), lambda i, h, j: (i, h, 0, 0)),
                  pl.BlockSpec((1, hps) + vt.shape[2:], lambda i, h, j: (i, h, 0, 0, 0)),
                  pl.BlockSpec((1, tq, hps * V_DIM), lambda i, h, j: (i, j, h)),
                  _const_spec((tk, tq))],
        out_specs=pl.BlockSpec((1, tq, hps * V_DIM), lambda i, h, j: (i, j, h)),
        out_shape=jax.ShapeDtypeStruct((b, sq, d), BF16),
        scratch_shapes=[pltpu.VMEM((hps, 1, tq), F32), pltpu.VMEM((hps, 1, tq), F32),
                        pltpu.VMEM((hps, V_DIM, tq), F32), pltpu.VMEM((hps, tk, tq), F32)],
        compiler_params=pltpu.CompilerParams(
            dimension_semantics=("parallel", "parallel", "arbitrary"),
            vmem_limit_bytes=VMEM_LIMIT_BYTES),
        name="mla_attention_heads",
    )(q, k, vt, sgb, diag_bias)


def _attn_cached_kernel(q_ref, ckv_ref, kpe_ref, knew_ref, sgb_ref, w_uv_ref, yb_ref,
                        m_scr, l_scr, acc_scr, *, past_len, sq, tk):
    m = N_HEADS * sq
    q = q_ref[0].reshape(m, K_CAT)
    q_lat = q[:, 0:KV_RANK]
    q_rope = q[:, KV_RANK:K_CAT].astype(F32)
    q_rope = (q_rope[:, 0:QK_ROPE] + q_rope[:, QK_ROPE:LANES]).astype(BF16)
    q_pos = past_len + (lax.broadcasted_iota(jnp.int32, (m, 1), 0) & (sq - 1))
    k_lim = (q_pos // CHUNK + 1) * CHUNK

    m_scr[...] = jnp.full(m_scr.shape, NEG_INF, F32)
    l_scr[...] = jnp.zeros(l_scr.shape, F32)
    acc_scr[...] = jnp.zeros(acc_scr.shape, F32)
    nt = (((1,), (1,)), ((), ()))

    def update(s, visible, v):
        s = jnp.where(visible, s, NEG_INF)
        m_prev = m_scr[...]
        m_next = jnp.maximum(m_prev, jnp.max(s, axis=-1, keepdims=True))
        p = jnp.exp2(s - jnp.concatenate([m_next] * (s.shape[1] // LANES), axis=1))
        alpha = jnp.exp2(m_prev - m_next)
        l_scr[...] = alpha * l_scr[...] + jnp.sum(p, axis=-1, keepdims=True)
        m_scr[...] = m_next
        pv = jnp.dot(p.astype(BF16), v, preferred_element_type=F32)
        acc_scr[...] = acc_scr[...] * jnp.concatenate([alpha] * (KV_RANK // LANES), axis=1) + pv

    def past_body(j, c):
        off = pl.multiple_of(j * tk, tk)
        lat = ckv_ref[0, 0, pl.ds(off, tk), :].astype(BF16)
        pe = kpe_ref[0, 0, pl.ds(off, tk), :].astype(BF16)
        s = (lax.dot_general(q_lat, lat, nt, preferred_element_type=F32)
             + lax.dot_general(q_rope, pe, nt, preferred_element_type=F32))
        k_pos = off + lax.broadcasted_iota(jnp.int32, (m, tk), 1)
        update(s, k_pos < k_lim, lat)
        return c

    lax.fori_loop(0, past_len // tk, past_body, 0)

    kn = knew_ref[0]
    s = lax.dot_general(q, kn, nt, preferred_element_type=F32)
    idx = lax.broadcasted_iota(jnp.int32, (m, kn.shape[0]), 1)
    update(s, jnp.where(idx < sq, past_len + idx, k_lim) < k_lim, kn[:, 0:KV_RANK])

    inv_l = 1.0 / l_scr[...]
    o = (acc_scr[...] * jnp.concatenate([inv_l] * (KV_RANK // LANES), axis=1)).astype(BF16)
    sgb = sgb_ref[0]
    for hd in range(N_HEADS):
        att = jnp.dot(o[hd * sq:(hd + 1) * sq], w_uv_ref[hd], preferred_element_type=F32)
        sl = slice(hd * V_DIM, (hd + 1) * V_DIM)
        yb_ref[0, :, sl] = (att * sgb[:, sl]).astype(BF16)


def _attention_cached(q, cache_ckv, cache_kpe, layer, knew, sgb, w_uv, *, tk):
    b, _, sq, _ = q.shape
    past_len = cache_ckv.shape[2]
    d = sgb.shape[-1]
    m = N_HEADS * sq
    assert sq & (sq - 1) == 0 and past_len % tk == 0 and tk % LANES == 0 and knew.shape[1] % LANES == 0
    kern = functools.partial(_attn_cached_kernel, past_len=past_len, sq=sq, tk=tk)
    return pl.pallas_call(
        kern,
        grid=(b,),
        in_specs=[pl.BlockSpec((1, N_HEADS, sq, K_CAT), lambda i: (i, 0, 0, 0)),
                  pl.BlockSpec((1, 1, past_len, KV_RANK), lambda i: (layer, i, 0, 0)),
                  pl.BlockSpec((1, 1, past_len, QK_ROPE), lambda i: (layer, i, 0, 0)),
                  pl.BlockSpec((1,) + knew.shape[1:], lambda i: (i, 0, 0)),
                  pl.BlockSpec((1, sq, d), lambda i: (i, 0, 0)),
                  _const_spec(w_uv.shape)],
        out_specs=pl.BlockSpec((1, sq, d), lambda i: (i, 0, 0)),
        out_shape=jax.ShapeDtypeStruct((b, sq, d), BF16),
        scratch_shapes=[pltpu.VMEM((m, LANES), F32), pltpu.VMEM((m, LANES), F32),
                        pltpu.VMEM((m, KV_RANK), F32)],
        compiler_params=pltpu.CompilerParams(
            dimension_semantics=("parallel",), vmem_limit_bytes=VMEM_LIMIT_BYTES),
        name="mla_attention_cached",
    )(q, cache_ckv, cache_kpe, knew, sgb, w_uv)


def _outproj_kernel(x_ref, mod_ref, ya_ref, yb_ref, sua_ref, sub_ref,
                    w_a_ref, w_b_ref, w_o_ref, post_norm_ref, y_ref):
    bb, t, d = x_ref.shape
    m = bb * t
    ya = ya_ref[...].reshape(m, d)
    yb = yb_ref[...].reshape(m, d)
    pa = jnp.dot(ya, w_a_ref[...], preferred_element_type=F32)
    pb = jnp.dot(yb, w_b_ref[...], preferred_element_type=F32)
    merged = sua_ref[...].reshape(m, d).astype(F32) * pa + sub_ref[...].reshape(m, d).astype(F32) * pb
    o = jnp.dot(merged.astype(BF16), w_o_ref[...], preferred_element_type=F32)
    gate = mod_ref[...][:, :, 2 * d:3 * d]
    y_ref[...] = x_ref[...] + gate * _rms(o, post_norm_ref[...]).reshape(bb, t, d)


def _outproj(x, mod, ya, yb, sua, sub, lw, *, bb, t):
    b, s, d = x.shape
    tok = pl.BlockSpec((bb, t, d), lambda i, j: (i, j, 0))
    weights = [lw["w_branch_a"], lw["w_branch_b"], lw["w_out"], lw["post_norm"]]
    return pl.pallas_call(
        _outproj_kernel,
        grid=(b // bb, s // t),
        in_specs=[tok, pl.BlockSpec((bb, 1, 3 * d), lambda i, j: (i, 0, 0)), tok, tok, tok, tok]
        + [_const_spec(w.shape) for w in weights],
        out_specs=tok,
        out_shape=jax.ShapeDtypeStruct((b, s, d), F32),
        compiler_params=pltpu.CompilerParams(
            dimension_semantics=("parallel", "parallel"), vmem_limit_bytes=VMEM_LIMIT_BYTES),
        name="outproj",
    )(x, mod, ya, yb, sua, sub, *weights)


def _rot_half(w):
    half = w.shape[-1] // 2
    return jnp.concatenate([-w[..., half:], w[..., :half]], axis=-1)


def _twice(w):
    return jnp.concatenate([w, w], axis=-1)


def _layer_weights(l, p):
    d = p["w_in"].shape[1]
    w_in = p["w_in"][l]
    o = 0
    seg = {}
    for name, width in (("xa", d), ("ga", d), ("cq", Q_RANK), ("ckv", KV_RANK), ("kpe", QK_ROPE),
                        ("gb", d), ("ua", d), ("ub", d)):
        seg[name] = w_in[:, o:o + width]
        o += width
    w_in_ext = jnp.concatenate(
        [seg["xa"], seg["ga"], seg["cq"], seg["ckv"], _twice(seg["kpe"]),
         _twice(_rot_half(seg["kpe"])), seg["gb"], seg["ua"], seg["ub"]], axis=1)
    wq = p["w_q_up"][l].reshape(Q_RANK, N_HEADS, QK_NOPE + QK_ROPE)
    wq_pe = wq[:, :, QK_NOPE:]
    w_q = jnp.concatenate(
        [wq[:, :, :QK_NOPE].reshape(Q_RANK, -1), wq_pe.reshape(Q_RANK, -1),
         _rot_half(wq_pe).reshape(Q_RANK, -1)], axis=1)
    row = lambda v: v.reshape(1, -1)
    return {
        "w_in": w_in_ext.astype(BF16),
        "w_q": w_q.astype(BF16),
        "w_uk": jnp.transpose(p["w_uk"][l], (1, 2, 0)).astype(BF16),
        "w_ukv": jnp.concatenate([p["w_uk"][l].reshape(KV_RANK, -1), p["w_uv"][l].reshape(KV_RANK, -1)],
                                 axis=1).astype(BF16),
        "w_uv": jnp.transpose(p["w_uv"][l], (1, 0, 2)).astype(BF16),
        "w_gate": jnp.concatenate([p["lru_wa"][l], p["lru_wx"][l]], axis=-1).astype(BF16),
        "pre_norm": row(p["pre_norm"][l]), "post_norm": row(p["post_norm"][l]),
        "conv_w": p["conv_w"][l], "conv_b": row(p["conv_b"][l]),
        "lru_ba": row(p["lru_ba"][l]), "lru_bx": row(p["lru_bx"][l]),
        "lru_lambda": row(p["lru_lambda"][l]),
        "q_norm": row(p["q_norm"][l]) * Q_SCALE, "kv_norm": row(p["kv_norm"][l]),
        "w_branch_a": p["w_branch_a"][l].astype(BF16), "w_branch_b": p["w_branch_b"][l].astype(BF16),
        "w_out": p["w_out"][l].astype(BF16),
    }


def _rope_tables(pos):
    half = QK_ROPE // 2
    inv = ROPE_THETA ** (-jnp.arange(half, dtype=F32) / half)
    ang = pos.astype(F32)[:, None] * inv[None, :]
    cos, sin = jnp.cos(ang), jnp.sin(ang)
    return jnp.concatenate([cos] * (LANES // half), axis=1), jnp.concatenate([sin] * (LANES // half), axis=1)


def _pick_tile(n, target):
    t = min(n, target)
    while n % t:
        t //= 2
    return t


def kernel(x_prompt, x_sample, c_prompt, c_sample, cache_ckv, cache_kpe, state_conv, state_lru, ada_w, ada_b, pre_norm, post_norm, w_in, conv_w, conv_b, lru_wa, lru_ba, lru_wx, lru_bx, lru_lambda, q_norm, w_q_up, kv_norm, w_uk, w_uv, w_branch_a, w_branch_b, w_out):
    p = dict(ada_w=ada_w, ada_b=ada_b, pre_norm=pre_norm, post_norm=post_norm, w_in=w_in,
             conv_w=conv_w, conv_b=conv_b, lru_wa=lru_wa, lru_ba=lru_ba, lru_wx=lru_wx, lru_bx=lru_bx,
             lru_lambda=lru_lambda, q_norm=q_norm, w_q_up=w_q_up, kv_norm=kv_norm, w_uk=w_uk,
             w_uv=w_uv, w_branch_a=w_branch_a, w_branch_b=w_branch_b, w_out=w_out)
    depth = w_in.shape[0]
    b_p, s_p, d = x_prompt.shape
    b_s, s_s, _ = x_sample.shape
    past_len = cache_ckv.shape[2]
    assert s_p % SUBLANES == 0 and s_s % SUBLANES == 0 and s_p >= SUBLANES and s_s >= SUBLANES

    mods = _modulation(jnp.concatenate([c_prompt, c_sample], axis=0), ada_w, ada_b)
    cos_p, sin_p = _rope_tables(jnp.arange(s_p, dtype=jnp.int32))
    cos_s, sin_s = _rope_tables(past_len + jnp.arange(s_s, dtype=jnp.int32))
    conv0_p = jnp.zeros((b_p, SUBLANES, d), F32)
    h0_p = jnp.zeros((b_p, 1, d), F32)

    t_in_p = _pick_tile(s_p, 256)
    t_out_p = _pick_tile(s_p, 512)
    tq_p = _pick_tile(s_p, 512)
    tk_p = _pick_tile(s_p, 512)
    tk_s = _pick_tile(past_len, 1024)

    yp, ys = x_prompt, x_sample
    outs_p, outs_s = [], []
    lat_p = lat_s = None
    for l in range(depth):
        lw = _layer_weights(l, p)

        mod_p = mods[l, :b_p, None, :]
        ya, sgb, sua, sub, q, ckv, kpe, k_heads, convo, hlast, vt = _inproj(
            yp, mod_p, conv0_p, h0_p, cos_p, sin_p, lw, bb=1, t=t_in_p, per_head=True,
            layer=l, depth=depth, stacked=lat_p)
        lat_p = (ckv, kpe)
        yb = _attention_heads(q, k_heads, vt, sgb, tq=tq_p, tk=tk_p)
        yp = _outproj(yp, mod_p, ya, yb, sua, sub, lw, bb=1, t=t_out_p)
        outs_p.append((convo[:, SUBLANES - (CONV_W - 1):, :], hlast[:, 0, :]))

        mod_s = mods[l, b_p:, None, :]
        conv0_s = jnp.pad(state_conv[l], ((0, 0), (SUBLANES - (CONV_W - 1), 0), (0, 0)))
        ya, sgb, sua, sub, q, ckv, kpe, kcat, convo, hlast = _inproj(
            ys, mod_s, conv0_s, state_lru[l][:, None, :], cos_s, sin_s, lw, bb=b_s, t=s_s,
            per_head=False, layer=l, depth=depth, stacked=lat_s)
        lat_s = (ckv, kpe)
        knew = jnp.pad(kcat, ((0, 0), (0, -s_s % LANES), (0, 0)))
        yb = _attention_cached(q, cache_ckv, cache_kpe, l, knew, sgb, lw["w_uv"], tk=tk_s)
        ys = _outproj(ys, mod_s, ya, yb, sua, sub, lw, bb=b_s, t=s_s)
        outs_s.append((convo[:, SUBLANES - (CONV_W - 1):, :], hlast[:, 0, :]))

    stack = lambda outs, k: jnp.stack([o[k] for o in outs])
    return (yp, ys,
            lat_p[0], lat_p[1], stack(outs_p, 0), stack(outs_p, 1),
            lat_s[0], lat_s[1], stack(outs_s, 0), stack(outs_s, 1))
```

```python
import functools
import math

import jax
import jax.numpy as jnp
from jax import lax
from jax.experimental import pallas as pl
from jax.experimental.pallas import tpu as pltpu

F32 = jnp.float32
BF16 = jnp.bfloat16

EPS = 1e-6
NEG_INF = -1e30
CHUNK = 64
LRU_C = 8.0
ROPE_THETA = 10000.0

LANES = 128
SUBLANES = 8
MXU_DIM = 256
VMEM_LIMIT_BYTES = 56 * 1024 * 1024

CONV_W = 4
N_HEADS = 8
QK_NOPE = 128
QK_ROPE = 64
V_DIM = 128
KV_RANK = 256
Q_RANK = 768
LRU_BLOCKS = 8
K_CAT = KV_RANK + LANES
Q_SCALE = (QK_NOPE + QK_ROPE) ** -0.5 * math.log2(math.e)


def _sigmoid(x):
    return jax.nn.sigmoid(x)


def _silu(x):
    return x * _sigmoid(x)


def _sqrt_nonneg(x):
    return jnp.where(x > 0.0, x * lax.rsqrt(x), 0.0)


def _rms(x, g):
    return x * lax.rsqrt(jnp.mean(x * x, axis=-1, keepdims=True) + EPS) * g


def _const_spec(shape):
    nd = len(shape)
    return pl.BlockSpec(shape, lambda *_: (0,) * nd, pipeline_mode=pl.Buffered(1))


def _mod_kernel(c_ref, w_ref, b_ref, o_ref):
    c = c_ref[...]
    o_ref[0] = jnp.dot(_silu(c), w_ref[0], preferred_element_type=F32,
                       precision=lax.Precision.HIGHEST) + b_ref[0]


def _modulation(c_all, ada_w, ada_b):
    depth, d, d3 = ada_w.shape
    n = c_all.shape[0]
    nblk = d3 // d
    return pl.pallas_call(
        _mod_kernel,
        grid=(depth, nblk),
        in_specs=[pl.BlockSpec((n, d), lambda l, j: (0, 0)),
                  pl.BlockSpec((1, d, d), lambda l, j: (l, 0, j)),
                  pl.BlockSpec((1, 1, d), lambda l, j: (l, 0, j))],
        out_specs=pl.BlockSpec((1, n, d), lambda l, j: (l, 0, j)),
        out_shape=jax.ShapeDtypeStruct((depth, n, d3), F32),
        name="adaln_mod",
    )(c_all, ada_w, ada_b.reshape(depth, 1, d3))


def _inproj_kernel(x_ref, mod_ref, conv0_ref, h0_ref, cos_ref, sin_ref,
                   w_in_ref, w_q_ref, w_uk_ref, w_gate_ref,
                   pre_norm_ref, conv_w_ref, conv_b_ref, ba_ref, bx_ref, lam_ref, qn_ref, kvn_ref,
                   *rest, per_head, n_alias):
    (ya_ref, sgb_ref, sua_ref, sub_ref, q_ref, ckv_ref, kpe_ref, kcat_ref,
     convo_ref, hlast_ref, *rest) = rest[n_alias:]
    if per_head:
        vt_ref, buf_ref, hcar_ref, hb_ref, cqn_ref, v_scr = rest
    else:
        buf_ref, hcar_ref, hb_ref, cqn_ref = rest
    bb, t, d = x_ref.shape
    q_main = QK_NOPE if per_head else KV_RANK
    m = bb * t
    groups = t // SUBLANES
    gw = d // LRU_BLOCKS
    cw_ = MXU_DIM
    n_chunks = d // cw_
    blocks_per_chunk = cw_ // gw
    nq = N_HEADS * QK_NOPE
    npe = N_HEADS * QK_ROPE
    o_xa, o_ga, o_cq = 0, d, 2 * d
    o_ckv = o_cq + Q_RANK
    o_kpe = o_ckv + KV_RANK
    o_krot = o_kpe + LANES
    o_gb = o_krot + LANES
    o_ua = o_gb + d
    o_ub = o_ua + d

    @pl.when(pl.program_id(1) == 0)
    def _():
        buf_ref[:, 0:SUBLANES, :] = conv0_ref[...]
        hcar_ref[...] = h0_ref[...]

    x = x_ref[...]
    mod = mod_ref[...]
    shift = mod[:, :, 0:d]
    scale = mod[:, :, d:2 * d]
    h = _rms(x, pre_norm_ref[...]) * (1.0 + scale) + shift
    hb_ref[...] = h.reshape(m, d).astype(BF16)

    def proj(lo, width):
        return jnp.dot(hb_ref[...], w_in_ref[:, lo:lo + width], preferred_element_type=F32)

    cos = cos_ref[...][None]
    sin = sin_ref[...][None]

    def conv_phase(c):
        cs = slice(c * cw_, (c + 1) * cw_)
        xa = proj(o_xa + c * cw_, cw_)
        buf_ref[:, SUBLANES:SUBLANES + t, cs] = xa.reshape(bb, t, cw_)
        cw = conv_w_ref[:, cs]
        xc = conv_b_ref[:, cs] + buf_ref[:, SUBLANES - 3:SUBLANES - 3 + t, cs] * cw[0:1]
        xc = xc + buf_ref[:, SUBLANES - 2:SUBLANES - 2 + t, cs] * cw[1:2]
        xc = xc + buf_ref[:, SUBLANES - 1:SUBLANES - 1 + t, cs] * cw[2:3]
        xc = xc + buf_ref[:, SUBLANES:SUBLANES + t, cs] * cw[3:4]
        tail = buf_ref[:, t:t + SUBLANES, cs]
        convo_ref[:, :, cs] = tail
        buf_ref[:, 0:SUBLANES, cs] = tail
        return xc.reshape(m, cw_)

    def gate_phase(c, xc2):
        cs = slice(c * cw_, (c + 1) * cw_)
        xcb = xc2.astype(BF16)
        r_parts, i_parts = [], []
        for n in range(blocks_per_chunk):
            g = jnp.dot(xcb[:, n * gw:(n + 1) * gw], w_gate_ref[c * blocks_per_chunk + n],
                        preferred_element_type=F32)
            r_parts.append(g[:, 0:gw])
            i_parts.append(g[:, gw:2 * gw])
        r = _sigmoid(jnp.concatenate(r_parts, axis=1) + ba_ref[:, cs])
        ig = _sigmoid(jnp.concatenate(i_parts, axis=1) + bx_ref[:, cs])
        nl = -lam_ref[:, cs]
        softplus = jnp.maximum(nl, 0.0) + jnp.log1p(jnp.exp(-jnp.abs(nl)))
        th = jnp.tanh((-0.5 * LRU_C * r) * softplus)
        inv = 1.0 / (1.0 - th)
        a = (1.0 + th) * inv
        mult = (2.0 * _sqrt_nonneg(-th)) * inv
        return a, mult * (ig * xc2)

    def scan_phase(c, a, b):
        cs = slice(c * cw_, (c + 1) * cw_)
        a4 = a.reshape(bb * groups, SUBLANES, cw_)
        b4 = b.reshape(bb * groups, SUBLANES, cw_)
        row = lax.broadcasted_iota(jnp.int32, a4.shape, 1)
        for sft in (1, 2, 4):
            keep = row >= sft
            a_sh = jnp.where(keep, pltpu.roll(a4, sft, axis=1), 1.0)
            b_sh = jnp.where(keep, pltpu.roll(b4, sft, axis=1), 0.0)
            b4 = b4 + a4 * b_sh
            a4 = a4 * a_sh
        a5 = a4.reshape(bb, groups, SUBLANES, cw_)
        b5 = b4.reshape(bb, groups, SUBLANES, cw_)
        hc = hcar_ref[:, :, cs]
        outs = []
        for g in range(groups):
            hs = a5[:, g] * hc + b5[:, g]
            outs.append(hs)
            hc = hs[:, SUBLANES - 1:SUBLANES, :]
        hcar_ref[:, :, cs] = hc
        hlast_ref[:, :, cs] = hc
        return jnp.concatenate(outs, axis=1).reshape(m, cw_)

    def out_phase(c, y_lru):
        cs = slice(c * cw_, (c + 1) * cw_)
        ga = proj(o_ga + c * cw_, cw_)
        ya_ref[:, :, cs] = (y_lru * _silu(ga)).astype(BF16).reshape(bb, t, cw_)

    def gate_out(ref, fn, off, c):
        cs = slice(c * cw_, (c + 1) * cw_)
        ref[:, :, cs] = fn(proj(off + c * cw_, cw_)).astype(BF16).reshape(bb, t, cw_)

    def q_latent(heads):
        q_nope = jnp.dot(cqn_ref[...], w_q_ref[:, heads[0] * QK_NOPE:(heads[-1] + 1) * QK_NOPE],
                         preferred_element_type=F32).astype(BF16)
        for k, hd in enumerate(heads):
            q_h = q_nope[:, k * QK_NOPE:(k + 1) * QK_NOPE]
            if not per_head:
                q_h = jnp.dot(q_h, w_uk_ref[hd], preferred_element_type=F32).astype(BF16)
            q_ref[:, hd, :, 0:q_main] = q_h.reshape(bb, t, q_main)

    def q_rotary():
        q_pe = jnp.dot(cqn_ref[...], w_q_ref[:, nq:nq + npe], preferred_element_type=F32)
        q_rot = jnp.dot(cqn_ref[...], w_q_ref[:, nq + npe:nq + 2 * npe], preferred_element_type=F32)
        lane = lax.broadcasted_iota(jnp.int32, (1, 1, LANES), 2)
        for pair in range(N_HEADS // 2):
            sl = slice(pair * LANES, (pair + 1) * LANES)
            both = q_pe[:, sl].reshape(bb, t, LANES) * cos + q_rot[:, sl].reshape(bb, t, LANES) * sin
            rs = slice(q_main, q_main + LANES)
            q_ref[:, 2 * pair, :, rs] = jnp.where(lane < QK_ROPE, both, 0.0).astype(BF16)
            q_ref[:, 2 * pair + 1, :, rs] = jnp.where(lane < QK_ROPE, 0.0, both).astype(BF16)

    def key_latents():
        ckv = _rms(proj(o_ckv, KV_RANK), kvn_ref[...])
        ckv_ref[...] = ckv.reshape(bb, t, KV_RANK)
        ckvb = ckv.astype(BF16)
        kp = proj(o_kpe, LANES).reshape(bb, t, LANES)
        kr = proj(o_krot, LANES).reshape(bb, t, LANES)
        kroped = kp * cos + kr * sin
        kpe_ref[...] = kroped[:, :, 0:QK_ROPE]
        if per_head:
            nk = N_HEADS * QK_NOPE
            k_nope = jnp.dot(ckvb, w_uk_ref[:, 0:nk], preferred_element_type=F32).astype(BF16)
            v_scr[...] = jnp.dot(ckvb, w_uk_ref[:, nk:nk + N_HEADS * V_DIM], preferred_element_type=F32)
            v_t = v_scr[...].T
            for hd in range(N_HEADS):
                kcat_ref[:, hd, :, 0:QK_NOPE] = k_nope[:, hd * QK_NOPE:(hd + 1) * QK_NOPE].reshape(
                    bb, t, QK_NOPE)
                kcat_ref[:, hd, :, QK_NOPE:QK_NOPE + LANES] = kroped.astype(BF16)
                vt_ref[0, hd, 0] = v_t[hd * V_DIM:(hd + 1) * V_DIM].astype(BF16)
        else:
            kcat_ref[:, :, 0:KV_RANK] = ckvb.reshape(bb, t, KV_RANK)
            kcat_ref[:, :, KV_RANK:K_CAT] = kroped.astype(BF16)

    cqn_ref[...] = _rms(proj(o_cq, Q_RANK), qn_ref[...]).astype(BF16)
    other = [lambda: q_latent((0, 1, 2, 3)), lambda: q_latent((4, 5, 6, 7)), q_rotary, key_latents]
    for c in range(n_chunks):
        xc2 = conv_phase(c)
        gate_out(sgb_ref, _silu, o_gb, c)
        a, b = gate_phase(c, xc2)
        gate_out(sua_ref, _sigmoid, o_ua, c)
        y_lru = scan_phase(c, a, b)
        gate_out(sub_ref, _sigmoid, o_ub, c)
        out_phase(c, y_lru)
        if c < len(other):
            other[c]()
    for f in other[n_chunks:]:
        f()


def _inproj(x, mod, conv0, h0, cos, sin, lw, *, bb, t, per_head, layer, depth, stacked):
    b, s, d = x.shape
    grid = (b // bb, s // t)
    tok = lambda w: pl.BlockSpec((bb, t, w), lambda i, j: (i, j, 0))
    tok_l = lambda w: pl.BlockSpec((None, bb, t, w), lambda i, j: (layer, i, j, 0))
    per_b = lambda r, w: pl.BlockSpec((bb, r, w), lambda i, j: (i, 0, 0))
    heads = lambda w: pl.BlockSpec((bb, N_HEADS, t, w), lambda i, j: (i, 0, j, 0))
    q_w = (QK_NOPE if per_head else KV_RANK) + LANES
    weights = [lw["w_in"], lw["w_q"], lw["w_ukv"] if per_head else lw["w_uk"], lw["w_gate"],
               lw["pre_norm"], lw["conv_w"],
               lw["conv_b"], lw["lru_ba"], lw["lru_bx"], lw["lru_lambda"], lw["q_norm"], lw["kv_norm"]]
    in_specs = ([tok(d), per_b(1, 3 * d), per_b(SUBLANES, d), per_b(1, d),
                 pl.BlockSpec((t, LANES), lambda i, j: (j, 0)),
                 pl.BlockSpec((t, LANES), lambda i, j: (j, 0))]
                + [_const_spec(w.shape) for w in weights])
    out_shape = [jax.ShapeDtypeStruct((b, s, d), BF16)] * 4 + [
        jax.ShapeDtypeStruct((b, N_HEADS, s, q_w), BF16),
        jax.ShapeDtypeStruct((depth, b, s, KV_RANK), F32),
        jax.ShapeDtypeStruct((depth, b, s, QK_ROPE), F32),
        jax.ShapeDtypeStruct((b, N_HEADS, s, q_w) if per_head else (b, s, K_CAT), BF16),
        jax.ShapeDtypeStruct((b, SUBLANES, d), F32),
        jax.ShapeDtypeStruct((b, 1, d), F32),
    ]
    out_specs = [tok(d)] * 4 + [
        heads(q_w), tok_l(KV_RANK), tok_l(QK_ROPE), heads(q_w) if per_head else tok(K_CAT),
        per_b(SUBLANES, d), per_b(1, d),
    ]
    aliased = [] if stacked is None else list(stacked)
    n_in = len(in_specs)
    in_specs = in_specs + [pl.BlockSpec(memory_space=pl.ANY)] * len(aliased)
    aliases = {n_in + k: 5 + k for k in range(len(aliased))}
    if per_head:
        assert bb == 1
        out_shape.append(jax.ShapeDtypeStruct((b, N_HEADS, s // t, V_DIM, t), BF16))
        out_specs.append(pl.BlockSpec((1, N_HEADS, 1, V_DIM, t), lambda i, j: (i, 0, j, 0, 0)))
    return pl.pallas_call(
        functools.partial(_inproj_kernel, per_head=per_head, n_alias=len(aliased)),
        grid=grid,
        in_specs=in_specs,
        out_specs=out_specs,
        out_shape=out_shape,
        input_output_aliases=aliases,
        scratch_shapes=[pltpu.VMEM((bb, t + SUBLANES, d), F32), pltpu.VMEM((bb, 1, d), F32),
                        pltpu.VMEM((bb * t, d), BF16), pltpu.VMEM((bb * t, Q_RANK), BF16)]
        + ([pltpu.VMEM((bb * t, N_HEADS * V_DIM), F32)] if per_head else []),
        compiler_params=pltpu.CompilerParams(
            dimension_semantics=("parallel", "arbitrary"), vmem_limit_bytes=VMEM_LIMIT_BYTES),
        name="inproj_lru",
    )(x, mod, conv0, h0, cos, sin, *weights, *aliased)


def _attn_heads_kernel(q_ref, k_ref, vt_ref, sgb_ref, bias_ref, yb_ref, m_scr, l_scr, acc_scr, s_scr,
                       *, tk):
    n_groups = q_ref.shape[1]
    n_sub = q_ref.shape[2] // tk
    gran = vt_ref.shape[-1]

    m_scr[...] = jnp.full(m_scr.shape, NEG_INF, F32)
    l_scr[...] = jnp.zeros(l_scr.shape, F32)
    acc_scr[...] = jnp.zeros(acc_scr.shape, F32)

    def qk(u, j, g):
        k = k_ref[0, g, pl.ds(pl.multiple_of(j * tk, tk), tk), :]
        return lax.dot_general(k, q_ref[0, g, u * tk:(u + 1) * tk, :], (((1,), (1,)), ((), ())),
                               preferred_element_type=F32)

    def step(u, j, masked, last_tile):
        for g in range(n_groups):
            if g + 1 < n_groups:
                s_scr[g + 1] = qk(u, j, g + 1)
            elif not last_tile:
                s_scr[0] = qk(u, j + 1, 0)
            s = s_scr[g]
            if masked:
                s = s + bias_ref[...]
            m_prev = m_scr[u, g]
            m_next = jnp.maximum(m_prev, jnp.max(s, axis=0, keepdims=True))
            p = jnp.exp2(s - m_next)
            alpha = jnp.exp2(m_prev - m_next)
            l_scr[u, g] = alpha * l_scr[u, g] + jnp.sum(p, axis=0, keepdims=True)
            m_scr[u, g] = m_next
            vt = jnp.concatenate([vt_ref[0, g, j * (tk // gran) + i] for i in range(tk // gran)], axis=1)
            pv = jnp.dot(vt, p.astype(BF16), preferred_element_type=F32)
            acc_scr[u, g] = acc_scr[u, g] * alpha + pv

    s_scr[0] = qk(0, 0, 0)
    for u in range(n_sub):
        last = pl.program_id(2) * n_sub + u
        lax.fori_loop(0, last, lambda j, c, u=u: (step(u, j, False, False), c)[1], 0)
        step(u, last, True, True)
        if u + 1 < n_sub:
            s_scr[0] = qk(u + 1, 0, 0)
        for g in range(n_groups):
            o = (acc_scr[u, g] * (1.0 / l_scr[u, g])).T
            sl = slice(g * V_DIM, (g + 1) * V_DIM)
            yb_ref[0, u * tk:(u + 1) * tk, sl] = (o * sgb_ref[0, u * tk:(u + 1) * tk, sl]).astype(BF16)


def _attention_heads(q, k, vt, sgb, *, tq, tk, heads_per_step=4):
    b, n_heads, sq, qw = q.shape
    sk = k.shape[2]
    d = sgb.shape[-1]
    gran = vt.shape[-1]
    hps = heads_per_step
    n_sub = max(n for n in (4, 2, 1) if sq % (n * tq) == 0)
    tqs = n_sub * tq
    assert n_heads % hps == 0 and sq % tqs == 0 and sk % tk == 0 and tk % gran == 0
    assert vt.shape[2] * gran == sk and (hps * V_DIM) % LANES == 0
    assert tk == tq and tq % CHUNK == 0 and sq == sk
    k_idx = lax.broadcasted_iota(jnp.int32, (tk, tq), 0)
    q_idx = lax.broadcasted_iota(jnp.int32, (tk, tq), 1)
    diag_bias = jnp.where(k_idx // CHUNK <= q_idx // CHUNK, 0.0, NEG_INF).astype(F32)
    kern = functools.partial(_attn_heads_kernel, tk=tk)
    return pl.pallas_call(
        kern,
        grid=(b, n_heads // hps, sq // tqs),
        in_specs=[pl.BlockSpec((1, hps, tqs, qw), lambda i, h, j: (i, h, j, 0)),
                  pl.BlockSpec((1, hps, sk, qw), lambda i, h, j: (i, h, 0, 0)),
                  pl.BlockSpec((1, hps) + vt.shape[2:], lambda i, h, j: (i, h, 0, 0, 0)),
                  pl.BlockSpec((1, tqs, hps * V_DIM), lambda i, h, j: (i, j, h)),
                  _const_spec((tk, tq))],
        out_specs=pl.BlockSpec((1, tqs, hps * V_DIM), lambda i, h, j: (i, j, h)),
        out_shape=jax.ShapeDtypeStruct((b, sq, d), BF16),
        scratch_shapes=[pltpu.VMEM((n_sub, hps, 1, tq), F32), pltpu.VMEM((n_sub, hps, 1, tq), F32),
                        pltpu.VMEM((n_sub, hps, V_DIM, tq), F32), pltpu.VMEM((hps, tk, tq), F32)],
        compiler_params=pltpu.CompilerParams(
            dimension_semantics=("parallel", "parallel", "arbitrary"),
            vmem_limit_bytes=VMEM_LIMIT_BYTES),
        name="mla_attention_heads",
    )(q, k, vt, sgb, diag_bias)


def _attn_cached_kernel(q_ref, ckv_ref, kpe_ref, knew_ref, sgb_ref, w_uv_ref, yb_ref,
                        m_scr, l_scr, acc_scr, *, past_len, sq, tk):
    m = N_HEADS * sq
    q = q_ref[0].reshape(m, K_CAT)
    q_lat = q[:, 0:KV_RANK]
    q_rope = q[:, KV_RANK:K_CAT].astype(F32)
    q_rope = (q_rope[:, 0:QK_ROPE] + q_rope[:, QK_ROPE:LANES]).astype(BF16)
    q_pos = past_len + (lax.broadcasted_iota(jnp.int32, (m, 1), 0) & (sq - 1))
    k_lim = (q_pos // CHUNK + 1) * CHUNK

    m_scr[...] = jnp.full(m_scr.shape, NEG_INF, F32)
    l_scr[...] = jnp.zeros(l_scr.shape, F32)
    acc_scr[...] = jnp.zeros(acc_scr.shape, F32)
    nt = (((1,), (1,)), ((), ()))

    def update(s, visible, v):
        s = jnp.where(visible, s, NEG_INF)
        m_prev = m_scr[...]
        m_next = jnp.maximum(m_prev, jnp.max(s, axis=-1, keepdims=True))
        p = jnp.exp2(s - jnp.concatenate([m_next] * (s.shape[1] // LANES), axis=1))
        alpha = jnp.exp2(m_prev - m_next)
        l_scr[...] = alpha * l_scr[...] + jnp.sum(p, axis=-1, keepdims=True)
        m_scr[...] = m_next
        pv = jnp.dot(p.astype(BF16), v, preferred_element_type=F32)
        acc_scr[...] = acc_scr[...] * jnp.concatenate([alpha] * (KV_RANK // LANES), axis=1) + pv

    def past_body(j, c):
        off = pl.multiple_of(j * tk, tk)
        lat = ckv_ref[0, 0, pl.ds(off, tk), :].astype(BF16)
        pe = kpe_ref[0, 0, pl.ds(off, tk), :].astype(BF16)
        s = (lax.dot_general(q_lat, lat, nt, preferred_element_type=F32)
             + lax.dot_general(q_rope, pe, nt, preferred_element_type=F32))
        k_pos = off + lax.broadcasted_iota(jnp.int32, (m, tk), 1)
        update(s, k_pos < k_lim, lat)
        return c

    lax.fori_loop(0, past_len // tk, past_body, 0)

    kn = knew_ref[0]
    s = lax.dot_general(q, kn, nt, preferred_element_type=F32)
    idx = lax.broadcasted_iota(jnp.int32, (m, kn.shape[0]), 1)
    update(s, jnp.where(idx < sq, past_len + idx, k_lim) < k_lim, kn[:, 0:KV_RANK])

    inv_l = 1.0 / l_scr[...]
    o = (acc_scr[...] * jnp.concatenate([inv_l] * (KV_RANK // LANES), axis=1)).astype(BF16)
    sgb = sgb_ref[0]
    for hd in range(N_HEADS):
        att = jnp.dot(o[hd * sq:(hd + 1) * sq], w_uv_ref[hd], preferred_element_type=F32)
        sl = slice(hd * V_DIM, (hd + 1) * V_DIM)
        yb_ref[0, :, sl] = (att * sgb[:, sl]).astype(BF16)


def _attention_cached(q, cache_ckv, cache_kpe, layer, knew, sgb, w_uv, *, tk):
    b, _, sq, _ = q.shape
    past_len = cache_ckv.shape[2]
    d = sgb.shape[-1]
    m = N_HEADS * sq
    assert sq & (sq - 1) == 0 and past_len % tk == 0 and tk % LANES == 0 and knew.shape[1] % LANES == 0
    kern = functools.partial(_attn_cached_kernel, past_len=past_len, sq=sq, tk=tk)
    return pl.pallas_call(
        kern,
        grid=(b,),
        in_specs=[pl.BlockSpec((1, N_HEADS, sq, K_CAT), lambda i: (i, 0, 0, 0)),
                  pl.BlockSpec((1, 1, past_len, KV_RANK), lambda i: (layer, i, 0, 0)),
                  pl.BlockSpec((1, 1, past_len, QK_ROPE), lambda i: (layer, i, 0, 0)),
                  pl.BlockSpec((1,) + knew.shape[1:], lambda i: (i, 0, 0)),
                  pl.BlockSpec((1, sq, d), lambda i: (i, 0, 0)),
                  _const_spec(w_uv.shape)],
        out_specs=pl.BlockSpec((1, sq, d), lambda i: (i, 0, 0)),
        out_shape=jax.ShapeDtypeStruct((b, sq, d), BF16),
        scratch_shapes=[pltpu.VMEM((m, LANES), F32), pltpu.VMEM((m, LANES), F32),
                        pltpu.VMEM((m, KV_RANK), F32)],
        compiler_params=pltpu.CompilerParams(
            dimension_semantics=("parallel",), vmem_limit_bytes=VMEM_LIMIT_BYTES),
        name="mla_attention_cached",
    )(q, cache_ckv, cache_kpe, knew, sgb, w_uv)


def _outproj_kernel(x_ref, mod_ref, ya_ref, yb_ref, sua_ref, sub_ref,
                    w_a_ref, w_b_ref, w_o_ref, post_norm_ref, y_ref):
    bb, t, d = x_ref.shape
    m = bb * t
    ya = ya_ref[...].reshape(m, d)
    yb = yb_ref[...].reshape(m, d)
    pa = jnp.dot(ya, w_a_ref[...], preferred_element_type=F32)
    pb = jnp.dot(yb, w_b_ref[...], preferred_element_type=F32)
    merged = sua_ref[...].reshape(m, d).astype(F32) * pa + sub_ref[...].reshape(m, d).astype(F32) * pb
    o = jnp.dot(merged.astype(BF16), w_o_ref[...], preferred_element_type=F32)
    gate = mod_ref[...][:, :, 2 * d:3 * d]
    y_ref[...] = x_ref[...] + gate * _rms(o, post_norm_ref[...]).reshape(bb, t, d)


def _outproj(x, mod, ya, yb, sua, sub, lw, *, bb, t):
    b, s, d = x.shape
    tok = pl.BlockSpec((bb, t, d), lambda i, j: (i, j, 0))
    weights = [lw["w_branch_a"], lw["w_branch_b"], lw["w_out"], lw["post_norm"]]
    return pl.pallas_call(
        _outproj_kernel,
        grid=(b // bb, s // t),
        in_specs=[tok, pl.BlockSpec((bb, 1, 3 * d), lambda i, j: (i, 0, 0)), tok, tok, tok, tok]
        + [_const_spec(w.shape) for w in weights],
        out_specs=tok,
        out_shape=jax.ShapeDtypeStruct((b, s, d), F32),
        compiler_params=pltpu.CompilerParams(
            dimension_semantics=("parallel", "parallel"), vmem_limit_bytes=VMEM_LIMIT_BYTES),
        name="outproj",
    )(x, mod, ya, yb, sua, sub, *weights)


def _rot_half(w):
    half = w.shape[-1] // 2
    return jnp.concatenate([-w[..., half:], w[..., :half]], axis=-1)


def _twice(w):
    return jnp.concatenate([w, w], axis=-1)


def _layer_weights(l, p):
    d = p["w_in"].shape[1]
    w_in = p["w_in"][l]
    o = 0
    seg = {}
    for name, width in (("xa", d), ("ga", d), ("cq", Q_RANK), ("ckv", KV_RANK), ("kpe", QK_ROPE),
                        ("gb", d), ("ua", d), ("ub", d)):
        seg[name] = w_in[:, o:o + width]
        o += width
    w_in_ext = jnp.concatenate(
        [seg["xa"], seg["ga"], seg["cq"], seg["ckv"], _twice(seg["kpe"]),
         _twice(_rot_half(seg["kpe"])), seg["gb"], seg["ua"], seg["ub"]], axis=1)
    wq = p["w_q_up"][l].reshape(Q_RANK, N_HEADS, QK_NOPE + QK_ROPE)
    wq_pe = wq[:, :, QK_NOPE:]
    w_q = jnp.concatenate(
        [wq[:, :, :QK_NOPE].reshape(Q_RANK, -1), wq_pe.reshape(Q_RANK, -1),
         _rot_half(wq_pe).reshape(Q_RANK, -1)], axis=1)
    row = lambda v: v.reshape(1, -1)
    return {
        "w_in": w_in_ext.astype(BF16),
        "w_q": w_q.astype(BF16),
        "w_uk": jnp.transpose(p["w_uk"][l], (1, 2, 0)).astype(BF16),
        "w_ukv": jnp.concatenate([p["w_uk"][l].reshape(KV_RANK, -1), p["w_uv"][l].reshape(KV_RANK, -1)],
                                 axis=1).astype(BF16),
        "w_uv": jnp.transpose(p["w_uv"][l], (1, 0, 2)).astype(BF16),
        "w_gate": jnp.concatenate([p["lru_wa"][l], p["lru_wx"][l]], axis=-1).astype(BF16),
        "pre_norm": row(p["pre_norm"][l]), "post_norm": row(p["post_norm"][l]),
        "conv_w": p["conv_w"][l], "conv_b": row(p["conv_b"][l]),
        "lru_ba": row(p["lru_ba"][l]), "lru_bx": row(p["lru_bx"][l]),
        "lru_lambda": row(p["lru_lambda"][l]),
        "q_norm": row(p["q_norm"][l]) * Q_SCALE, "kv_norm": row(p["kv_norm"][l]),
        "w_branch_a": p["w_branch_a"][l].astype(BF16), "w_branch_b": p["w_branch_b"][l].astype(BF16),
        "w_out": p["w_out"][l].astype(BF16),
    }


def _rope_tables(pos):
    half = QK_ROPE // 2
    inv = ROPE_THETA ** (-jnp.arange(half, dtype=F32) / half)
    ang = pos.astype(F32)[:, None] * inv[None, :]
    cos, sin = jnp.cos(ang), jnp.sin(ang)
    return jnp.concatenate([cos] * (LANES // half), axis=1), jnp.concatenate([sin] * (LANES // half), axis=1)


def _pick_tile(n, target):
    t = min(n, target)
    while n % t:
        t //= 2
    return t


def kernel(x_prompt, x_sample, c_prompt, c_sample, cache_ckv, cache_kpe, state_conv, state_lru, ada_w, ada_b, pre_norm, post_norm, w_in, conv_w, conv_b, lru_wa, lru_ba, lru_wx, lru_bx, lru_lambda, q_norm, w_q_up, kv_norm, w_uk, w_uv, w_branch_a, w_branch_b, w_out):
    p = dict(ada_w=ada_w, ada_b=ada_b, pre_norm=pre_norm, post_norm=post_norm, w_in=w_in,
             conv_w=conv_w, conv_b=conv_b, lru_wa=lru_wa, lru_ba=lru_ba, lru_wx=lru_wx, lru_bx=lru_bx,
             lru_lambda=lru_lambda, q_norm=q_norm, w_q_up=w_q_up, kv_norm=kv_norm, w_uk=w_uk,
             w_uv=w_uv, w_branch_a=w_branch_a, w_branch_b=w_branch_b, w_out=w_out)
    depth = w_in.shape[0]
    b_p, s_p, d = x_prompt.shape
    b_s, s_s, _ = x_sample.shape
    past_len = cache_ckv.shape[2]
    assert s_p % SUBLANES == 0 and s_s % SUBLANES == 0 and s_p >= SUBLANES and s_s >= SUBLANES

    mods = _modulation(jnp.concatenate([c_prompt, c_sample], axis=0), ada_w, ada_b)
    cos_p, sin_p = _rope_tables(jnp.arange(s_p, dtype=jnp.int32))
    cos_s, sin_s = _rope_tables(past_len + jnp.arange(s_s, dtype=jnp.int32))
    conv0_p = jnp.zeros((b_p, SUBLANES, d), F32)
    h0_p = jnp.zeros((b_p, 1, d), F32)

    t_in_p = _pick_tile(s_p, 256)
    t_out_p = _pick_tile(s_p, 1024)
    tq_p = _pick_tile(s_p, 512)
    tk_p = _pick_tile(s_p, 512)
    tk_s = _pick_tile(past_len, 1024)

    yp, ys = x_prompt, x_sample
    outs_p, outs_s = [], []
    lat_p = lat_s = None
    for l in range(depth):
        lw = _layer_weights(l, p)

        mod_p = mods[l, :b_p, None, :]
        ya, sgb, sua, sub, q, ckv, kpe, k_heads, convo, hlast, vt = _inproj(
            yp, mod_p, conv0_p, h0_p, cos_p, sin_p, lw, bb=1, t=t_in_p, per_head=True,
            layer=l, depth=depth, stacked=lat_p)
        lat_p = (ckv, kpe)
        yb = _attention_heads(q, k_heads, vt, sgb, tq=tq_p, tk=tk_p)
        yp = _outproj(yp, mod_p, ya, yb, sua, sub, lw, bb=1, t=t_out_p)
        outs_p.append((convo[:, SUBLANES - (CONV_W - 1):, :], hlast[:, 0, :]))

        mod_s = mods[l, b_p:, None, :]
        conv0_s = jnp.pad(state_conv[l], ((0, 0), (SUBLANES - (CONV_W - 1), 0), (0, 0)))
        ya, sgb, sua, sub, q, ckv, kpe, kcat, convo, hlast = _inproj(
            ys, mod_s, conv0_s, state_lru[l][:, None, :], cos_s, sin_s, lw, bb=b_s, t=s_s,
            per_head=False, layer=l, depth=depth, stacked=lat_s)
        lat_s = (ckv, kpe)
        knew = jnp.pad(kcat, ((0, 0), (0, -s_s % LANES), (0, 0)))
        yb = _attention_cached(q, cache_ckv, cache_kpe, l, knew, sgb, lw["w_uv"], tk=tk_s)
        ys = _outproj(ys, mod_s, ya, yb, sua, sub, lw, bb=b_s, t=s_s)
        outs_s.append((convo[:, SUBLANES - (CONV_W - 1):, :], hlast[:, 0, :]))

    stack = lambda outs, k: jnp.stack([o[k] for o in outs])
    return (yp, ys,
            lat_p[0], lat_p[1], stack(outs_p, 0), stack(outs_p, 1),
            lat_s[0], lat_s[1], stack(outs_s, 0), stack(outs_s, 1))
```

```python
import functools
import math

import jax
import jax.numpy as jnp
from jax import lax
from jax.experimental import pallas as pl
from jax.experimental.pallas import tpu as pltpu

F32 = jnp.float32
BF16 = jnp.bfloat16

EPS = 1e-6
NEG_INF = -1e30
CHUNK = 64
LRU_C = 8.0
ROPE_THETA = 10000.0

LANES = 128
SUBLANES = 8
MXU_DIM = 256
VMEM_LIMIT_BYTES = 56 * 1024 * 1024

CONV_W = 4
N_HEADS = 8
QK_NOPE = 128
QK_ROPE = 64
V_DIM = 128
KV_RANK = 256
Q_RANK = 768
LRU_BLOCKS = 8
K_CAT = KV_RANK + LANES
Q_SCALE = (QK_NOPE + QK_ROPE) ** -0.5 * math.log2(math.e)


def _sigmoid(x):
    return jax.nn.sigmoid(x)


def _silu(x):
    return x * _sigmoid(x)


def _sqrt_nonneg(x):
    return jnp.where(x > 0.0, x * lax.rsqrt(x), 0.0)


def _rms(x, g):
    return x * lax.rsqrt(jnp.mean(x * x, axis=-1, keepdims=True) + EPS) * g


def _const_spec(shape):
    nd = len(shape)
    return pl.BlockSpec(shape, lambda *_: (0,) * nd, pipeline_mode=pl.Buffered(1))


def _mod_kernel(c_ref, w_ref, b_ref, o_ref):
    c = c_ref[...]
    o_ref[0] = jnp.dot(_silu(c), w_ref[0], preferred_element_type=F32,
                       precision=lax.Precision.HIGHEST) + b_ref[0]


def _modulation(c_all, ada_w, ada_b):
    depth, d, d3 = ada_w.shape
    n = c_all.shape[0]
    nblk = d3 // d
    return pl.pallas_call(
        _mod_kernel,
        grid=(depth, nblk),
        in_specs=[pl.BlockSpec((n, d), lambda l, j: (0, 0)),
                  pl.BlockSpec((1, d, d), lambda l, j: (l, 0, j)),
                  pl.BlockSpec((1, 1, d), lambda l, j: (l, 0, j))],
        out_specs=pl.BlockSpec((1, n, d), lambda l, j: (l, 0, j)),
        out_shape=jax.ShapeDtypeStruct((depth, n, d3), F32),
        name="adaln_mod",
    )(c_all, ada_w, ada_b.reshape(depth, 1, d3))


def _inproj_kernel(x_ref, mod_ref, conv0_ref, h0_ref, cos_ref, sin_ref,
                   w_in_ref, w_q_ref, w_uk_ref, w_gate_ref,
                   pre_norm_ref, conv_w_ref, conv_b_ref, ba_ref, bx_ref, lam_ref, qn_ref, kvn_ref,
                   *rest, per_head, n_alias):
    (ya_ref, sgb_ref, sua_ref, sub_ref, q_ref, ckv_ref, kpe_ref, kcat_ref,
     convo_ref, hlast_ref, *rest) = rest[n_alias:]
    if per_head:
        vt_ref, buf_ref, hcar_ref, hb_ref, cqn_ref, v_scr = rest
    else:
        buf_ref, hcar_ref, hb_ref, cqn_ref = rest
    bb, t, d = x_ref.shape
    q_main = QK_NOPE if per_head else KV_RANK
    m = bb * t
    groups = t // SUBLANES
    gw = d // LRU_BLOCKS
    cw_ = MXU_DIM
    n_chunks = d // cw_
    blocks_per_chunk = cw_ // gw
    nq = N_HEADS * QK_NOPE
    npe = N_HEADS * QK_ROPE
    o_xa, o_ga, o_cq = 0, d, 2 * d
    o_ckv = o_cq + Q_RANK
    o_kpe = o_ckv + KV_RANK
    o_krot = o_kpe + LANES
    o_gb = o_krot + LANES
    o_ua = o_gb + d
    o_ub = o_ua + d

    @pl.when(pl.program_id(1) == 0)
    def _():
        buf_ref[:, 0:SUBLANES, :] = conv0_ref[...]
        hcar_ref[...] = h0_ref[...]

    x = x_ref[...]
    mod = mod_ref[...]
    shift = mod[:, :, 0:d]
    scale = mod[:, :, d:2 * d]
    h = _rms(x, pre_norm_ref[...]) * (1.0 + scale) + shift
    hb_ref[...] = h.reshape(m, d).astype(BF16)

    def proj(lo, width):
        return jnp.dot(hb_ref[...], w_in_ref[:, lo:lo + width], preferred_element_type=F32)

    cos = cos_ref[...][None]
    sin = sin_ref[...][None]

    def conv_phase(c):
        cs = slice(c * cw_, (c + 1) * cw_)
        xa = proj(o_xa + c * cw_, cw_)
        buf_ref[:, SUBLANES:SUBLANES + t, cs] = xa.reshape(bb, t, cw_)
        cw = conv_w_ref[:, cs]
        xc = conv_b_ref[:, cs] + buf_ref[:, SUBLANES - 3:SUBLANES - 3 + t, cs] * cw[0:1]
        xc = xc + buf_ref[:, SUBLANES - 2:SUBLANES - 2 + t, cs] * cw[1:2]
        xc = xc + buf_ref[:, SUBLANES - 1:SUBLANES - 1 + t, cs] * cw[2:3]
        xc = xc + buf_ref[:, SUBLANES:SUBLANES + t, cs] * cw[3:4]
        tail = buf_ref[:, t:t + SUBLANES, cs]
        convo_ref[:, :, cs] = tail
        buf_ref[:, 0:SUBLANES, cs] = tail
        return xc.reshape(m, cw_)

    def gate_phase(c, xc2):
        cs = slice(c * cw_, (c + 1) * cw_)
        xcb = xc2.astype(BF16)
        r_parts, i_parts = [], []
        for n in range(blocks_per_chunk):
            g = jnp.dot(xcb[:, n * gw:(n + 1) * gw], w_gate_ref[c * blocks_per_chunk + n],
                        preferred_element_type=F32)
            r_parts.append(g[:, 0:gw])
            i_parts.append(g[:, gw:2 * gw])
        r = _sigmoid(jnp.concatenate(r_parts, axis=1) + ba_ref[:, cs])
        ig = _sigmoid(jnp.concatenate(i_parts, axis=1) + bx_ref[:, cs])
        nl = -lam_ref[:, cs]
        softplus = jnp.maximum(nl, 0.0) + jnp.log1p(jnp.exp(-jnp.abs(nl)))
        th = jnp.tanh((-0.5 * LRU_C * r) * softplus)
        inv = 1.0 / (1.0 - th)
        a = (1.0 + th) * inv
        mult = (2.0 * _sqrt_nonneg(-th)) * inv
        return a, mult * (ig * xc2)

    def scan_phase(c, a, b):
        cs = slice(c * cw_, (c + 1) * cw_)
        a4 = a.reshape(bb * groups, SUBLANES, cw_)
        b4 = b.reshape(bb * groups, SUBLANES, cw_)
        row = lax.broadcasted_iota(jnp.int32, a4.shape, 1)
        for sft in (1, 2, 4):
            keep = row >= sft
            a_sh = jnp.where(keep, pltpu.roll(a4, sft, axis=1), 1.0)
            b_sh = jnp.where(keep, pltpu.roll(b4, sft, axis=1), 0.0)
            b4 = b4 + a4 * b_sh
            a4 = a4 * a_sh
        a5 = a4.reshape(bb, groups, SUBLANES, cw_)
        b5 = b4.reshape(bb, groups, SUBLANES, cw_)
        hc = hcar_ref[:, :, cs]
        outs = []
        for g in range(groups):
            hs = a5[:, g] * hc + b5[:, g]
            outs.append(hs)
            hc = hs[:, SUBLANES - 1:SUBLANES, :]
        hcar_ref[:, :, cs] = hc
        hlast_ref[:, :, cs] = hc
        return jnp.concatenate(outs, axis=1).reshape(m, cw_)

    def out_phase(c, y_lru):
        cs = slice(c * cw_, (c + 1) * cw_)
        ga = proj(o_ga + c * cw_, cw_)
        ya_ref[:, :, cs] = (y_lru * _silu(ga)).astype(BF16).reshape(bb, t, cw_)

    def gate_out(ref, fn, off, c):
        cs = slice(c * cw_, (c + 1) * cw_)
        ref[:, :, cs] = fn(proj(off + c * cw_, cw_)).astype(BF16).reshape(bb, t, cw_)

    def q_latent(heads):
        q_nope = jnp.dot(cqn_ref[...], w_q_ref[:, heads[0] * QK_NOPE:(heads[-1] + 1) * QK_NOPE],
                         preferred_element_type=F32).astype(BF16)
        for k, hd in enumerate(heads):
            q_h = q_nope[:, k * QK_NOPE:(k + 1) * QK_NOPE]
            if not per_head:
                q_h = jnp.dot(q_h, w_uk_ref[hd], preferred_element_type=F32).astype(BF16)
            q_ref[:, hd, :, 0:q_main] = q_h.reshape(bb, t, q_main)

    def q_rotary():
        q_pe = jnp.dot(cqn_ref[...], w_q_ref[:, nq:nq + npe], preferred_element_type=F32)
        q_rot = jnp.dot(cqn_ref[...], w_q_ref[:, nq + npe:nq + 2 * npe], preferred_element_type=F32)
        lane = lax.broadcasted_iota(jnp.int32, (1, 1, LANES), 2)
        for pair in range(N_HEADS // 2):
            sl = slice(pair * LANES, (pair + 1) * LANES)
            both = q_pe[:, sl].reshape(bb, t, LANES) * cos + q_rot[:, sl].reshape(bb, t, LANES) * sin
            rs = slice(q_main, q_main + LANES)
            q_ref[:, 2 * pair, :, rs] = jnp.where(lane < QK_ROPE, both, 0.0).astype(BF16)
            q_ref[:, 2 * pair + 1, :, rs] = jnp.where(lane < QK_ROPE, 0.0, both).astype(BF16)

    def key_latents():
        ckv = _rms(proj(o_ckv, KV_RANK), kvn_ref[...])
        ckv_ref[...] = ckv.reshape(bb, t, KV_RANK)
        ckvb = ckv.astype(BF16)
        kp = proj(o_kpe, LANES).reshape(bb, t, LANES)
        kr = proj(o_krot, LANES).reshape(bb, t, LANES)
        kroped = kp * cos + kr * sin
        kpe_ref[...] = kroped[:, :, 0:QK_ROPE]
        if per_head:
            nk = N_HEADS * QK_NOPE
            k_nope = jnp.dot(ckvb, w_uk_ref[:, 0:nk], preferred_element_type=F32).astype(BF16)
            v_scr[...] = jnp.dot(ckvb, w_uk_ref[:, nk:nk + N_HEADS * V_DIM], preferred_element_type=F32)
            v_t = v_scr[...].T
            for hd in range(N_HEADS):
                kcat_ref[:, hd, :, 0:QK_NOPE] = k_nope[:, hd * QK_NOPE:(hd + 1) * QK_NOPE].reshape(
                    bb, t, QK_NOPE)
                kcat_ref[:, hd, :, QK_NOPE:QK_NOPE + LANES] = kroped.astype(BF16)
                vt_ref[0, hd, 0] = v_t[hd * V_DIM:(hd + 1) * V_DIM].astype(BF16)
        else:
            kcat_ref[:, :, 0:KV_RANK] = ckvb.reshape(bb, t, KV_RANK)
            kcat_ref[:, :, KV_RANK:K_CAT] = kroped.astype(BF16)

    cqn_ref[...] = _rms(proj(o_cq, Q_RANK), qn_ref[...]).astype(BF16)
    other = [lambda: q_latent((0, 1, 2, 3)), lambda: q_latent((4, 5, 6, 7)), q_rotary, key_latents]
    for c in range(n_chunks):
        xc2 = conv_phase(c)
        gate_out(sgb_ref, _silu, o_gb, c)
        a, b = gate_phase(c, xc2)
        gate_out(sua_ref, _sigmoid, o_ua, c)
        y_lru = scan_phase(c, a, b)
        gate_out(sub_ref, _sigmoid, o_ub, c)
        out_phase(c, y_lru)
        if c < len(other):
            other[c]()
    for f in other[n_chunks:]:
        f()


def _inproj(x, mod, conv0, h0, cos, sin, lw, *, bb, t, per_head, layer, depth, stacked):
    b, s, d = x.shape
    grid = (b // bb, s // t)
    tok = lambda w: pl.BlockSpec((bb, t, w), lambda i, j: (i, j, 0))
    tok_l = lambda w: pl.BlockSpec((None, bb, t, w), lambda i, j: (layer, i, j, 0))
    per_b = lambda r, w: pl.BlockSpec((bb, r, w), lambda i, j: (i, 0, 0))
    heads = lambda w: pl.BlockSpec((bb, N_HEADS, t, w), lambda i, j: (i, 0, j, 0))
    q_w = (QK_NOPE if per_head else KV_RANK) + LANES
    weights = [lw["w_in"], lw["w_q"], lw["w_ukv"] if per_head else lw["w_uk"], lw["w_gate"],
               lw["pre_norm"], lw["conv_w"],
               lw["conv_b"], lw["lru_ba"], lw["lru_bx"], lw["lru_lambda"], lw["q_norm"], lw["kv_norm"]]
    in_specs = ([tok(d), per_b(1, 3 * d), per_b(SUBLANES, d), per_b(1, d),
                 pl.BlockSpec((t, LANES), lambda i, j: (j, 0)),
                 pl.BlockSpec((t, LANES), lambda i, j: (j, 0))]
                + [_const_spec(w.shape) for w in weights])
    out_shape = [jax.ShapeDtypeStruct((b, s, d), BF16)] * 4 + [
        jax.ShapeDtypeStruct((b, N_HEADS, s, q_w), BF16),
        jax.ShapeDtypeStruct((depth, b, s, KV_RANK), F32),
        jax.ShapeDtypeStruct((depth, b, s, QK_ROPE), F32),
        jax.ShapeDtypeStruct((b, N_HEADS, s, q_w) if per_head else (b, s, K_CAT), BF16),
        jax.ShapeDtypeStruct((b, SUBLANES, d), F32),
        jax.ShapeDtypeStruct((b, 1, d), F32),
    ]
    out_specs = [tok(d)] * 4 + [
        heads(q_w), tok_l(KV_RANK), tok_l(QK_ROPE), heads(q_w) if per_head else tok(K_CAT),
        per_b(SUBLANES, d), per_b(1, d),
    ]
    aliased = [] if stacked is None else list(stacked)
    n_in = len(in_specs)
    in_specs = in_specs + [pl.BlockSpec(memory_space=pl.ANY)] * len(aliased)
    aliases = {n_in + k: 5 + k for k in range(len(aliased))}
    if per_head:
        assert bb == 1
        out_shape.append(jax.ShapeDtypeStruct((b, N_HEADS, s // t, V_DIM, t), BF16))
        out_specs.append(pl.BlockSpec((1, N_HEADS, 1, V_DIM, t), lambda i, j: (i, 0, j, 0, 0)))
    return pl.pallas_call(
        functools.partial(_inproj_kernel, per_head=per_head, n_alias=len(aliased)),
        grid=grid,
        in_specs=in_specs,
        out_specs=out_specs,
        out_shape=out_shape,
        input_output_aliases=aliases,
        scratch_shapes=[pltpu.VMEM((bb, t + SUBLANES, d), F32), pltpu.VMEM((bb, 1, d), F32),
                        pltpu.VMEM((bb * t, d), BF16), pltpu.VMEM((bb * t, Q_RANK), BF16)]
        + ([pltpu.VMEM((bb * t, N_HEADS * V_DIM), F32)] if per_head else []),
        compiler_params=pltpu.CompilerParams(
            dimension_semantics=("parallel", "arbitrary"), vmem_limit_bytes=VMEM_LIMIT_BYTES),
        name="inproj_lru",
    )(x, mod, conv0, h0, cos, sin, *weights, *aliased)


def _attn_heads_kernel(q_ref, k_ref, vt_ref, sgb_ref, bias_ref, yb_ref, m_scr, l_scr, acc_scr, s_scr,
                       *, tk):
    n_groups = q_ref.shape[1]
    n_sub = q_ref.shape[2] // tk
    gran = vt_ref.shape[-1]

    m_scr[...] = jnp.full(m_scr.shape, NEG_INF, F32)
    l_scr[...] = jnp.zeros(l_scr.shape, F32)
    acc_scr[...] = jnp.zeros(acc_scr.shape, F32)

    def qk(u, j, g):
        k = k_ref[0, g, pl.ds(pl.multiple_of(j * tk, tk), tk), :]
        return lax.dot_general(k, q_ref[0, g, u * tk:(u + 1) * tk, :], (((1,), (1,)), ((), ())),
                               preferred_element_type=F32)

    def step(u, j, masked, last_tile):
        for g in range(n_groups):
            if g + 1 < n_groups:
                s_scr[g + 1] = qk(u, j, g + 1)
            elif not last_tile:
                s_scr[0] = qk(u, j + 1, 0)
            s = s_scr[g]
            if masked:
                s = s + bias_ref[...]
            m_prev = m_scr[u, g]
            m_next = jnp.maximum(m_prev, jnp.max(s, axis=0, keepdims=True))
            p = jnp.exp2(s - m_next)
            alpha = jnp.exp2(m_prev - m_next)
            l_scr[u, g] = alpha * l_scr[u, g] + jnp.sum(p, axis=0, keepdims=True)
            m_scr[u, g] = m_next
            vt = jnp.concatenate([vt_ref[0, g, j * (tk // gran) + i] for i in range(tk // gran)], axis=1)
            pv = jnp.dot(vt, p.astype(BF16), preferred_element_type=F32)
            acc_scr[u, g] = acc_scr[u, g] * alpha + pv

    s_scr[0] = qk(0, 0, 0)
    for u in range(n_sub):
        last = pl.program_id(2) * n_sub + u

        def pair(jj, c, u=u):
            step(u, 2 * jj, False, False)
            step(u, 2 * jj + 1, False, False)
            return c

        lax.fori_loop(0, last // 2, pair, 0)
        if n_sub % 2 == 0:
            if u % 2 == 1:
                step(u, last - 1, False, False)
        else:
            @pl.when(last % 2 == 1)
            def _(u=u, last=last):
                step(u, last - 1, False, False)
        step(u, last, True, True)
        if u + 1 < n_sub:
            s_scr[0] = qk(u + 1, 0, 0)
        for g in range(n_groups):
            o = (acc_scr[u, g] * (1.0 / l_scr[u, g])).T
            sl = slice(g * V_DIM, (g + 1) * V_DIM)
            yb_ref[0, u * tk:(u + 1) * tk, sl] = (o * sgb_ref[0, u * tk:(u + 1) * tk, sl]).astype(BF16)


def _attention_heads(q, k, vt, sgb, *, tq, tk, heads_per_step=4):
    b, n_heads, sq, qw = q.shape
    sk = k.shape[2]
    d = sgb.shape[-1]
    gran = vt.shape[-1]
    hps = heads_per_step
    n_sub = max(n for n in (4, 2, 1) if sq % (n * tq) == 0)
    tqs = n_sub * tq
    assert n_heads % hps == 0 and sq % tqs == 0 and sk % tk == 0 and tk % gran == 0
    assert vt.shape[2] * gran == sk and (hps * V_DIM) % LANES == 0
    assert tk == tq and tq % CHUNK == 0 and sq == sk
    k_idx = lax.broadcasted_iota(jnp.int32, (tk, tq), 0)
    q_idx = lax.broadcasted_iota(jnp.int32, (tk, tq), 1)
    diag_bias = jnp.where(k_idx // CHUNK <= q_idx // CHUNK, 0.0, NEG_INF).astype(F32)
    kern = functools.partial(_attn_heads_kernel, tk=tk)
    return pl.pallas_call(
        kern,
        grid=(b, n_heads // hps, sq // tqs),
        in_specs=[pl.BlockSpec((1, hps, tqs, qw), lambda i, h, j: (i, h, j, 0)),
                  pl.BlockSpec((1, hps, sk, qw), lambda i, h, j: (i, h, 0, 0)),
                  pl.BlockSpec((1, hps) + vt.shape[2:], lambda i, h, j: (i, h, 0, 0, 0)),
                  pl.BlockSpec((1, tqs, hps * V_DIM), lambda i, h, j: (i, j, h)),
                  _const_spec((tk, tq))],
        out_specs=pl.BlockSpec((1, tqs, hps * V_DIM), lambda i, h, j: (i, j, h)),
        out_shape=jax.ShapeDtypeStruct((b, sq, d), BF16),
        scratch_shapes=[pltpu.VMEM((n_sub, hps, 1, tq), F32), pltpu.VMEM((n_sub, hps, 1, tq), F32),
                        pltpu.VMEM((n_sub, hps, V_DIM, tq), F32), pltpu.VMEM((hps, tk, tq), F32)],
        compiler_params=pltpu.CompilerParams(
            dimension_semantics=("parallel", "parallel", "arbitrary"),
            vmem_limit_bytes=VMEM_LIMIT_BYTES),
        name="mla_attention_heads",
    )(q, k, vt, sgb, diag_bias)


def _attn_cached_kernel(q_ref, ckv_ref, kpe_ref, knew_ref, sgb_ref, w_uv_ref, yb_ref,
                        m_scr, l_scr, acc_scr, *, past_len, sq, tk):
    m = N_HEADS * sq
    q = q_ref[0].reshape(m, K_CAT)
    q_lat = q[:, 0:KV_RANK]
    q_rope = q[:, KV_RANK:K_CAT].astype(F32)
    q_rope = (q_rope[:, 0:QK_ROPE] + q_rope[:, QK_ROPE:LANES]).astype(BF16)
    q_pos = past_len + (lax.broadcasted_iota(jnp.int32, (m, 1), 0) & (sq - 1))
    k_lim = (q_pos // CHUNK + 1) * CHUNK

    m_scr[...] = jnp.full(m_scr.shape, NEG_INF, F32)
    l_scr[...] = jnp.zeros(l_scr.shape, F32)
    acc_scr[...] = jnp.zeros(acc_scr.shape, F32)
    nt = (((1,), (1,)), ((), ()))

    def update(s, visible, v):
        s = jnp.where(visible, s, NEG_INF)
        m_prev = m_scr[...]
        m_next = jnp.maximum(m_prev, jnp.max(s, axis=-1, keepdims=True))
        p = jnp.exp2(s - jnp.concatenate([m_next] * (s.shape[1] // LANES), axis=1))
        alpha = jnp.exp2(m_prev - m_next)
        l_scr[...] = alpha * l_scr[...] + jnp.sum(p, axis=-1, keepdims=True)
        m_scr[...] = m_next
        pv = jnp.dot(p.astype(BF16), v, preferred_element_type=F32)
        acc_scr[...] = acc_scr[...] * jnp.concatenate([alpha] * (KV_RANK // LANES), axis=1) + pv

    def past_body(j, c):
        off = pl.multiple_of(j * tk, tk)
        lat = ckv_ref[0, 0, pl.ds(off, tk), :].astype(BF16)
        pe = kpe_ref[0, 0, pl.ds(off, tk), :].astype(BF16)
        s = (lax.dot_general(q_lat, lat, nt, preferred_element_type=F32)
             + lax.dot_general(q_rope, pe, nt, preferred_element_type=F32))
        k_pos = off + lax.broadcasted_iota(jnp.int32, (m, tk), 1)
        update(s, k_pos < k_lim, lat)
        return c

    lax.fori_loop(0, past_len // tk, past_body, 0)

    kn = knew_ref[0]
    s = lax.dot_general(q, kn, nt, preferred_element_type=F32)
    idx = lax.broadcasted_iota(jnp.int32, (m, kn.shape[0]), 1)
    update(s, jnp.where(idx < sq, past_len + idx, k_lim) < k_lim, kn[:, 0:KV_RANK])

    inv_l = 1.0 / l_scr[...]
    o = (acc_scr[...] * jnp.concatenate([inv_l] * (KV_RANK // LANES), axis=1)).astype(BF16)
    sgb = sgb_ref[0]
    for hd in range(N_HEADS):
        att = jnp.dot(o[hd * sq:(hd + 1) * sq], w_uv_ref[hd], preferred_element_type=F32)
        sl = slice(hd * V_DIM, (hd + 1) * V_DIM)
        yb_ref[0, :, sl] = (att * sgb[:, sl]).astype(BF16)


def _attention_cached(q, cache_ckv, cache_kpe, layer, knew, sgb, w_uv, *, tk):
    b, _, sq, _ = q.shape
    past_len = cache_ckv.shape[2]
    d = sgb.shape[-1]
    m = N_HEADS * sq
    assert sq & (sq - 1) == 0 and past_len % tk == 0 and tk % LANES == 0 and knew.shape[1] % LANES == 0
    kern = functools.partial(_attn_cached_kernel, past_len=past_len, sq=sq, tk=tk)
    return pl.pallas_call(
        kern,
        grid=(b,),
        in_specs=[pl.BlockSpec((1, N_HEADS, sq, K_CAT), lambda i: (i, 0, 0, 0)),
                  pl.BlockSpec((1, 1, past_len, KV_RANK), lambda i: (layer, i, 0, 0)),
                  pl.BlockSpec((1, 1, past_len, QK_ROPE), lambda i: (layer, i, 0, 0)),
                  pl.BlockSpec((1,) + knew.shape[1:], lambda i: (i, 0, 0)),
                  pl.BlockSpec((1, sq, d), lambda i: (i, 0, 0)),
                  _const_spec(w_uv.shape)],
        out_specs=pl.BlockSpec((1, sq, d), lambda i: (i, 0, 0)),
        out_shape=jax.ShapeDtypeStruct((b, sq, d), BF16),
        scratch_shapes=[pltpu.VMEM((m, LANES), F32), pltpu.VMEM((m, LANES), F32),
                        pltpu.VMEM((m, KV_RANK), F32)],
        compiler_params=pltpu.CompilerParams(
            dimension_semantics=("parallel",), vmem_limit_bytes=VMEM_LIMIT_BYTES),
        name="mla_attention_cached",
    )(q, cache_ckv, cache_kpe, knew, sgb, w_uv)


def _outproj_kernel(x_ref, mod_ref, ya_ref, yb_ref, sua_ref, sub_ref,
                    w_a_ref, w_b_ref, w_o_ref, post_norm_ref, y_ref):
    bb, t, d = x_ref.shape
    m = bb * t
    ya = ya_ref[...].reshape(m, d)
    yb = yb_ref[...].reshape(m, d)
    pa = jnp.dot(ya, w_a_ref[...], preferred_element_type=F32)
    pb = jnp.dot(yb, w_b_ref[...], preferred_element_type=F32)
    merged = sua_ref[...].reshape(m, d).astype(F32) * pa + sub_ref[...].reshape(m, d).astype(F32) * pb
    o = jnp.dot(merged.astype(BF16), w_o_ref[...], preferred_element_type=F32)
    gate = mod_ref[...][:, :, 2 * d:3 * d]
    y_ref[...] = x_ref[...] + gate * _rms(o, post_norm_ref[...]).reshape(bb, t, d)


def _outproj(x, mod, ya, yb, sua, sub, lw, *, bb, t):
    b, s, d = x.shape
    tok = pl.BlockSpec((bb, t, d), lambda i, j: (i, j, 0))
    weights = [lw["w_branch_a"], lw["w_branch_b"], lw["w_out"], lw["post_norm"]]
    return pl.pallas_call(
        _outproj_kernel,
        grid=(b // bb, s // t),
        in_specs=[tok, pl.BlockSpec((bb, 1, 3 * d), lambda i, j: (i, 0, 0)), tok, tok, tok, tok]
        + [_const_spec(w.shape) for w in weights],
        out_specs=tok,
        out_shape=jax.ShapeDtypeStruct((b, s, d), F32),
        compiler_params=pltpu.CompilerParams(
            dimension_semantics=("parallel", "parallel"), vmem_limit_bytes=VMEM_LIMIT_BYTES),
        name="outproj",
    )(x, mod, ya, yb, sua, sub, *weights)


def _rot_half(w):
    half = w.shape[-1] // 2
    return jnp.concatenate([-w[..., half:], w[..., :half]], axis=-1)


def _twice(w):
    return jnp.concatenate([w, w], axis=-1)


def _layer_weights(l, p):
    d = p["w_in"].shape[1]
    w_in = p["w_in"][l].astype(BF16)
    o = 0
    seg = {}
    for name, width in (("xa", d), ("ga", d), ("cq", Q_RANK), ("ckv", KV_RANK), ("kpe", QK_ROPE),
                        ("gb", d), ("ua", d), ("ub", d)):
        seg[name] = w_in[:, o:o + width]
        o += width
    w_in_ext = jnp.concatenate(
        [seg["xa"], seg["ga"], seg["cq"], seg["ckv"], _twice(seg["kpe"]),
         _twice(_rot_half(seg["kpe"])), seg["gb"], seg["ua"], seg["ub"]], axis=1)
    wq = p["w_q_up"][l].reshape(Q_RANK, N_HEADS, QK_NOPE + QK_ROPE)
    wq_pe = wq[:, :, QK_NOPE:]
    w_q = jnp.concatenate(
        [wq[:, :, :QK_NOPE].reshape(Q_RANK, -1), wq_pe.reshape(Q_RANK, -1),
         _rot_half(wq_pe).reshape(Q_RANK, -1)], axis=1)
    row = lambda v: v.reshape(1, -1)
    return {
        "w_in": w_in_ext,
        "w_q": w_q.astype(BF16),
        "w_uk": jnp.transpose(p["w_uk"][l], (1, 2, 0)).astype(BF16),
        "w_ukv": jnp.concatenate([p["w_uk"][l].reshape(KV_RANK, -1), p["w_uv"][l].reshape(KV_RANK, -1)],
                                 axis=1).astype(BF16),
        "w_uv": jnp.transpose(p["w_uv"][l], (1, 0, 2)).astype(BF16),
        "w_gate": jnp.concatenate([p["lru_wa"][l], p["lru_wx"][l]], axis=-1).astype(BF16),
        "pre_norm": row(p["pre_norm"][l]), "post_norm": row(p["post_norm"][l]),
        "conv_w": p["conv_w"][l], "conv_b": row(p["conv_b"][l]),
        "lru_ba": row(p["lru_ba"][l]), "lru_bx": row(p["lru_bx"][l]),
        "lru_lambda": row(p["lru_lambda"][l]),
        "q_norm": row(p["q_norm"][l]) * Q_SCALE, "kv_norm": row(p["kv_norm"][l]),
        "w_branch_a": p["w_branch_a"][l].astype(BF16), "w_branch_b": p["w_branch_b"][l].astype(BF16),
        "w_out": p["w_out"][l].astype(BF16),
    }


def _rope_tables(pos):
    half = QK_ROPE // 2
    inv = ROPE_THETA ** (-jnp.arange(half, dtype=F32) / half)
    ang = pos.astype(F32)[:, None] * inv[None, :]
    cos, sin = jnp.cos(ang), jnp.sin(ang)
    return jnp.concatenate([cos] * (LANES // half), axis=1), jnp.concatenate([sin] * (LANES // half), axis=1)


def _pick_tile(n, target):
    t = min(n, target)
    while n % t:
        t //= 2
    return t


def kernel(x_prompt, x_sample, c_prompt, c_sample, cache_ckv, cache_kpe, state_conv, state_lru, ada_w, ada_b, pre_norm, post_norm, w_in, conv_w, conv_b, lru_wa, lru_ba, lru_wx, lru_bx, lru_lambda, q_norm, w_q_up, kv_norm, w_uk, w_uv, w_branch_a, w_branch_b, w_out):
    p = dict(ada_w=ada_w, ada_b=ada_b, pre_norm=pre_norm, post_norm=post_norm, w_in=w_in,
             conv_w=conv_w, conv_b=conv_b, lru_wa=lru_wa, lru_ba=lru_ba, lru_wx=lru_wx, lru_bx=lru_bx,
             lru_lambda=lru_lambda, q_norm=q_norm, w_q_up=w_q_up, kv_norm=kv_norm, w_uk=w_uk,
             w_uv=w_uv, w_branch_a=w_branch_a, w_branch_b=w_branch_b, w_out=w_out)
    depth = w_in.shape[0]
    b_p, s_p, d = x_prompt.shape
    b_s, s_s, _ = x_sample.shape
    past_len = cache_ckv.shape[2]
    assert s_p % SUBLANES == 0 and s_s % SUBLANES == 0 and s_p >= SUBLANES and s_s >= SUBLANES

    mods = _modulation(jnp.concatenate([c_prompt, c_sample], axis=0), ada_w, ada_b)
    cos_p, sin_p = _rope_tables(jnp.arange(s_p, dtype=jnp.int32))
    cos_s, sin_s = _rope_tables(past_len + jnp.arange(s_s, dtype=jnp.int32))
    conv0_p = jnp.zeros((b_p, SUBLANES, d), F32)
    h0_p = jnp.zeros((b_p, 1, d), F32)

    t_in_p = _pick_tile(s_p, 256)
    t_out_p = _pick_tile(s_p, 1024)
    tq_p = _pick_tile(s_p, 512)
    tk_p = _pick_tile(s_p, 512)
    tk_s = _pick_tile(past_len, 1024)

    yp, ys = x_prompt, x_sample
    outs_p, outs_s = [], []
    lat_p = lat_s = None
    for l in range(depth):
        lw = _layer_weights(l, p)

        mod_p = mods[l, :b_p, None, :]
        ya, sgb, sua, sub, q, ckv, kpe, k_heads, convo, hlast, vt = _inproj(
            yp, mod_p, conv0_p, h0_p, cos_p, sin_p, lw, bb=1, t=t_in_p, per_head=True,
            layer=l, depth=depth, stacked=lat_p)
        lat_p = (ckv, kpe)
        yb = _attention_heads(q, k_heads, vt, sgb, tq=tq_p, tk=tk_p)
        yp = _outproj(yp, mod_p, ya, yb, sua, sub, lw, bb=1, t=t_out_p)
        outs_p.append((convo[:, SUBLANES - (CONV_W - 1):, :], hlast[:, 0, :]))

        mod_s = mods[l, b_p:, None, :]
        conv0_s = jnp.pad(state_conv[l], ((0, 0), (SUBLANES - (CONV_W - 1), 0), (0, 0)))
        ya, sgb, sua, sub, q, ckv, kpe, kcat, convo, hlast = _inproj(
            ys, mod_s, conv0_s, state_lru[l][:, None, :], cos_s, sin_s, lw, bb=b_s, t=s_s,
            per_head=False, layer=l, depth=depth, stacked=lat_s)
        lat_s = (ckv, kpe)
        knew = jnp.pad(kcat, ((0, 0), (0, -s_s % LANES), (0, 0)))
        yb = _attention_cached(q, cache_ckv, cache_kpe, l, knew, sgb, lw["w_uv"], tk=tk_s)
        ys = _outproj(ys, mod_s, ya, yb, sua, sub, lw, bb=b_s, t=s_s)
        outs_s.append((convo[:, SUBLANES - (CONV_W - 1):, :], hlast[:, 0, :]))

    stack = lambda outs, k: jnp.stack([o[k] for o in outs])
    return (yp, ys,
            lat_p[0], lat_p[1], stack(outs_p, 0), stack(outs_p, 1),
            lat_s[0], lat_s[1], stack(outs_s, 0), stack(outs_s, 1))
```

```python
import functools
import math

import jax
import jax.numpy as jnp
from jax import lax
from jax.experimental import pallas as pl
from jax.experimental.pallas import tpu as pltpu

F32 = jnp.float32
BF16 = jnp.bfloat16

EPS = 1e-6
NEG_INF = -1e30
CHUNK = 64
LRU_C = 8.0
ROPE_THETA = 10000.0

LANES = 128
SUBLANES = 8
MXU_DIM = 256
VMEM_LIMIT_BYTES = 56 * 1024 * 1024

CONV_W = 4
N_HEADS = 8
QK_NOPE = 128
QK_ROPE = 64
V_DIM = 128
KV_RANK = 256
Q_RANK = 768
LRU_BLOCKS = 8
K_CAT = KV_RANK + LANES
Q_SCALE = (QK_NOPE + QK_ROPE) ** -0.5 * math.log2(math.e)


def _sigmoid(x):
    return jax.nn.sigmoid(x)


def _silu(x):
    return x * _sigmoid(x)


def _sqrt_nonneg(x):
    return jnp.where(x > 0.0, x * lax.rsqrt(x), 0.0)


def _rms(x, g):
    return x * lax.rsqrt(jnp.mean(x * x, axis=-1, keepdims=True) + EPS) * g


def _const_spec(shape):
    nd = len(shape)
    return pl.BlockSpec(shape, lambda *_: (0,) * nd, pipeline_mode=pl.Buffered(1))


def _mod_kernel(c_ref, w_ref, b_ref, o_ref):
    c = c_ref[...]
    o_ref[0] = jnp.dot(_silu(c), w_ref[0], preferred_element_type=F32,
                       precision=lax.Precision.HIGHEST) + b_ref[0]


def _modulation(c_all, ada_w, ada_b):
    depth, d, d3 = ada_w.shape
    n = c_all.shape[0]
    nblk = d3 // d
    return pl.pallas_call(
        _mod_kernel,
        grid=(depth, nblk),
        in_specs=[pl.BlockSpec((n, d), lambda l, j: (0, 0)),
                  pl.BlockSpec((1, d, d), lambda l, j: (l, 0, j)),
                  pl.BlockSpec((1, 1, d), lambda l, j: (l, 0, j))],
        out_specs=pl.BlockSpec((1, n, d), lambda l, j: (l, 0, j)),
        out_shape=jax.ShapeDtypeStruct((depth, n, d3), F32),
        name="adaln_mod",
    )(c_all, ada_w, ada_b.reshape(depth, 1, d3))


def _inproj_kernel(x_ref, mod_ref, conv0_ref, h0_ref, cos_ref, sin_ref,
                   w_in_ref, w_q_ref, w_uk_ref, w_gate_ref,
                   pre_norm_ref, conv_w_ref, conv_b_ref, ba_ref, bx_ref, lam_ref, qn_ref, kvn_ref,
                   *rest, per_head, n_alias):
    (ya_ref, sgb_ref, sua_ref, sub_ref, q_ref, ckv_ref, kpe_ref, kcat_ref,
     convo_ref, hlast_ref, *rest) = rest[n_alias:]
    if per_head:
        vt_ref, buf_ref, hcar_ref, hb_ref, cqn_ref, v_scr = rest
    else:
        buf_ref, hcar_ref, hb_ref, cqn_ref = rest
    bb, t, d = x_ref.shape
    q_main = QK_NOPE if per_head else KV_RANK
    m = bb * t
    groups = t // SUBLANES
    gw = d // LRU_BLOCKS
    cw_ = MXU_DIM
    n_chunks = d // cw_
    blocks_per_chunk = cw_ // gw
    nq = N_HEADS * QK_NOPE
    npe = N_HEADS * QK_ROPE
    o_xa, o_ga, o_cq = 0, d, 2 * d
    o_ckv = o_cq + Q_RANK
    o_kpe = o_ckv + KV_RANK
    o_krot = o_kpe + LANES
    o_gb = o_krot + LANES
    o_ua = o_gb + d
    o_ub = o_ua + d

    @pl.when(pl.program_id(1) == 0)
    def _():
        buf_ref[:, 0:SUBLANES, :] = conv0_ref[...]
        hcar_ref[...] = h0_ref[...]

    x = x_ref[...]
    mod = mod_ref[...]
    shift = mod[:, :, 0:d]
    scale = mod[:, :, d:2 * d]
    h = _rms(x, pre_norm_ref[...]) * (1.0 + scale) + shift
    hb_ref[...] = h.reshape(m, d).astype(BF16)

    def proj(lo, width):
        return jnp.dot(hb_ref[...], w_in_ref[:, lo:lo + width], preferred_element_type=F32)

    cos = cos_ref[...][None]
    sin = sin_ref[...][None]

    def conv_phase(c):
        cs = slice(c * cw_, (c + 1) * cw_)
        xa = proj(o_xa + c * cw_, cw_)
        buf_ref[:, SUBLANES:SUBLANES + t, cs] = xa.reshape(bb, t, cw_)
        cw = conv_w_ref[:, cs]
        xc = conv_b_ref[:, cs] + buf_ref[:, SUBLANES - 3:SUBLANES - 3 + t, cs] * cw[0:1]
        xc = xc + buf_ref[:, SUBLANES - 2:SUBLANES - 2 + t, cs] * cw[1:2]
        xc = xc + buf_ref[:, SUBLANES - 1:SUBLANES - 1 + t, cs] * cw[2:3]
        xc = xc + buf_ref[:, SUBLANES:SUBLANES + t, cs] * cw[3:4]
        tail = buf_ref[:, t:t + SUBLANES, cs]
        convo_ref[:, :, cs] = tail
        buf_ref[:, 0:SUBLANES, cs] = tail
        return xc.reshape(m, cw_)

    def gate_phase(c, xc2):
        cs = slice(c * cw_, (c + 1) * cw_)
        xcb = xc2.astype(BF16)
        r_parts, i_parts = [], []
        for n in range(blocks_per_chunk):
            g = jnp.dot(xcb[:, n * gw:(n + 1) * gw], w_gate_ref[c * blocks_per_chunk + n],
                        preferred_element_type=F32)
            r_parts.append(g[:, 0:gw])
            i_parts.append(g[:, gw:2 * gw])
        r = _sigmoid(jnp.concatenate(r_parts, axis=1) + ba_ref[:, cs])
        ig = _sigmoid(jnp.concatenate(i_parts, axis=1) + bx_ref[:, cs])
        nl = -lam_ref[:, cs]
        softplus = jnp.maximum(nl, 0.0) + jnp.log1p(jnp.exp(-jnp.abs(nl)))
        th = jnp.tanh((-0.5 * LRU_C * r) * softplus)
        inv = 1.0 / (1.0 - th)
        a = (1.0 + th) * inv
        mult = (2.0 * _sqrt_nonneg(-th)) * inv
        return a, mult * (ig * xc2)

    def scan_phase(c, a, b):
        cs = slice(c * cw_, (c + 1) * cw_)
        a4 = a.reshape(bb * groups, SUBLANES, cw_)
        b4 = b.reshape(bb * groups, SUBLANES, cw_)
        row = lax.broadcasted_iota(jnp.int32, a4.shape, 1)
        for sft in (1, 2, 4):
            keep = row >= sft
            a_sh = jnp.where(keep, pltpu.roll(a4, sft, axis=1), 1.0)
            b_sh = jnp.where(keep, pltpu.roll(b4, sft, axis=1), 0.0)
            b4 = b4 + a4 * b_sh
            a4 = a4 * a_sh
        a5 = a4.reshape(bb, groups, SUBLANES, cw_)
        b5 = b4.reshape(bb, groups, SUBLANES, cw_)
        hc = hcar_ref[:, :, cs]
        outs = []
        for g in range(groups):
            hs = a5[:, g] * hc + b5[:, g]
            outs.append(hs)
            hc = hs[:, SUBLANES - 1:SUBLANES, :]
        hcar_ref[:, :, cs] = hc
        hlast_ref[:, :, cs] = hc
        return jnp.concatenate(outs, axis=1).reshape(m, cw_)

    def out_phase(c, y_lru):
        cs = slice(c * cw_, (c + 1) * cw_)
        ga = proj(o_ga + c * cw_, cw_)
        ya_ref[:, :, cs] = (y_lru * _silu(ga)).astype(BF16).reshape(bb, t, cw_)

    def gate_out(ref, fn, off, c):
        cs = slice(c * cw_, (c + 1) * cw_)
        ref[:, :, cs] = fn(proj(off + c * cw_, cw_)).astype(BF16).reshape(bb, t, cw_)

    def q_latent(heads):
        q_nope = jnp.dot(cqn_ref[...], w_q_ref[:, heads[0] * QK_NOPE:(heads[-1] + 1) * QK_NOPE],
                         preferred_element_type=F32).astype(BF16)
        for k, hd in enumerate(heads):
            q_h = q_nope[:, k * QK_NOPE:(k + 1) * QK_NOPE]
            if not per_head:
                q_h = jnp.dot(q_h, w_uk_ref[hd], preferred_element_type=F32).astype(BF16)
            q_ref[:, hd, :, 0:q_main] = q_h.reshape(bb, t, q_main)

    def q_rotary():
        q_pe = jnp.dot(cqn_ref[...], w_q_ref[:, nq:nq + npe], preferred_element_type=F32)
        lane = lax.broadcasted_iota(jnp.int32, (1, 1, LANES), 2)
        half = QK_ROPE // 2
        low_half = (lane & (QK_ROPE - 1)) < half
        for pair in range(N_HEADS // 2):
            sl = slice(pair * LANES, (pair + 1) * LANES)
            x = q_pe[:, sl].reshape(bb, t, LANES)
            rot = jnp.where(low_half, -pltpu.roll(x, LANES - half, axis=2), pltpu.roll(x, half, axis=2))
            both = x * cos + rot * sin
            rs = slice(q_main, q_main + LANES)
            q_ref[:, 2 * pair, :, rs] = jnp.where(lane < QK_ROPE, both, 0.0).astype(BF16)
            q_ref[:, 2 * pair + 1, :, rs] = jnp.where(lane < QK_ROPE, 0.0, both).astype(BF16)

    def key_latents():
        ckv = _rms(proj(o_ckv, KV_RANK), kvn_ref[...])
        ckv_ref[...] = ckv.reshape(bb, t, KV_RANK)
        ckvb = ckv.astype(BF16)
        kp = proj(o_kpe, LANES).reshape(bb, t, LANES)
        kr = proj(o_krot, LANES).reshape(bb, t, LANES)
        kroped = kp * cos + kr * sin
        kpe_ref[...] = kroped[:, :, 0:QK_ROPE]
        if per_head:
            nk = N_HEADS * QK_NOPE
            k_nope = jnp.dot(ckvb, w_uk_ref[:, 0:nk], preferred_element_type=F32).astype(BF16)
            v_scr[...] = jnp.dot(ckvb, w_uk_ref[:, nk:nk + N_HEADS * V_DIM], preferred_element_type=F32)
            v_t = v_scr[...].T
            for hd in range(N_HEADS):
                kcat_ref[:, hd, :, 0:QK_NOPE] = k_nope[:, hd * QK_NOPE:(hd + 1) * QK_NOPE].reshape(
                    bb, t, QK_NOPE)
                kcat_ref[:, hd, :, QK_NOPE:QK_NOPE + LANES] = kroped.astype(BF16)
                vt_ref[0, hd, 0] = v_t[hd * V_DIM:(hd + 1) * V_DIM].astype(BF16)
        else:
            kcat_ref[:, :, 0:KV_RANK] = ckvb.reshape(bb, t, KV_RANK)
            kcat_ref[:, :, KV_RANK:K_CAT] = kroped.astype(BF16)

    cqn_ref[...] = _rms(proj(o_cq, Q_RANK), qn_ref[...]).astype(BF16)
    other = [lambda: q_latent((0, 1, 2, 3)), lambda: q_latent((4, 5, 6, 7)), q_rotary, key_latents]
    for c in range(n_chunks):
        xc2 = conv_phase(c)
        gate_out(sgb_ref, _silu, o_gb, c)
        a, b = gate_phase(c, xc2)
        gate_out(sua_ref, _sigmoid, o_ua, c)
        y_lru = scan_phase(c, a, b)
        gate_out(sub_ref, _sigmoid, o_ub, c)
        out_phase(c, y_lru)
        if c < len(other):
            other[c]()
    for f in other[n_chunks:]:
        f()


def _inproj(x, mod, conv0, h0, cos, sin, lw, *, bb, t, per_head, layer, depth, stacked):
    b, s, d = x.shape
    grid = (b // bb, s // t)
    tok = lambda w: pl.BlockSpec((bb, t, w), lambda i, j: (i, j, 0))
    tok_l = lambda w: pl.BlockSpec((None, bb, t, w), lambda i, j: (layer, i, j, 0))
    per_b = lambda r, w: pl.BlockSpec((bb, r, w), lambda i, j: (i, 0, 0))
    heads = lambda w: pl.BlockSpec((bb, N_HEADS, t, w), lambda i, j: (i, 0, j, 0))
    q_w = (QK_NOPE if per_head else KV_RANK) + LANES
    weights = [lw["w_in"], lw["w_q"], lw["w_ukv"] if per_head else lw["w_uk"], lw["w_gate"],
               lw["pre_norm"], lw["conv_w"],
               lw["conv_b"], lw["lru_ba"], lw["lru_bx"], lw["lru_lambda"], lw["q_norm"], lw["kv_norm"]]
    in_specs = ([tok(d), per_b(1, 3 * d), per_b(SUBLANES, d), per_b(1, d),
                 pl.BlockSpec((t, LANES), lambda i, j: (j, 0)),
                 pl.BlockSpec((t, LANES), lambda i, j: (j, 0))]
                + [_const_spec(w.shape) for w in weights])
    out_shape = [jax.ShapeDtypeStruct((b, s, d), BF16)] * 4 + [
        jax.ShapeDtypeStruct((b, N_HEADS, s, q_w), BF16),
        jax.ShapeDtypeStruct((depth, b, s, KV_RANK), F32),
        jax.ShapeDtypeStruct((depth, b, s, QK_ROPE), F32),
        jax.ShapeDtypeStruct((b, N_HEADS, s, q_w) if per_head else (b, s, K_CAT), BF16),
        jax.ShapeDtypeStruct((b, SUBLANES, d), F32),
        jax.ShapeDtypeStruct((b, 1, d), F32),
    ]
    out_specs = [tok(d)] * 4 + [
        heads(q_w), tok_l(KV_RANK), tok_l(QK_ROPE), heads(q_w) if per_head else tok(K_CAT),
        per_b(SUBLANES, d), per_b(1, d),
    ]
    aliased = [] if stacked is None else list(stacked)
    n_in = len(in_specs)
    in_specs = in_specs + [pl.BlockSpec(memory_space=pl.ANY)] * len(aliased)
    aliases = {n_in + k: 5 + k for k in range(len(aliased))}
    if per_head:
        assert bb == 1
        out_shape.append(jax.ShapeDtypeStruct((b, N_HEADS, s // t, V_DIM, t), BF16))
        out_specs.append(pl.BlockSpec((1, N_HEADS, 1, V_DIM, t), lambda i, j: (i, 0, j, 0, 0)))
    return pl.pallas_call(
        functools.partial(_inproj_kernel, per_head=per_head, n_alias=len(aliased)),
        grid=grid,
        in_specs=in_specs,
        out_specs=out_specs,
        out_shape=out_shape,
        input_output_aliases=aliases,
        scratch_shapes=[pltpu.VMEM((bb, t + SUBLANES, d), F32), pltpu.VMEM((bb, 1, d), F32),
                        pltpu.VMEM((bb * t, d), BF16), pltpu.VMEM((bb * t, Q_RANK), BF16)]
        + ([pltpu.VMEM((bb * t, N_HEADS * V_DIM), F32)] if per_head else []),
        compiler_params=pltpu.CompilerParams(
            dimension_semantics=("parallel", "arbitrary"), vmem_limit_bytes=VMEM_LIMIT_BYTES),
        name="inproj_lru",
    )(x, mod, conv0, h0, cos, sin, *weights, *aliased)


def _attn_heads_kernel(q_ref, k_ref, vt_ref, sgb_ref, bias_ref, yb_ref, m_scr, l_scr, acc_scr, s_scr,
                       *, tk):
    n_groups = q_ref.shape[1]
    n_sub = q_ref.shape[2] // tk
    gran = vt_ref.shape[-1]

    m_scr[...] = jnp.full(m_scr.shape, NEG_INF, F32)
    l_scr[...] = jnp.zeros(l_scr.shape, F32)
    acc_scr[...] = jnp.zeros(acc_scr.shape, F32)

    def qk(u, j, g):
        k = k_ref[0, g, pl.ds(pl.multiple_of(j * tk, tk), tk), :]
        return lax.dot_general(k, q_ref[0, g, u * tk:(u + 1) * tk, :], (((1,), (1,)), ((), ())),
                               preferred_element_type=F32)

    def step(u, j, masked, last_tile):
        for g in range(n_groups):
            if g + 1 < n_groups:
                s_scr[g + 1] = qk(u, j, g + 1)
            elif not last_tile:
                s_scr[0] = qk(u, j + 1, 0)
            s = s_scr[g]
            if masked:
                s = s + bias_ref[...]
            m_prev = m_scr[u, g]
            m_next = jnp.maximum(m_prev, jnp.max(s, axis=0, keepdims=True))
            p = jnp.exp2(s - m_next)
            alpha = jnp.exp2(m_prev - m_next)
            l_scr[u, g] = alpha * l_scr[u, g] + jnp.sum(p, axis=0, keepdims=True)
            m_scr[u, g] = m_next
            vt = jnp.concatenate([vt_ref[0, g, j * (tk // gran) + i] for i in range(tk // gran)], axis=1)
            pv = jnp.dot(vt, p.astype(BF16), preferred_element_type=F32)
            acc_scr[u, g] = acc_scr[u, g] * alpha + pv

    s_scr[0] = qk(0, 0, 0)
    for u in range(n_sub):
        last = pl.program_id(2) * n_sub + u

        def pair(jj, c, u=u):
            step(u, 2 * jj, False, False)
            step(u, 2 * jj + 1, False, False)
            return c

        lax.fori_loop(0, last // 2, pair, 0)
        if n_sub % 2 == 0:
            if u % 2 == 1:
                step(u, last - 1, False, False)
        else:
            @pl.when(last % 2 == 1)
            def _(u=u, last=last):
                step(u, last - 1, False, False)
        step(u, last, True, True)
        if u + 1 < n_sub:
            s_scr[0] = qk(u + 1, 0, 0)
        for g in range(n_groups):
            o = (acc_scr[u, g] * (1.0 / l_scr[u, g])).T
            sl = slice(g * V_DIM, (g + 1) * V_DIM)
            yb_ref[0, u * tk:(u + 1) * tk, sl] = (o * sgb_ref[0, u * tk:(u + 1) * tk, sl]).astype(BF16)


def _attention_heads(q, k, vt, sgb, *, tq, tk, heads_per_step=4):
    b, n_heads, sq, qw = q.shape
    sk = k.shape[2]
    d = sgb.shape[-1]
    gran = vt.shape[-1]
    hps = heads_per_step
    n_sub = max(n for n in (4, 2, 1) if sq % (n * tq) == 0)
    tqs = n_sub * tq
    assert n_heads % hps == 0 and sq % tqs == 0 and sk % tk == 0 and tk % gran == 0
    assert vt.shape[2] * gran == sk and (hps * V_DIM) % LANES == 0
    assert tk == tq and tq % CHUNK == 0 and sq == sk
    k_idx = lax.broadcasted_iota(jnp.int32, (tk, tq), 0)
    q_idx = lax.broadcasted_iota(jnp.int32, (tk, tq), 1)
    diag_bias = jnp.where(k_idx // CHUNK <= q_idx // CHUNK, 0.0, NEG_INF).astype(F32)
    kern = functools.partial(_attn_heads_kernel, tk=tk)
    return pl.pallas_call(
        kern,
        grid=(b, n_heads // hps, sq // tqs),
        in_specs=[pl.BlockSpec((1, hps, tqs, qw), lambda i, h, j: (i, h, j, 0)),
                  pl.BlockSpec((1, hps, sk, qw), lambda i, h, j: (i, h, 0, 0)),
                  pl.BlockSpec((1, hps) + vt.shape[2:], lambda i, h, j: (i, h, 0, 0, 0)),
                  pl.BlockSpec((1, tqs, hps * V_DIM), lambda i, h, j: (i, j, h)),
                  _const_spec((tk, tq))],
        out_specs=pl.BlockSpec((1, tqs, hps * V_DIM), lambda i, h, j: (i, j, h)),
        out_shape=jax.ShapeDtypeStruct((b, sq, d), BF16),
        scratch_shapes=[pltpu.VMEM((n_sub, hps, 1, tq), F32), pltpu.VMEM((n_sub, hps, 1, tq), F32),
                        pltpu.VMEM((n_sub, hps, V_DIM, tq), F32), pltpu.VMEM((hps, tk, tq), F32)],
        compiler_params=pltpu.CompilerParams(
            dimension_semantics=("parallel", "parallel", "arbitrary"),
            vmem_limit_bytes=VMEM_LIMIT_BYTES),
        name="mla_attention_heads",
    )(q, k, vt, sgb, diag_bias)


def _attn_cached_kernel(q_ref, ckv_ref, kpe_ref, knew_ref, sgb_ref, w_uv_ref, yb_ref,
                        m_scr, l_scr, acc_scr, *, past_len, sq, tk):
    m = N_HEADS * sq
    q = q_ref[0].reshape(m, K_CAT)
    q_lat = q[:, 0:KV_RANK]
    q_rope = q[:, KV_RANK:K_CAT].astype(F32)
    q_rope = (q_rope[:, 0:QK_ROPE] + q_rope[:, QK_ROPE:LANES]).astype(BF16)
    q_pos = past_len + (lax.broadcasted_iota(jnp.int32, (m, 1), 0) & (sq - 1))
    k_lim = (q_pos // CHUNK + 1) * CHUNK

    m_scr[...] = jnp.full(m_scr.shape, NEG_INF, F32)
    l_scr[...] = jnp.zeros(l_scr.shape, F32)
    acc_scr[...] = jnp.zeros(acc_scr.shape, F32)
    nt = (((1,), (1,)), ((), ()))

    def update(s, visible, v):
        s = jnp.where(visible, s, NEG_INF)
        m_prev = m_scr[...]
        m_next = jnp.maximum(m_prev, jnp.max(s, axis=-1, keepdims=True))
        p = jnp.exp2(s - jnp.concatenate([m_next] * (s.shape[1] // LANES), axis=1))
        alpha = jnp.exp2(m_prev - m_next)
        l_scr[...] = alpha * l_scr[...] + jnp.sum(p, axis=-1, keepdims=True)
        m_scr[...] = m_next
        pv = jnp.dot(p.astype(BF16), v, preferred_element_type=F32)
        acc_scr[...] = acc_scr[...] * jnp.concatenate([alpha] * (KV_RANK // LANES), axis=1) + pv

    def past_body(j, c):
        off = pl.multiple_of(j * tk, tk)
        lat = ckv_ref[0, 0, pl.ds(off, tk), :].astype(BF16)
        pe = kpe_ref[0, 0, pl.ds(off, tk), :].astype(BF16)
        s = (lax.dot_general(q_lat, lat, nt, preferred_element_type=F32)
             + lax.dot_general(q_rope, pe, nt, preferred_element_type=F32))
        k_pos = off + lax.broadcasted_iota(jnp.int32, (m, tk), 1)
        update(s, k_pos < k_lim, lat)
        return c

    lax.fori_loop(0, past_len // tk, past_body, 0)

    kn = knew_ref[0]
    s = lax.dot_general(q, kn, nt, preferred_element_type=F32)
    idx = lax.broadcasted_iota(jnp.int32, (m, kn.shape[0]), 1)
    update(s, jnp.where(idx < sq, past_len + idx, k_lim) < k_lim, kn[:, 0:KV_RANK])

    inv_l = 1.0 / l_scr[...]
    o = (acc_scr[...] * jnp.concatenate([inv_l] * (KV_RANK // LANES), axis=1)).astype(BF16)
    sgb = sgb_ref[0]
    for hd in range(N_HEADS):
        att = jnp.dot(o[hd * sq:(hd + 1) * sq], w_uv_ref[hd], preferred_element_type=F32)
        sl = slice(hd * V_DIM, (hd + 1) * V_DIM)
        yb_ref[0, :, sl] = (att * sgb[:, sl]).astype(BF16)


def _attention_cached(q, cache_ckv, cache_kpe, layer, knew, sgb, w_uv, *, tk):
    b, _, sq, _ = q.shape
    past_len = cache_ckv.shape[2]
    d = sgb.shape[-1]
    m = N_HEADS * sq
    assert sq & (sq - 1) == 0 and past_len % tk == 0 and tk % LANES == 0 and knew.shape[1] % LANES == 0
    kern = functools.partial(_attn_cached_kernel, past_len=past_len, sq=sq, tk=tk)
    return pl.pallas_call(
        kern,
        grid=(b,),
        in_specs=[pl.BlockSpec((1, N_HEADS, sq, K_CAT), lambda i: (i, 0, 0, 0)),
                  pl.BlockSpec((1, 1, past_len, KV_RANK), lambda i: (layer, i, 0, 0)),
                  pl.BlockSpec((1, 1, past_len, QK_ROPE), lambda i: (layer, i, 0, 0)),
                  pl.BlockSpec((1,) + knew.shape[1:], lambda i: (i, 0, 0)),
                  pl.BlockSpec((1, sq, d), lambda i: (i, 0, 0)),
                  _const_spec(w_uv.shape)],
        out_specs=pl.BlockSpec((1, sq, d), lambda i: (i, 0, 0)),
        out_shape=jax.ShapeDtypeStruct((b, sq, d), BF16),
        scratch_shapes=[pltpu.VMEM((m, LANES), F32), pltpu.VMEM((m, LANES), F32),
                        pltpu.VMEM((m, KV_RANK), F32)],
        compiler_params=pltpu.CompilerParams(
            dimension_semantics=("parallel",), vmem_limit_bytes=VMEM_LIMIT_BYTES),
        name="mla_attention_cached",
    )(q, cache_ckv, cache_kpe, knew, sgb, w_uv)


def _outproj_kernel(x_ref, mod_ref, ya_ref, yb_ref, sua_ref, sub_ref,
                    w_a_ref, w_b_ref, w_o_ref, post_norm_ref, y_ref):
    bb, t, d = x_ref.shape
    m = bb * t
    ya = ya_ref[...].reshape(m, d)
    yb = yb_ref[...].reshape(m, d)
    pa = jnp.dot(ya, w_a_ref[...], preferred_element_type=F32)
    pb = jnp.dot(yb, w_b_ref[...], preferred_element_type=F32)
    merged = sua_ref[...].reshape(m, d).astype(F32) * pa + sub_ref[...].reshape(m, d).astype(F32) * pb
    o = jnp.dot(merged.astype(BF16), w_o_ref[...], preferred_element_type=F32)
    gate = mod_ref[...][:, :, 2 * d:3 * d]
    y_ref[...] = x_ref[...] + gate * _rms(o, post_norm_ref[...]).reshape(bb, t, d)


def _outproj(x, mod, ya, yb, sua, sub, lw, *, bb, t):
    b, s, d = x.shape
    tok = pl.BlockSpec((bb, t, d), lambda i, j: (i, j, 0))
    weights = [lw["w_branch_a"], lw["w_branch_b"], lw["w_out"], lw["post_norm"]]
    return pl.pallas_call(
        _outproj_kernel,
        grid=(b // bb, s // t),
        in_specs=[tok, pl.BlockSpec((bb, 1, 3 * d), lambda i, j: (i, 0, 0)), tok, tok, tok, tok]
        + [_const_spec(w.shape) for w in weights],
        out_specs=tok,
        out_shape=jax.ShapeDtypeStruct((b, s, d), F32),
        compiler_params=pltpu.CompilerParams(
            dimension_semantics=("parallel", "parallel"), vmem_limit_bytes=VMEM_LIMIT_BYTES),
        name="outproj",
    )(x, mod, ya, yb, sua, sub, *weights)


def _rot_half(w):
    half = w.shape[-1] // 2
    return jnp.concatenate([-w[..., half:], w[..., :half]], axis=-1)


def _twice(w):
    return jnp.concatenate([w, w], axis=-1)


def _layer_weights(l, p):
    d = p["w_in"].shape[1]
    w_in = p["w_in"][l].astype(BF16)
    o = 0
    seg = {}
    for name, width in (("xa", d), ("ga", d), ("cq", Q_RANK), ("ckv", KV_RANK), ("kpe", QK_ROPE),
                        ("gb", d), ("ua", d), ("ub", d)):
        seg[name] = w_in[:, o:o + width]
        o += width
    w_in_ext = jnp.concatenate(
        [seg["xa"], seg["ga"], seg["cq"], seg["ckv"], _twice(seg["kpe"]),
         _twice(_rot_half(seg["kpe"])), seg["gb"], seg["ua"], seg["ub"]], axis=1)
    wq = p["w_q_up"][l].reshape(Q_RANK, N_HEADS, QK_NOPE + QK_ROPE)
    wq_pe = wq[:, :, QK_NOPE:]
    w_q = jnp.concatenate(
        [wq[:, :, :QK_NOPE].reshape(Q_RANK, -1), wq_pe.reshape(Q_RANK, -1)], axis=1)
    row = lambda v: v.reshape(1, -1)
    return {
        "w_in": w_in_ext,
        "w_q": w_q.astype(BF16),
        "w_uk": jnp.transpose(p["w_uk"][l], (1, 2, 0)).astype(BF16),
        "w_ukv": jnp.concatenate([p["w_uk"][l].reshape(KV_RANK, -1), p["w_uv"][l].reshape(KV_RANK, -1)],
                                 axis=1).astype(BF16),
        "w_uv": jnp.transpose(p["w_uv"][l], (1, 0, 2)).astype(BF16),
        "w_gate": jnp.concatenate([p["lru_wa"][l], p["lru_wx"][l]], axis=-1).astype(BF16),
        "pre_norm": row(p["pre_norm"][l]), "post_norm": row(p["post_norm"][l]),
        "conv_w": p["conv_w"][l], "conv_b": row(p["conv_b"][l]),
        "lru_ba": row(p["lru_ba"][l]), "lru_bx": row(p["lru_bx"][l]),
        "lru_lambda": row(p["lru_lambda"][l]),
        "q_norm": row(p["q_norm"][l]) * Q_SCALE, "kv_norm": row(p["kv_norm"][l]),
        "w_branch_a": p["w_branch_a"][l].astype(BF16), "w_branch_b": p["w_branch_b"][l].astype(BF16),
        "w_out": p["w_out"][l].astype(BF16),
    }


def _rope_tables(pos):
    half = QK_ROPE // 2
    inv = ROPE_THETA ** (-jnp.arange(half, dtype=F32) / half)
    ang = pos.astype(F32)[:, None] * inv[None, :]
    cos, sin = jnp.cos(ang), jnp.sin(ang)
    return jnp.concatenate([cos] * (LANES // half), axis=1), jnp.concatenate([sin] * (LANES // half), axis=1)


def _pick_tile(n, target):
    t = min(n, target)
    while n % t:
        t //= 2
    return t


def kernel(x_prompt, x_sample, c_prompt, c_sample, cache_ckv, cache_kpe, state_conv, state_lru, ada_w, ada_b, pre_norm, post_norm, w_in, conv_w, conv_b, lru_wa, lru_ba, lru_wx, lru_bx, lru_lambda, q_norm, w_q_up, kv_norm, w_uk, w_uv, w_branch_a, w_branch_b, w_out):
    p = dict(ada_w=ada_w, ada_b=ada_b, pre_norm=pre_norm, post_norm=post_norm, w_in=w_in,
             conv_w=conv_w, conv_b=conv_b, lru_wa=lru_wa, lru_ba=lru_ba, lru_wx=lru_wx, lru_bx=lru_bx,
             lru_lambda=lru_lambda, q_norm=q_norm, w_q_up=w_q_up, kv_norm=kv_norm, w_uk=w_uk,
             w_uv=w_uv, w_branch_a=w_branch_a, w_branch_b=w_branch_b, w_out=w_out)
    depth = w_in.shape[0]
    b_p, s_p, d = x_prompt.shape
    b_s, s_s, _ = x_sample.shape
    past_len = cache_ckv.shape[2]
    assert s_p % SUBLANES == 0 and s_s % SUBLANES == 0 and s_p >= SUBLANES and s_s >= SUBLANES

    mods = _modulation(jnp.concatenate([c_prompt, c_sample], axis=0), ada_w, ada_b)
    cos_p, sin_p = _rope_tables(jnp.arange(s_p, dtype=jnp.int32))
    cos_s, sin_s = _rope_tables(past_len + jnp.arange(s_s, dtype=jnp.int32))
    conv0_p = jnp.zeros((b_p, SUBLANES, d), F32)
    h0_p = jnp.zeros((b_p, 1, d), F32)

    t_in_p = _pick_tile(s_p, 256)
    t_out_p = _pick_tile(s_p, 1024)
    tq_p = _pick_tile(s_p, 512)
    tk_p = _pick_tile(s_p, 512)
    tk_s = _pick_tile(past_len, 1024)

    yp, ys = x_prompt, x_sample
    outs_p, outs_s = [], []
    lat_p = lat_s = None
    for l in range(depth):
        lw = _layer_weights(l, p)

        mod_p = mods[l, :b_p, None, :]
        ya, sgb, sua, sub, q, ckv, kpe, k_heads, convo, hlast, vt = _inproj(
            yp, mod_p, conv0_p, h0_p, cos_p, sin_p, lw, bb=1, t=t_in_p, per_head=True,
            layer=l, depth=depth, stacked=lat_p)
        lat_p = (ckv, kpe)
        yb = _attention_heads(q, k_heads, vt, sgb, tq=tq_p, tk=tk_p)
        yp = _outproj(yp, mod_p, ya, yb, sua, sub, lw, bb=1, t=t_out_p)
        outs_p.append((convo[:, SUBLANES - (CONV_W - 1):, :], hlast[:, 0, :]))

        mod_s = mods[l, b_p:, None, :]
        conv0_s = jnp.pad(state_conv[l], ((0, 0), (SUBLANES - (CONV_W - 1), 0), (0, 0)))
        ya, sgb, sua, sub, q, ckv, kpe, kcat, convo, hlast = _inproj(
            ys, mod_s, conv0_s, state_lru[l][:, None, :], cos_s, sin_s, lw, bb=b_s, t=s_s,
            per_head=False, layer=l, depth=depth, stacked=lat_s)
        lat_s = (ckv, kpe)
        knew = jnp.pad(kcat, ((0, 0), (0, -s_s % LANES), (0, 0)))
        yb = _attention_cached(q, cache_ckv, cache_kpe, l, knew, sgb, lw["w_uv"], tk=tk_s)
        ys = _outproj(ys, mod_s, ya, yb, sua, sub, lw, bb=b_s, t=s_s)
        outs_s.append((convo[:, SUBLANES - (CONV_W - 1):, :], hlast[:, 0, :]))

    stack = lambda outs, k: jnp.stack([o[k] for o in outs])
    return (yp, ys,
            lat_p[0], lat_p[1], stack(outs_p, 0), stack(outs_p, 1),
            lat_s[0], lat_s[1], stack(outs_s, 0), stack(outs_s, 1))
```

```python
import functools
import math

import jax
import jax.numpy as jnp
from jax import lax
from jax.experimental import pallas as pl
from jax.experimental.pallas import tpu as pltpu

F32 = jnp.float32
BF16 = jnp.bfloat16

EPS = 1e-6
NEG_INF = -1e30
CHUNK = 64
LRU_C = 8.0
ROPE_THETA = 10000.0

LANES = 128
SUBLANES = 8
MXU_DIM = 256
VMEM_LIMIT_BYTES = 56 * 1024 * 1024

T_INPROJ = MXU_DIM
T_OUTPROJ = 4 * MXU_DIM
T_ATTN = 2 * MXU_DIM
T_CACHED = 4 * MXU_DIM

CONV_W = 4
N_HEADS = 8
QK_NOPE = 128
QK_ROPE = 64
V_DIM = 128
KV_RANK = 256
Q_RANK = 768
LRU_BLOCKS = 8
K_CAT = KV_RANK + LANES
Q_SCALE = (QK_NOPE + QK_ROPE) ** -0.5 * math.log2(math.e)


def _sigmoid(x):
    return jax.nn.sigmoid(x)


def _silu(x):
    return x * _sigmoid(x)


def _sqrt_nonneg(x):
    return jnp.where(x > 0.0, x * lax.rsqrt(x), 0.0)


def _rms(x, g):
    return x * lax.rsqrt(jnp.mean(x * x, axis=-1, keepdims=True) + EPS) * g


def _const_spec(shape):
    nd = len(shape)
    return pl.BlockSpec(shape, lambda *_: (0,) * nd, pipeline_mode=pl.Buffered(1))


def _mod_kernel(c_ref, w_ref, b_ref, o_ref):
    c = c_ref[...]
    o_ref[0] = jnp.dot(_silu(c), w_ref[0], preferred_element_type=F32,
                       precision=lax.Precision.HIGHEST) + b_ref[0]


def _modulation(c_all, ada_w, ada_b):
    depth, d, d3 = ada_w.shape
    n = c_all.shape[0]
    nblk = d3 // d
    return pl.pallas_call(
        _mod_kernel,
        grid=(depth, nblk),
        in_specs=[pl.BlockSpec((n, d), lambda l, j: (0, 0)),
                  pl.BlockSpec((1, d, d), lambda l, j: (l, 0, j)),
                  pl.BlockSpec((1, 1, d), lambda l, j: (l, 0, j))],
        out_specs=pl.BlockSpec((1, n, d), lambda l, j: (l, 0, j)),
        out_shape=jax.ShapeDtypeStruct((depth, n, d3), F32),
        name="adaln_mod",
    )(c_all, ada_w, ada_b.reshape(depth, 1, d3))


def _inproj_kernel(x_ref, mod_ref, conv0_ref, h0_ref, cos_ref, sin_ref,
                   w_in_ref, w_q_ref, w_uk_ref, w_gate_ref,
                   pre_norm_ref, conv_w_ref, conv_b_ref, ba_ref, bx_ref, lam_ref, qn_ref, kvn_ref,
                   *rest, per_head, n_alias):
    (ya_ref, sgb_ref, sua_ref, sub_ref, q_ref, ckv_ref, kpe_ref, kcat_ref,
     convo_ref, hlast_ref, *rest) = rest[n_alias:]
    if per_head:
        vt_ref, buf_ref, hcar_ref, hb_ref, cqn_ref, v_scr = rest
    else:
        buf_ref, hcar_ref, hb_ref, cqn_ref = rest
    bb, t, d = x_ref.shape
    q_main = QK_NOPE if per_head else KV_RANK
    m = bb * t
    groups = t // SUBLANES
    gw = d // LRU_BLOCKS
    cw_ = MXU_DIM
    n_chunks = d // cw_
    blocks_per_chunk = cw_ // gw
    nq = N_HEADS * QK_NOPE
    npe = N_HEADS * QK_ROPE
    o_xa, o_ga, o_cq = 0, d, 2 * d
    o_ckv = o_cq + Q_RANK
    o_kpe = o_ckv + KV_RANK
    o_krot = o_kpe + LANES
    o_gb = o_krot + LANES
    o_ua = o_gb + d
    o_ub = o_ua + d

    @pl.when(pl.program_id(1) == 0)
    def _():
        buf_ref[:, 0:SUBLANES, :] = conv0_ref[...]
        hcar_ref[...] = h0_ref[...]

    x = x_ref[...]
    mod = mod_ref[...]
    shift = mod[:, :, 0:d]
    scale = mod[:, :, d:2 * d]
    h = _rms(x, pre_norm_ref[...]) * (1.0 + scale) + shift
    hb_ref[...] = h.reshape(m, d).astype(BF16)

    def proj(lo, width):
        return jnp.dot(hb_ref[...], w_in_ref[:, lo:lo + width], preferred_element_type=F32)

    cos = cos_ref[...][None]
    sin = sin_ref[...][None]

    def conv_phase(c):
        cs = slice(c * cw_, (c + 1) * cw_)
        xa = proj(o_xa + c * cw_, cw_)
        buf_ref[:, SUBLANES:SUBLANES + t, cs] = xa.reshape(bb, t, cw_)
        cw = conv_w_ref[:, cs]
        xc = conv_b_ref[:, cs] + buf_ref[:, SUBLANES - 3:SUBLANES - 3 + t, cs] * cw[0:1]
        xc = xc + buf_ref[:, SUBLANES - 2:SUBLANES - 2 + t, cs] * cw[1:2]
        xc = xc + buf_ref[:, SUBLANES - 1:SUBLANES - 1 + t, cs] * cw[2:3]
        xc = xc + buf_ref[:, SUBLANES:SUBLANES + t, cs] * cw[3:4]
        tail = buf_ref[:, t:t + SUBLANES, cs]
        convo_ref[:, :, cs] = tail
        buf_ref[:, 0:SUBLANES, cs] = tail
        return xc.reshape(m, cw_)

    def gate_phase(c, xc2):
        cs = slice(c * cw_, (c + 1) * cw_)
        xcb = xc2.astype(BF16)
        r_parts, i_parts = [], []
        for n in range(blocks_per_chunk):
            g = jnp.dot(xcb[:, n * gw:(n + 1) * gw], w_gate_ref[c * blocks_per_chunk + n],
                        preferred_element_type=F32)
            r_parts.append(g[:, 0:gw])
            i_parts.append(g[:, gw:2 * gw])
        r = _sigmoid(jnp.concatenate(r_parts, axis=1) + ba_ref[:, cs])
        ig = _sigmoid(jnp.concatenate(i_parts, axis=1) + bx_ref[:, cs])
        nl = -lam_ref[:, cs]
        softplus = jnp.maximum(nl, 0.0) + jnp.log1p(jnp.exp(-jnp.abs(nl)))
        th = jnp.tanh((-0.5 * LRU_C * r) * softplus)
        inv = 1.0 / (1.0 - th)
        a = (1.0 + th) * inv
        mult = (2.0 * _sqrt_nonneg(-th)) * inv
        return a, mult * (ig * xc2)

    def scan_phase(c, a, b):
        cs = slice(c * cw_, (c + 1) * cw_)
        a4 = a.reshape(bb * groups, SUBLANES, cw_)
        b4 = b.reshape(bb * groups, SUBLANES, cw_)
        row = lax.broadcasted_iota(jnp.int32, a4.shape, 1)
        for sft in (1, 2, 4):
            keep = row >= sft
            a_sh = jnp.where(keep, pltpu.roll(a4, sft, axis=1), 1.0)
            b_sh = jnp.where(keep, pltpu.roll(b4, sft, axis=1), 0.0)
            b4 = b4 + a4 * b_sh
            a4 = a4 * a_sh
        a5 = a4.reshape(bb, groups, SUBLANES, cw_)
        b5 = b4.reshape(bb, groups, SUBLANES, cw_)
        hc = hcar_ref[:, :, cs]
        outs = []
        for g in range(groups):
            hs = a5[:, g] * hc + b5[:, g]
            outs.append(hs)
            hc = hs[:, SUBLANES - 1:SUBLANES, :]
        hcar_ref[:, :, cs] = hc
        hlast_ref[:, :, cs] = hc
        return jnp.concatenate(outs, axis=1).reshape(m, cw_)

    def out_phase(c, y_lru):
        cs = slice(c * cw_, (c + 1) * cw_)
        ga = proj(o_ga + c * cw_, cw_)
        ya_ref[:, :, cs] = (y_lru * _silu(ga)).astype(BF16).reshape(bb, t, cw_)

    def gate_out(ref, fn, off, c):
        cs = slice(c * cw_, (c + 1) * cw_)
        ref[:, :, cs] = fn(proj(off + c * cw_, cw_)).astype(BF16).reshape(bb, t, cw_)

    def q_latent(heads):
        q_nope = jnp.dot(cqn_ref[...], w_q_ref[:, heads[0] * QK_NOPE:(heads[-1] + 1) * QK_NOPE],
                         preferred_element_type=F32)
        if per_head:
            width = len(heads) * QK_NOPE
            v_scr[:, 0:width] = q_nope
            for k, hd in enumerate(heads):
                q_ref[0, hd, 0, 0:QK_NOPE, :] = v_scr[:, k * QK_NOPE:(k + 1) * QK_NOPE].T.astype(BF16)
            return
        q_nope = q_nope.astype(BF16)
        for k, hd in enumerate(heads):
            q_h = jnp.dot(q_nope[:, k * QK_NOPE:(k + 1) * QK_NOPE], w_uk_ref[hd],
                          preferred_element_type=F32).astype(BF16)
            q_ref[:, hd, :, 0:q_main] = q_h.reshape(bb, t, q_main)

    def q_rotary():
        q_pe = jnp.dot(cqn_ref[...], w_q_ref[:, nq:nq + npe], preferred_element_type=F32)
        lane = lax.broadcasted_iota(jnp.int32, (1, 1, LANES), 2)
        half = QK_ROPE // 2
        low_half = (lane & (QK_ROPE - 1)) < half
        for pair in range(N_HEADS // 2):
            sl = slice(pair * LANES, (pair + 1) * LANES)
            x = q_pe[:, sl].reshape(bb, t, LANES)
            rot = jnp.where(low_half, -pltpu.roll(x, LANES - half, axis=2), pltpu.roll(x, half, axis=2))
            both = x * cos + rot * sin
            even = jnp.where(lane < QK_ROPE, both, 0.0)
            odd = jnp.where(lane < QK_ROPE, 0.0, both)
            rs = slice(q_main, q_main + LANES)
            if per_head:
                q_ref[0, 2 * pair, 0, rs, :] = even.reshape(m, LANES).T.astype(BF16)
                q_ref[0, 2 * pair + 1, 0, rs, :] = odd.reshape(m, LANES).T.astype(BF16)
            else:
                q_ref[:, 2 * pair, :, rs] = even.astype(BF16)
                q_ref[:, 2 * pair + 1, :, rs] = odd.astype(BF16)

    def key_latents():
        ckv = _rms(proj(o_ckv, KV_RANK), kvn_ref[...])
        ckv_ref[...] = ckv.reshape(bb, t, KV_RANK)
        ckvb = ckv.astype(BF16)
        kp = proj(o_kpe, LANES).reshape(bb, t, LANES)
        kr = proj(o_krot, LANES).reshape(bb, t, LANES)
        kroped = kp * cos + kr * sin
        kpe_ref[...] = kroped[:, :, 0:QK_ROPE]
        if per_head:
            nk = N_HEADS * QK_NOPE
            k_nope = jnp.dot(ckvb, w_uk_ref[:, 0:nk], preferred_element_type=F32).astype(BF16)
            v_scr[...] = jnp.dot(ckvb, w_uk_ref[:, nk:nk + N_HEADS * V_DIM], preferred_element_type=F32)
            v_t = v_scr[...].T
            for hd in range(N_HEADS):
                kcat_ref[:, hd, :, 0:QK_NOPE] = k_nope[:, hd * QK_NOPE:(hd + 1) * QK_NOPE].reshape(
                    bb, t, QK_NOPE)
                kcat_ref[:, hd, :, QK_NOPE:QK_NOPE + LANES] = kroped.astype(BF16)
                vt_ref[0, hd, 0] = v_t[hd * V_DIM:(hd + 1) * V_DIM].astype(BF16)
        else:
            kcat_ref[:, :, 0:KV_RANK] = ckvb.reshape(bb, t, KV_RANK)
            kcat_ref[:, :, KV_RANK:K_CAT] = kroped.astype(BF16)

    cqn_ref[...] = _rms(proj(o_cq, Q_RANK), qn_ref[...]).astype(BF16)
    other = [lambda: q_latent((0, 1, 2, 3)), lambda: q_latent((4, 5, 6, 7)), q_rotary, key_latents]
    for c in range(n_chunks):
        xc2 = conv_phase(c)
        gate_out(sgb_ref, _silu, o_gb, c)
        a, b = gate_phase(c, xc2)
        gate_out(sua_ref, _sigmoid, o_ua, c)
        y_lru = scan_phase(c, a, b)
        gate_out(sub_ref, _sigmoid, o_ub, c)
        out_phase(c, y_lru)
        if c < len(other):
            other[c]()
    for f in other[n_chunks:]:
        f()


def _inproj(x, mod, conv0, h0, cos, sin, lw, *, bb, t, per_head, layer, depth, stacked):
    b, s, d = x.shape
    grid = (b // bb, s // t)
    tok = lambda w: pl.BlockSpec((bb, t, w), lambda i, j: (i, j, 0))
    tok_l = lambda w: pl.BlockSpec((None, bb, t, w), lambda i, j: (layer, i, j, 0))
    per_b = lambda r, w: pl.BlockSpec((bb, r, w), lambda i, j: (i, 0, 0))
    heads = lambda w: pl.BlockSpec((bb, N_HEADS, t, w), lambda i, j: (i, 0, j, 0))
    q_w = (QK_NOPE if per_head else KV_RANK) + LANES
    weights = [lw["w_in"], lw["w_q"], lw["w_ukv"] if per_head else lw["w_uk"], lw["w_gate"],
               lw["pre_norm"], lw["conv_w"],
               lw["conv_b"], lw["lru_ba"], lw["lru_bx"], lw["lru_lambda"], lw["q_norm"], lw["kv_norm"]]
    in_specs = ([tok(d), per_b(1, 3 * d), per_b(SUBLANES, d), per_b(1, d),
                 pl.BlockSpec((t, LANES), lambda i, j: (j, 0)),
                 pl.BlockSpec((t, LANES), lambda i, j: (j, 0))]
                + [_const_spec(w.shape) for w in weights])
    q_shape = (b, N_HEADS, s // t, q_w, t) if per_head else (b, N_HEADS, s, q_w)
    q_spec = (pl.BlockSpec((1, N_HEADS, 1, q_w, t), lambda i, j: (i, 0, j, 0, 0)) if per_head
              else heads(q_w))
    out_shape = [jax.ShapeDtypeStruct((b, s, d), BF16)] * 4 + [
        jax.ShapeDtypeStruct(q_shape, BF16),
        jax.ShapeDtypeStruct((depth, b, s, KV_RANK), F32),
        jax.ShapeDtypeStruct((depth, b, s, QK_ROPE), F32),
        jax.ShapeDtypeStruct((b, N_HEADS, s, q_w) if per_head else (b, s, K_CAT), BF16),
        jax.ShapeDtypeStruct((b, SUBLANES, d), F32),
        jax.ShapeDtypeStruct((b, 1, d), F32),
    ]
    out_specs = [tok(d)] * 4 + [
        q_spec, tok_l(KV_RANK), tok_l(QK_ROPE), heads(q_w) if per_head else tok(K_CAT),
        per_b(SUBLANES, d), per_b(1, d),
    ]
    aliased = [] if stacked is None else list(stacked)
    n_in = len(in_specs)
    in_specs = in_specs + [pl.BlockSpec(memory_space=pl.ANY)] * len(aliased)
    aliases = {n_in + k: 5 + k for k in range(len(aliased))}
    if per_head:
        assert bb == 1
        out_shape.append(jax.ShapeDtypeStruct((b, N_HEADS, s // t, V_DIM, t), BF16))
        out_specs.append(pl.BlockSpec((1, N_HEADS, 1, V_DIM, t), lambda i, j: (i, 0, j, 0, 0)))
    return pl.pallas_call(
        functools.partial(_inproj_kernel, per_head=per_head, n_alias=len(aliased)),
        grid=grid,
        in_specs=in_specs,
        out_specs=out_specs,
        out_shape=out_shape,
        input_output_aliases=aliases,
        scratch_shapes=[pltpu.VMEM((bb, t + SUBLANES, d), F32), pltpu.VMEM((bb, 1, d), F32),
                        pltpu.VMEM((bb * t, d), BF16), pltpu.VMEM((bb * t, Q_RANK), BF16)]
        + ([pltpu.VMEM((bb * t, N_HEADS * V_DIM), F32)] if per_head else []),
        compiler_params=pltpu.CompilerParams(
            dimension_semantics=("parallel", "arbitrary"), vmem_limit_bytes=VMEM_LIMIT_BYTES),
        name="inproj_lru",
    )(x, mod, conv0, h0, cos, sin, *weights, *aliased)


def _attn_heads_kernel(q_ref, k_ref, vt_ref, sgb_ref, bias_ref, yb_ref, m_scr, l_scr, acc_scr, s_scr,
                       *, tk):
    n_groups = q_ref.shape[1]
    q_gran = q_ref.shape[-1]
    n_sub = q_ref.shape[2] * q_gran // tk
    gran = vt_ref.shape[-1]

    m_scr[...] = jnp.full(m_scr.shape, NEG_INF, F32)
    l_scr[...] = jnp.zeros(l_scr.shape, F32)
    acc_scr[...] = jnp.zeros(acc_scr.shape, F32)

    def qk(u, j, g):
        k = k_ref[0, g, pl.ds(pl.multiple_of(j * tk, tk), tk), :]
        q_t = jnp.concatenate([q_ref[0, g, u * (tk // q_gran) + i] for i in range(tk // q_gran)], axis=1)
        return jnp.dot(k, q_t, preferred_element_type=F32)

    def step(u, j, masked, last_tile):
        for g in range(n_groups):
            if g + 1 < n_groups:
                s_scr[g + 1] = qk(u, j, g + 1)
            elif not last_tile:
                s_scr[0] = qk(u, j + 1, 0)
            s = s_scr[g]
            if masked:
                s = s + bias_ref[...]
            m_prev = m_scr[u, g]
            m_next = jnp.maximum(m_prev, jnp.max(s, axis=0, keepdims=True))
            p = jnp.exp2(s - m_next)
            alpha = jnp.exp2(m_prev - m_next)
            l_scr[u, g] = alpha * l_scr[u, g] + jnp.sum(p, axis=0, keepdims=True)
            m_scr[u, g] = m_next
            vt = jnp.concatenate([vt_ref[0, g, j * (tk // gran) + i] for i in range(tk // gran)], axis=1)
            pv = jnp.dot(vt, p.astype(BF16), preferred_element_type=F32)
            acc_scr[u, g] = acc_scr[u, g] * alpha + pv

    s_scr[0] = qk(0, 0, 0)
    for u in range(n_sub):
        last = pl.program_id(2) * n_sub + u

        def pair(jj, c, u=u):
            step(u, 2 * jj, False, False)
            step(u, 2 * jj + 1, False, False)
            return c

        lax.fori_loop(0, last // 2, pair, 0)
        if n_sub % 2 == 0:
            if u % 2 == 1:
                step(u, last - 1, False, False)
        else:
            @pl.when(last % 2 == 1)
            def _(u=u, last=last):
                step(u, last - 1, False, False)
        step(u, last, True, True)
        if u + 1 < n_sub:
            s_scr[0] = qk(u + 1, 0, 0)
        for g in range(n_groups):
            o = (acc_scr[u, g] * (1.0 / l_scr[u, g])).T
            sl = slice(g * V_DIM, (g + 1) * V_DIM)
            yb_ref[0, u * tk:(u + 1) * tk, sl] = (o * sgb_ref[0, u * tk:(u + 1) * tk, sl]).astype(BF16)


def _attention_heads(q, k, vt, sgb, *, tq, tk, heads_per_step=4):
    b, n_heads, q_grans, qw, q_gran = q.shape
    sq = q_grans * q_gran
    sk = k.shape[2]
    d = sgb.shape[-1]
    gran = vt.shape[-1]
    hps = heads_per_step
    n_sub = max(n for n in (4, 2, 1) if sq % (n * tq) == 0)
    tqs = n_sub * tq
    assert n_heads % hps == 0 and sq % tqs == 0 and sk % tk == 0 and tk % gran == 0 and tq % q_gran == 0
    assert vt.shape[2] * gran == sk and (hps * V_DIM) % LANES == 0
    assert tk == tq and tq % CHUNK == 0 and sq == sk
    k_idx = lax.broadcasted_iota(jnp.int32, (tk, tq), 0)
    q_idx = lax.broadcasted_iota(jnp.int32, (tk, tq), 1)
    diag_bias = jnp.where(k_idx // CHUNK <= q_idx // CHUNK, 0.0, NEG_INF).astype(F32)
    kern = functools.partial(_attn_heads_kernel, tk=tk)
    return pl.pallas_call(
        kern,
        grid=(b, n_heads // hps, sq // tqs),
        in_specs=[pl.BlockSpec((1, hps, tqs // q_gran, qw, q_gran), lambda i, h, j: (i, h, j, 0, 0)),
                  pl.BlockSpec((1, hps, sk, qw), lambda i, h, j: (i, h, 0, 0)),
                  pl.BlockSpec((1, hps) + vt.shape[2:], lambda i, h, j: (i, h, 0, 0, 0)),
                  pl.BlockSpec((1, tqs, hps * V_DIM), lambda i, h, j: (i, j, h)),
                  _const_spec((tk, tq))],
        out_specs=pl.BlockSpec((1, tqs, hps * V_DIM), lambda i, h, j: (i, j, h)),
        out_shape=jax.ShapeDtypeStruct((b, sq, d), BF16),
        scratch_shapes=[pltpu.VMEM((n_sub, hps, 1, tq), F32), pltpu.VMEM((n_sub, hps, 1, tq), F32),
                        pltpu.VMEM((n_sub, hps, V_DIM, tq), F32), pltpu.VMEM((hps, tk, tq), F32)],
        compiler_params=pltpu.CompilerParams(
            dimension_semantics=("parallel", "parallel", "arbitrary"),
            vmem_limit_bytes=VMEM_LIMIT_BYTES),
        name="mla_attention_heads",
    )(q, k, vt, sgb, diag_bias)


def _attn_cached_kernel(q_ref, ckv_ref, kpe_ref, knew_ref, sgb_ref, w_uv_ref, yb_ref,
                        m_scr, l_scr, acc_scr, *, past_len, sq, tk):
    m = N_HEADS * sq
    q = q_ref[0].reshape(m, K_CAT)
    q_lat = q[:, 0:KV_RANK]
    q_rope = q[:, KV_RANK:K_CAT].astype(F32)
    q_rope = (q_rope[:, 0:QK_ROPE] + q_rope[:, QK_ROPE:LANES]).astype(BF16)
    q_pos = past_len + (lax.broadcasted_iota(jnp.int32, (m, 1), 0) & (sq - 1))
    k_lim = (q_pos // CHUNK + 1) * CHUNK

    m_scr[...] = jnp.full(m_scr.shape, NEG_INF, F32)
    l_scr[...] = jnp.zeros(l_scr.shape, F32)
    acc_scr[...] = jnp.zeros(acc_scr.shape, F32)
    nt = (((1,), (1,)), ((), ()))

    def update(s, visible, v):
        s = jnp.where(visible, s, NEG_INF)
        m_prev = m_scr[...]
        m_next = jnp.maximum(m_prev, jnp.max(s, axis=-1, keepdims=True))
        p = jnp.exp2(s - jnp.concatenate([m_next] * (s.shape[1] // LANES), axis=1))
        alpha = jnp.exp2(m_prev - m_next)
        l_scr[...] = alpha * l_scr[...] + jnp.sum(p, axis=-1, keepdims=True)
        m_scr[...] = m_next
        pv = jnp.dot(p.astype(BF16), v, preferred_element_type=F32)
        acc_scr[...] = acc_scr[...] * jnp.concatenate([alpha] * (KV_RANK // LANES), axis=1) + pv

    def past_body(j, c):
        off = pl.multiple_of(j * tk, tk)
        lat = ckv_ref[0, 0, pl.ds(off, tk), :].astype(BF16)
        pe = kpe_ref[0, 0, pl.ds(off, tk), :].astype(BF16)
        s = (lax.dot_general(q_lat, lat, nt, preferred_element_type=F32)
             + lax.dot_general(q_rope, pe, nt, preferred_element_type=F32))
        k_pos = off + lax.broadcasted_iota(jnp.int32, (m, tk), 1)
        update(s, k_pos < k_lim, lat)
        return c

    lax.fori_loop(0, past_len // tk, past_body, 0)

    kn = knew_ref[0]
    s = lax.dot_general(q, kn, nt, preferred_element_type=F32)
    idx = lax.broadcasted_iota(jnp.int32, (m, kn.shape[0]), 1)
    update(s, jnp.where(idx < sq, past_len + idx, k_lim) < k_lim, kn[:, 0:KV_RANK])

    inv_l = 1.0 / l_scr[...]
    o = (acc_scr[...] * jnp.concatenate([inv_l] * (KV_RANK // LANES), axis=1)).astype(BF16)
    sgb = sgb_ref[0]
    for hd in range(N_HEADS):
        att = jnp.dot(o[hd * sq:(hd + 1) * sq], w_uv_ref[hd], preferred_element_type=F32)
        sl = slice(hd * V_DIM, (hd + 1) * V_DIM)
        yb_ref[0, :, sl] = (att * sgb[:, sl]).astype(BF16)


def _attention_cached(q, cache_ckv, cache_kpe, layer, knew, sgb, w_uv, *, tk):
    b, _, sq, _ = q.shape
    past_len = cache_ckv.shape[2]
    d = sgb.shape[-1]
    m = N_HEADS * sq
    assert sq & (sq - 1) == 0 and past_len % tk == 0 and tk % LANES == 0 and knew.shape[1] % LANES == 0
    kern = functools.partial(_attn_cached_kernel, past_len=past_len, sq=sq, tk=tk)
    return pl.pallas_call(
        kern,
        grid=(b,),
        in_specs=[pl.BlockSpec((1, N_HEADS, sq, K_CAT), lambda i: (i, 0, 0, 0)),
                  pl.BlockSpec((1, 1, past_len, KV_RANK), lambda i: (layer, i, 0, 0)),
                  pl.BlockSpec((1, 1, past_len, QK_ROPE), lambda i: (layer, i, 0, 0)),
                  pl.BlockSpec((1,) + knew.shape[1:], lambda i: (i, 0, 0)),
                  pl.BlockSpec((1, sq, d), lambda i: (i, 0, 0)),
                  _const_spec(w_uv.shape)],
        out_specs=pl.BlockSpec((1, sq, d), lambda i: (i, 0, 0)),
        out_shape=jax.ShapeDtypeStruct((b, sq, d), BF16),
        scratch_shapes=[pltpu.VMEM((m, LANES), F32), pltpu.VMEM((m, LANES), F32),
                        pltpu.VMEM((m, KV_RANK), F32)],
        compiler_params=pltpu.CompilerParams(
            dimension_semantics=("parallel",), vmem_limit_bytes=VMEM_LIMIT_BYTES),
        name="mla_attention_cached",
    )(q, cache_ckv, cache_kpe, knew, sgb, w_uv)


def _outproj_kernel(x_ref, mod_ref, ya_ref, yb_ref, sua_ref, sub_ref,
                    w_a_ref, w_b_ref, w_o_ref, post_norm_ref, y_ref):
    bb, t, d = x_ref.shape
    m = bb * t
    ya = ya_ref[...].reshape(m, d)
    yb = yb_ref[...].reshape(m, d)
    pa = jnp.dot(ya, w_a_ref[...], preferred_element_type=F32)
    pb = jnp.dot(yb, w_b_ref[...], preferred_element_type=F32)
    merged = sua_ref[...].reshape(m, d).astype(F32) * pa + sub_ref[...].reshape(m, d).astype(F32) * pb
    o = jnp.dot(merged.astype(BF16), w_o_ref[...], preferred_element_type=F32)
    gate = mod_ref[...][:, :, 2 * d:3 * d]
    y_ref[...] = x_ref[...] + gate * _rms(o, post_norm_ref[...]).reshape(bb, t, d)


def _outproj(x, mod, ya, yb, sua, sub, lw, *, bb, t):
    b, s, d = x.shape
    tok = pl.BlockSpec((bb, t, d), lambda i, j: (i, j, 0))
    weights = [lw["w_branch_a"], lw["w_branch_b"], lw["w_out"], lw["post_norm"]]
    return pl.pallas_call(
        _outproj_kernel,
        grid=(b // bb, s // t),
        in_specs=[tok, pl.BlockSpec((bb, 1, 3 * d), lambda i, j: (i, 0, 0)), tok, tok, tok, tok]
        + [_const_spec(w.shape) for w in weights],
        out_specs=tok,
        out_shape=jax.ShapeDtypeStruct((b, s, d), F32),
        compiler_params=pltpu.CompilerParams(
            dimension_semantics=("parallel", "parallel"), vmem_limit_bytes=VMEM_LIMIT_BYTES),
        name="outproj",
    )(x, mod, ya, yb, sua, sub, *weights)


def _rot_half(w):
    half = w.shape[-1] // 2
    return jnp.concatenate([-w[..., half:], w[..., :half]], axis=-1)


def _twice(w):
    return jnp.concatenate([w, w], axis=-1)


def _layer_weights(l, p):
    d = p["w_in"].shape[1]
    w_in = p["w_in"][l].astype(BF16)
    o = 0
    seg = {}
    for name, width in (("xa", d), ("ga", d), ("cq", Q_RANK), ("ckv", KV_RANK), ("kpe", QK_ROPE),
                        ("gb", d), ("ua", d), ("ub", d)):
        seg[name] = w_in[:, o:o + width]
        o += width
    w_in_ext = jnp.concatenate(
        [seg["xa"], seg["ga"], seg["cq"], seg["ckv"], _twice(seg["kpe"]),
         _twice(_rot_half(seg["kpe"])), seg["gb"], seg["ua"], seg["ub"]], axis=1)
    wq = p["w_q_up"][l].reshape(Q_RANK, N_HEADS, QK_NOPE + QK_ROPE)
    wq_pe = wq[:, :, QK_NOPE:]
    w_q = jnp.concatenate(
        [wq[:, :, :QK_NOPE].reshape(Q_RANK, -1), wq_pe.reshape(Q_RANK, -1)], axis=1)
    row = lambda v: v.reshape(1, -1)
    return {
        "w_in": w_in_ext,
        "w_q": w_q.astype(BF16),
        "w_uk": jnp.transpose(p["w_uk"][l], (1, 2, 0)).astype(BF16),
        "w_ukv": jnp.concatenate([p["w_uk"][l].reshape(KV_RANK, -1), p["w_uv"][l].reshape(KV_RANK, -1)],
                                 axis=1).astype(BF16),
        "w_uv": jnp.transpose(p["w_uv"][l], (1, 0, 2)).astype(BF16),
        "w_gate": jnp.concatenate([p["lru_wa"][l], p["lru_wx"][l]], axis=-1).astype(BF16),
        "pre_norm": row(p["pre_norm"][l]), "post_norm": row(p["post_norm"][l]),
        "conv_w": p["conv_w"][l], "conv_b": row(p["conv_b"][l]),
        "lru_ba": row(p["lru_ba"][l]), "lru_bx": row(p["lru_bx"][l]),
        "lru_lambda": row(p["lru_lambda"][l]),
        "q_norm": row(p["q_norm"][l]) * Q_SCALE, "kv_norm": row(p["kv_norm"][l]),
        "w_branch_a": p["w_branch_a"][l].astype(BF16), "w_branch_b": p["w_branch_b"][l].astype(BF16),
        "w_out": p["w_out"][l].astype(BF16),
    }


def _rope_tables(pos):
    half = QK_ROPE // 2
    inv = ROPE_THETA ** (-jnp.arange(half, dtype=F32) / half)
    ang = pos.astype(F32)[:, None] * inv[None, :]
    cos, sin = jnp.cos(ang), jnp.sin(ang)
    return jnp.concatenate([cos] * (LANES // half), axis=1), jnp.concatenate([sin] * (LANES // half), axis=1)


def _pick_tile(n, target):
    t = min(n, target)
    while n % t:
        t //= 2
    return t


def kernel(x_prompt, x_sample, c_prompt, c_sample, cache_ckv, cache_kpe, state_conv, state_lru, ada_w, ada_b, pre_norm, post_norm, w_in, conv_w, conv_b, lru_wa, lru_ba, lru_wx, lru_bx, lru_lambda, q_norm, w_q_up, kv_norm, w_uk, w_uv, w_branch_a, w_branch_b, w_out):
    p = dict(ada_w=ada_w, ada_b=ada_b, pre_norm=pre_norm, post_norm=post_norm, w_in=w_in,
             conv_w=conv_w, conv_b=conv_b, lru_wa=lru_wa, lru_ba=lru_ba, lru_wx=lru_wx, lru_bx=lru_bx,
             lru_lambda=lru_lambda, q_norm=q_norm, w_q_up=w_q_up, kv_norm=kv_norm, w_uk=w_uk,
             w_uv=w_uv, w_branch_a=w_branch_a, w_branch_b=w_branch_b, w_out=w_out)
    depth = w_in.shape[0]
    b_p, s_p, d = x_prompt.shape
    b_s, s_s, _ = x_sample.shape
    past_len = cache_ckv.shape[2]
    assert s_p % SUBLANES == 0 and s_s % SUBLANES == 0 and s_p >= SUBLANES and s_s >= SUBLANES

    mods = _modulation(jnp.concatenate([c_prompt, c_sample], axis=0), ada_w, ada_b)
    cos_p, sin_p = _rope_tables(jnp.arange(s_p, dtype=jnp.int32))
    cos_s, sin_s = _rope_tables(past_len + jnp.arange(s_s, dtype=jnp.int32))
    conv0_p = jnp.zeros((b_p, SUBLANES, d), F32)
    h0_p = jnp.zeros((b_p, 1, d), F32)

    t_in_p = _pick_tile(s_p, T_INPROJ)
    t_out_p = _pick_tile(s_p, T_OUTPROJ)
    tq_p = tk_p = _pick_tile(s_p, T_ATTN)
    tk_s = _pick_tile(past_len, T_CACHED)

    yp, ys = x_prompt, x_sample
    outs_p, outs_s = [], []
    lat_p = lat_s = None
    for l in range(depth):
        lw = _layer_weights(l, p)

        mod_p = mods[l, :b_p, None, :]
        ya, sgb, sua, sub, q, ckv, kpe, k_heads, convo, hlast, vt = _inproj(
            yp, mod_p, conv0_p, h0_p, cos_p, sin_p, lw, bb=1, t=t_in_p, per_head=True,
            layer=l, depth=depth, stacked=lat_p)
        lat_p = (ckv, kpe)
        yb = _attention_heads(q, k_heads, vt, sgb, tq=tq_p, tk=tk_p)
        yp = _outproj(yp, mod_p, ya, yb, sua, sub, lw, bb=1, t=t_out_p)
        outs_p.append((convo[:, SUBLANES - (CONV_W - 1):, :], hlast[:, 0, :]))

        mod_s = mods[l, b_p:, None, :]
        conv0_s = jnp.pad(state_conv[l], ((0, 0), (SUBLANES - (CONV_W - 1), 0), (0, 0)))
        ya, sgb, sua, sub, q, ckv, kpe, kcat, convo, hlast = _inproj(
            ys, mod_s, conv0_s, state_lru[l][:, None, :], cos_s, sin_s, lw, bb=b_s, t=s_s,
            per_head=False, layer=l, depth=depth, stacked=lat_s)
        lat_s = (ckv, kpe)
        knew = jnp.pad(kcat, ((0, 0), (0, -s_s % LANES), (0, 0)))
        yb = _attention_cached(q, cache_ckv, cache_kpe, l, knew, sgb, lw["w_uv"], tk=tk_s)
        ys = _outproj(ys, mod_s, ya, yb, sua, sub, lw, bb=b_s, t=s_s)
        outs_s.append((convo[:, SUBLANES - (CONV_W - 1):, :], hlast[:, 0, :]))

    stack = lambda outs, k: jnp.stack([o[k] for o in outs])
    return (yp, ys,
            lat_p[0], lat_p[1], stack(outs_p, 0), stack(outs_p, 1),
            lat_s[0], lat_s[1], stack(outs_s, 0), stack(outs_s, 1))
```

```python
import functools
import math

import jax
import jax.numpy as jnp
from jax import lax
from jax.experimental import pallas as pl
from jax.experimental.pallas import tpu as pltpu

F32 = jnp.float32
BF16 = jnp.bfloat16

EPS = 1e-6
NEG_INF = -1e30
CHUNK = 64
LRU_C = 8.0
ROPE_THETA = 10000.0

LANES = 128
SUBLANES = 8
MXU_DIM = 256
VMEM_LIMIT_BYTES = 56 * 1024 * 1024

T_INPROJ = MXU_DIM
T_OUTPROJ = 4 * MXU_DIM
T_ATTN = 2 * MXU_DIM
T_CACHED = 8 * MXU_DIM

CONV_W = 4
N_HEADS = 8
QK_NOPE = 128
QK_ROPE = 64
V_DIM = 128
KV_RANK = 256
Q_RANK = 768
LRU_BLOCKS = 8
K_CAT = KV_RANK + LANES
Q_SCALE = (QK_NOPE + QK_ROPE) ** -0.5 * math.log2(math.e)


def _sigmoid(x):
    return jax.nn.sigmoid(x)


def _silu(x):
    return x * _sigmoid(x)


def _sqrt_nonneg(x):
    return jnp.where(x > 0.0, x * lax.rsqrt(x), 0.0)


def _rms(x, g):
    return x * lax.rsqrt(jnp.mean(x * x, axis=-1, keepdims=True) + EPS) * g


def _const_spec(shape):
    nd = len(shape)
    return pl.BlockSpec(shape, lambda *_: (0,) * nd, pipeline_mode=pl.Buffered(1))


def _mod_kernel(c_ref, w_ref, b_ref, o_ref):
    c = c_ref[...]
    o_ref[0] = jnp.dot(_silu(c), w_ref[0], preferred_element_type=F32,
                       precision=lax.Precision.HIGHEST) + b_ref[0]


def _modulation(c_all, ada_w, ada_b):
    depth, d, d3 = ada_w.shape
    n = c_all.shape[0]
    nblk = d3 // d
    return pl.pallas_call(
        _mod_kernel,
        grid=(depth, nblk),
        in_specs=[pl.BlockSpec((n, d), lambda l, j: (0, 0)),
                  pl.BlockSpec((1, d, d), lambda l, j: (l, 0, j)),
                  pl.BlockSpec((1, 1, d), lambda l, j: (l, 0, j))],
        out_specs=pl.BlockSpec((1, n, d), lambda l, j: (l, 0, j)),
        out_shape=jax.ShapeDtypeStruct((depth, n, d3), F32),
        name="adaln_mod",
    )(c_all, ada_w, ada_b.reshape(depth, 1, d3))


def _inproj_kernel(x_ref, mod_ref, conv0_ref, h0_ref, cos_ref, sin_ref,
                   w_in_ref, w_q_ref, w_uk_ref, w_gate_ref,
                   pre_norm_ref, conv_w_ref, conv_b_ref, ba_ref, bx_ref, lam_ref, qn_ref, kvn_ref,
                   *rest, per_head, n_alias):
    (ya_ref, sgb_ref, sua_ref, sub_ref, q_ref, ckv_ref, kpe_ref, kcat_ref,
     convo_ref, hlast_ref, *rest) = rest[n_alias:]
    if per_head:
        vt_ref, buf_ref, hcar_ref, hb_ref, cqn_ref, v_scr = rest
    else:
        buf_ref, hcar_ref, hb_ref, cqn_ref = rest
    bb, t, d = x_ref.shape
    q_main = QK_NOPE if per_head else KV_RANK
    m = bb * t
    groups = t // SUBLANES
    gw = d // LRU_BLOCKS
    cw_ = MXU_DIM
    n_chunks = d // cw_
    blocks_per_chunk = cw_ // gw
    nq = N_HEADS * QK_NOPE
    npe = N_HEADS * QK_ROPE
    o_xa, o_ga, o_cq = 0, d, 2 * d
    o_ckv = o_cq + Q_RANK
    o_kpe = o_ckv + KV_RANK
    o_krot = o_kpe + LANES
    o_gb = o_krot + LANES
    o_ua = o_gb + d
    o_ub = o_ua + d

    @pl.when(pl.program_id(1) == 0)
    def _():
        buf_ref[:, 0:SUBLANES, :] = conv0_ref[...]
        hcar_ref[...] = h0_ref[...]

    x = x_ref[...]
    mod = mod_ref[...]
    shift = mod[:, :, 0:d]
    scale = mod[:, :, d:2 * d]
    h = _rms(x, pre_norm_ref[...]) * (1.0 + scale) + shift
    hb_ref[...] = h.reshape(m, d).astype(BF16)

    def proj(lo, width):
        return jnp.dot(hb_ref[...], w_in_ref[:, lo:lo + width], preferred_element_type=F32)

    cos = cos_ref[...][None]
    sin = sin_ref[...][None]

    def conv_phase(c):
        cs = slice(c * cw_, (c + 1) * cw_)
        xa = proj(o_xa + c * cw_, cw_)
        buf_ref[:, SUBLANES:SUBLANES + t, cs] = xa.reshape(bb, t, cw_)
        cw = conv_w_ref[:, cs]
        xc = conv_b_ref[:, cs] + buf_ref[:, SUBLANES - 3:SUBLANES - 3 + t, cs] * cw[0:1]
        xc = xc + buf_ref[:, SUBLANES - 2:SUBLANES - 2 + t, cs] * cw[1:2]
        xc = xc + buf_ref[:, SUBLANES - 1:SUBLANES - 1 + t, cs] * cw[2:3]
        xc = xc + buf_ref[:, SUBLANES:SUBLANES + t, cs] * cw[3:4]
        tail = buf_ref[:, t:t + SUBLANES, cs]
        convo_ref[:, :, cs] = tail
        buf_ref[:, 0:SUBLANES, cs] = tail
        return xc.reshape(m, cw_)

    def gate_phase(c, xc2):
        cs = slice(c * cw_, (c + 1) * cw_)
        xcb = xc2.astype(BF16)
        r_parts, i_parts = [], []
        for n in range(blocks_per_chunk):
            g = jnp.dot(xcb[:, n * gw:(n + 1) * gw], w_gate_ref[c * blocks_per_chunk + n],
                        preferred_element_type=F32)
            r_parts.append(g[:, 0:gw])
            i_parts.append(g[:, gw:2 * gw])
        r = _sigmoid(jnp.concatenate(r_parts, axis=1) + ba_ref[:, cs])
        ig = _sigmoid(jnp.concatenate(i_parts, axis=1) + bx_ref[:, cs])
        nl = -lam_ref[:, cs]
        softplus = jnp.maximum(nl, 0.0) + jnp.log1p(jnp.exp(-jnp.abs(nl)))
        th = jnp.tanh((-0.5 * LRU_C * r) * softplus)
        inv = 1.0 / (1.0 - th)
        a = (1.0 + th) * inv
        mult = (2.0 * _sqrt_nonneg(-th)) * inv
        return a, mult * (ig * xc2)

    def scan_phase(c, a, b):
        cs = slice(c * cw_, (c + 1) * cw_)
        a4 = a.reshape(bb * groups, SUBLANES, cw_)
        b4 = b.reshape(bb * groups, SUBLANES, cw_)
        row = lax.broadcasted_iota(jnp.int32, a4.shape, 1)
        for sft in (1, 2, 4):
            keep = row >= sft
            a_sh = jnp.where(keep, pltpu.roll(a4, sft, axis=1), 1.0)
            b_sh = jnp.where(keep, pltpu.roll(b4, sft, axis=1), 0.0)
            b4 = b4 + a4 * b_sh
            a4 = a4 * a_sh
        a5 = a4.reshape(bb, groups, SUBLANES, cw_)
        b5 = b4.reshape(bb, groups, SUBLANES, cw_)
        hc = hcar_ref[:, :, cs]
        outs = []
        for g in range(groups):
            hs = a5[:, g] * hc + b5[:, g]
            outs.append(hs)
            hc = hs[:, SUBLANES - 1:SUBLANES, :]
        hcar_ref[:, :, cs] = hc
        hlast_ref[:, :, cs] = hc
        return jnp.concatenate(outs, axis=1).reshape(m, cw_)

    def out_phase(c, y_lru):
        cs = slice(c * cw_, (c + 1) * cw_)
        ga = proj(o_ga + c * cw_, cw_)
        ya_ref[:, :, cs] = (y_lru * _silu(ga)).astype(BF16).reshape(bb, t, cw_)

    def gate_out(ref, fn, off, c):
        cs = slice(c * cw_, (c + 1) * cw_)
        ref[:, :, cs] = fn(proj(off + c * cw_, cw_)).astype(BF16).reshape(bb, t, cw_)

    def q_latent(heads):
        q_nope = jnp.dot(cqn_ref[...], w_q_ref[:, heads[0] * QK_NOPE:(heads[-1] + 1) * QK_NOPE],
                         preferred_element_type=F32)
        if per_head:
            width = len(heads) * QK_NOPE
            v_scr[:, 0:width] = q_nope
            for k, hd in enumerate(heads):
                q_ref[0, hd, 0, 0:QK_NOPE, :] = v_scr[:, k * QK_NOPE:(k + 1) * QK_NOPE].T.astype(BF16)
            return
        q_nope = q_nope.astype(BF16)
        for k, hd in enumerate(heads):
            q_h = jnp.dot(q_nope[:, k * QK_NOPE:(k + 1) * QK_NOPE], w_uk_ref[hd],
                          preferred_element_type=F32).astype(BF16)
            q_ref[:, hd, :, 0:q_main] = q_h.reshape(bb, t, q_main)

    def q_rotary():
        q_pe = jnp.dot(cqn_ref[...], w_q_ref[:, nq:nq + npe], preferred_element_type=F32)
        lane = lax.broadcasted_iota(jnp.int32, (1, 1, LANES), 2)
        half = QK_ROPE // 2
        low_half = (lane & (QK_ROPE - 1)) < half
        for pair in range(N_HEADS // 2):
            sl = slice(pair * LANES, (pair + 1) * LANES)
            x = q_pe[:, sl].reshape(bb, t, LANES)
            rot = jnp.where(low_half, -pltpu.roll(x, LANES - half, axis=2), pltpu.roll(x, half, axis=2))
            both = x * cos + rot * sin
            even = jnp.where(lane < QK_ROPE, both, 0.0)
            odd = jnp.where(lane < QK_ROPE, 0.0, both)
            rs = slice(q_main, q_main + LANES)
            if per_head:
                q_ref[0, 2 * pair, 0, rs, :] = even.reshape(m, LANES).T.astype(BF16)
                q_ref[0, 2 * pair + 1, 0, rs, :] = odd.reshape(m, LANES).T.astype(BF16)
            else:
                q_ref[:, 2 * pair, :, rs] = even.astype(BF16)
                q_ref[:, 2 * pair + 1, :, rs] = odd.astype(BF16)

    def key_latents():
        ckv = _rms(proj(o_ckv, KV_RANK), kvn_ref[...])
        ckv_ref[...] = ckv.reshape(bb, t, KV_RANK)
        ckvb = ckv.astype(BF16)
        kp = proj(o_kpe, LANES).reshape(bb, t, LANES)
        kr = proj(o_krot, LANES).reshape(bb, t, LANES)
        kroped = kp * cos + kr * sin
        kpe_ref[...] = kroped[:, :, 0:QK_ROPE]
        if per_head:
            nk = N_HEADS * QK_NOPE
            k_nope = jnp.dot(ckvb, w_uk_ref[:, 0:nk], preferred_element_type=F32).astype(BF16)
            v_scr[...] = jnp.dot(ckvb, w_uk_ref[:, nk:nk + N_HEADS * V_DIM], preferred_element_type=F32)
            v_t = v_scr[...].T
            for hd in range(N_HEADS):
                kcat_ref[:, hd, :, 0:QK_NOPE] = k_nope[:, hd * QK_NOPE:(hd + 1) * QK_NOPE].reshape(
                    bb, t, QK_NOPE)
                kcat_ref[:, hd, :, QK_NOPE:QK_NOPE + LANES] = kroped.astype(BF16)
                vt_ref[0, hd, 0] = v_t[hd * V_DIM:(hd + 1) * V_DIM].astype(BF16)
        else:
            kcat_ref[:, :, 0:KV_RANK] = ckvb.reshape(bb, t, KV_RANK)
            kcat_ref[:, :, KV_RANK:K_CAT] = kroped.astype(BF16)

    cqn_ref[...] = _rms(proj(o_cq, Q_RANK), qn_ref[...]).astype(BF16)
    other = [lambda: q_latent((0, 1, 2, 3)), lambda: q_latent((4, 5, 6, 7)), q_rotary, key_latents]
    for c in range(n_chunks):
        xc2 = conv_phase(c)
        gate_out(sgb_ref, _silu, o_gb, c)
        a, b = gate_phase(c, xc2)
        gate_out(sua_ref, _sigmoid, o_ua, c)
        y_lru = scan_phase(c, a, b)
        gate_out(sub_ref, _sigmoid, o_ub, c)
        out_phase(c, y_lru)
        if c < len(other):
            other[c]()
    for f in other[n_chunks:]:
        f()


def _inproj(x, mod, conv0, h0, cos, sin, lw, *, bb, t, per_head, layer, depth, stacked):
    b, s, d = x.shape
    grid = (b // bb, s // t)
    tok = lambda w: pl.BlockSpec((bb, t, w), lambda i, j: (i, j, 0))
    tok_l = lambda w: pl.BlockSpec((None, bb, t, w), lambda i, j: (layer, i, j, 0))
    per_b = lambda r, w: pl.BlockSpec((bb, r, w), lambda i, j: (i, 0, 0))
    heads = lambda w: pl.BlockSpec((bb, N_HEADS, t, w), lambda i, j: (i, 0, j, 0))
    q_w = (QK_NOPE if per_head else KV_RANK) + LANES
    weights = [lw["w_in"], lw["w_q"], lw["w_ukv"] if per_head else lw["w_uk"], lw["w_gate"],
               lw["pre_norm"], lw["conv_w"],
               lw["conv_b"], lw["lru_ba"], lw["lru_bx"], lw["lru_lambda"], lw["q_norm"], lw["kv_norm"]]
    in_specs = ([tok(d), per_b(1, 3 * d), per_b(SUBLANES, d), per_b(1, d),
                 pl.BlockSpec((t, LANES), lambda i, j: (j, 0)),
                 pl.BlockSpec((t, LANES), lambda i, j: (j, 0))]
                + [_const_spec(w.shape) for w in weights])
    q_shape = (b, N_HEADS, s // t, q_w, t) if per_head else (b, N_HEADS, s, q_w)
    q_spec = (pl.BlockSpec((1, N_HEADS, 1, q_w, t), lambda i, j: (i, 0, j, 0, 0)) if per_head
              else heads(q_w))
    out_shape = [jax.ShapeDtypeStruct((b, s, d), BF16)] * 4 + [
        jax.ShapeDtypeStruct(q_shape, BF16),
        jax.ShapeDtypeStruct((depth, b, s, KV_RANK), F32),
        jax.ShapeDtypeStruct((depth, b, s, QK_ROPE), F32),
        jax.ShapeDtypeStruct((b, N_HEADS, s, q_w) if per_head else (b, s, K_CAT), BF16),
        jax.ShapeDtypeStruct((b, SUBLANES, d), F32),
        jax.ShapeDtypeStruct((b, 1, d), F32),
    ]
    out_specs = [tok(d)] * 4 + [
        q_spec, tok_l(KV_RANK), tok_l(QK_ROPE), heads(q_w) if per_head else tok(K_CAT),
        per_b(SUBLANES, d), per_b(1, d),
    ]
    aliased = [] if stacked is None else list(stacked)
    n_in = len(in_specs)
    in_specs = in_specs + [pl.BlockSpec(memory_space=pl.ANY)] * len(aliased)
    aliases = {n_in + k: 5 + k for k in range(len(aliased))}
    if per_head:
        assert bb == 1
        out_shape.append(jax.ShapeDtypeStruct((b, N_HEADS, s // t, V_DIM, t), BF16))
        out_specs.append(pl.BlockSpec((1, N_HEADS, 1, V_DIM, t), lambda i, j: (i, 0, j, 0, 0)))
    return pl.pallas_call(
        functools.partial(_inproj_kernel, per_head=per_head, n_alias=len(aliased)),
        grid=grid,
        in_specs=in_specs,
        out_specs=out_specs,
        out_shape=out_shape,
        input_output_aliases=aliases,
        scratch_shapes=[pltpu.VMEM((bb, t + SUBLANES, d), F32), pltpu.VMEM((bb, 1, d), F32),
                        pltpu.VMEM((bb * t, d), BF16), pltpu.VMEM((bb * t, Q_RANK), BF16)]
        + ([pltpu.VMEM((bb * t, N_HEADS * V_DIM), F32)] if per_head else []),
        compiler_params=pltpu.CompilerParams(
            dimension_semantics=("parallel", "arbitrary"), vmem_limit_bytes=VMEM_LIMIT_BYTES),
        name="inproj_lru",
    )(x, mod, conv0, h0, cos, sin, *weights, *aliased)


def _attn_heads_kernel(q_ref, k_ref, vt_ref, sgb_ref, bias_ref, yb_ref, m_scr, l_scr, acc_scr, s_scr,
                       *, tk):
    n_groups = q_ref.shape[1]
    q_gran = q_ref.shape[-1]
    n_sub = q_ref.shape[2] * q_gran // tk
    gran = vt_ref.shape[-1]

    m_scr[...] = jnp.full(m_scr.shape, NEG_INF, F32)
    l_scr[...] = jnp.zeros(l_scr.shape, F32)
    acc_scr[...] = jnp.zeros(acc_scr.shape, F32)

    def qk(u, j, g):
        k = k_ref[0, g, pl.ds(pl.multiple_of(j * tk, tk), tk), :]
        q_t = jnp.concatenate([q_ref[0, g, u * (tk // q_gran) + i] for i in range(tk // q_gran)], axis=1)
        return jnp.dot(k, q_t, preferred_element_type=F32)

    def step(u, j, masked, last_tile):
        for g in range(n_groups):
            if g + 1 < n_groups:
                s_scr[g + 1] = qk(u, j, g + 1)
            elif not last_tile:
                s_scr[0] = qk(u, j + 1, 0)
            s = s_scr[g]
            if masked:
                s = s + bias_ref[...]
            m_prev = m_scr[u, g]
            m_next = jnp.maximum(m_prev, jnp.max(s, axis=0, keepdims=True))
            p = jnp.exp2(s - m_next)
            alpha = jnp.exp2(m_prev - m_next)
            l_scr[u, g] = alpha * l_scr[u, g] + jnp.sum(p, axis=0, keepdims=True)
            m_scr[u, g] = m_next
            vt = jnp.concatenate([vt_ref[0, g, j * (tk // gran) + i] for i in range(tk // gran)], axis=1)
            pv = jnp.dot(vt, p.astype(BF16), preferred_element_type=F32)
            acc_scr[u, g] = acc_scr[u, g] * alpha + pv

    s_scr[0] = qk(0, 0, 0)
    for u in range(n_sub):
        last = pl.program_id(2) * n_sub + u

        def pair(jj, c, u=u):
            step(u, 2 * jj, False, False)
            step(u, 2 * jj + 1, False, False)
            return c

        lax.fori_loop(0, last // 2, pair, 0)
        if n_sub % 2 == 0:
            if u % 2 == 1:
                step(u, last - 1, False, False)
        else:
            @pl.when(last % 2 == 1)
            def _(u=u, last=last):
                step(u, last - 1, False, False)
        step(u, last, True, True)
        if u + 1 < n_sub:
            s_scr[0] = qk(u + 1, 0, 0)
        for g in range(n_groups):
            o = (acc_scr[u, g] * (1.0 / l_scr[u, g])).T
            sl = slice(g * V_DIM, (g + 1) * V_DIM)
            yb_ref[0, u * tk:(u + 1) * tk, sl] = (o * sgb_ref[0, u * tk:(u + 1) * tk, sl]).astype(BF16)


def _attention_heads(q, k, vt, sgb, *, tq, tk, heads_per_step=4):
    b, n_heads, q_grans, qw, q_gran = q.shape
    sq = q_grans * q_gran
    sk = k.shape[2]
    d = sgb.shape[-1]
    gran = vt.shape[-1]
    hps = heads_per_step
    n_sub = max(n for n in (4, 2, 1) if sq % (n * tq) == 0)
    tqs = n_sub * tq
    assert n_heads % hps == 0 and sq % tqs == 0 and sk % tk == 0 and tk % gran == 0 and tq % q_gran == 0
    assert vt.shape[2] * gran == sk and (hps * V_DIM) % LANES == 0
    assert tk == tq and tq % CHUNK == 0 and sq == sk
    k_idx = lax.broadcasted_iota(jnp.int32, (tk, tq), 0)
    q_idx = lax.broadcasted_iota(jnp.int32, (tk, tq), 1)
    diag_bias = jnp.where(k_idx // CHUNK <= q_idx // CHUNK, 0.0, NEG_INF).astype(F32)
    kern = functools.partial(_attn_heads_kernel, tk=tk)
    return pl.pallas_call(
        kern,
        grid=(b, n_heads // hps, sq // tqs),
        in_specs=[pl.BlockSpec((1, hps, tqs // q_gran, qw, q_gran), lambda i, h, j: (i, h, j, 0, 0)),
                  pl.BlockSpec((1, hps, sk, qw), lambda i, h, j: (i, h, 0, 0)),
                  pl.BlockSpec((1, hps) + vt.shape[2:], lambda i, h, j: (i, h, 0, 0, 0)),
                  pl.BlockSpec((1, tqs, hps * V_DIM), lambda i, h, j: (i, j, h)),
                  _const_spec((tk, tq))],
        out_specs=pl.BlockSpec((1, tqs, hps * V_DIM), lambda i, h, j: (i, j, h)),
        out_shape=jax.ShapeDtypeStruct((b, sq, d), BF16),
        scratch_shapes=[pltpu.VMEM((n_sub, hps, 1, tq), F32), pltpu.VMEM((n_sub, hps, 1, tq), F32),
                        pltpu.VMEM((n_sub, hps, V_DIM, tq), F32), pltpu.VMEM((hps, tk, tq), F32)],
        compiler_params=pltpu.CompilerParams(
            dimension_semantics=("parallel", "parallel", "arbitrary"),
            vmem_limit_bytes=VMEM_LIMIT_BYTES),
        name="mla_attention_heads",
    )(q, k, vt, sgb, diag_bias)


def _attn_cached_kernel(q_ref, ckv_ref, kpe_ref, knew_ref, sgb_ref, w_uv_ref, yb_ref,
                        m_scr, l_scr, acc_scr, *, past_len, sq, tk):
    m = N_HEADS * sq
    q = q_ref[0].reshape(m, K_CAT)
    q_lat = q[:, 0:KV_RANK]
    q_rope = q[:, KV_RANK:K_CAT].astype(F32)
    q_rope = (q_rope[:, 0:QK_ROPE] + q_rope[:, QK_ROPE:LANES]).astype(BF16)
    q_pos = past_len + (lax.broadcasted_iota(jnp.int32, (m, 1), 0) & (sq - 1))
    k_lim = (q_pos // CHUNK + 1) * CHUNK

    m_scr[...] = jnp.full(m_scr.shape, NEG_INF, F32)
    l_scr[...] = jnp.zeros(l_scr.shape, F32)
    acc_scr[...] = jnp.zeros(acc_scr.shape, F32)
    nt = (((1,), (1,)), ((), ()))

    def update(s, visible, v):
        s = jnp.where(visible, s, NEG_INF)
        m_prev = m_scr[...]
        m_next = jnp.maximum(m_prev, jnp.max(s, axis=-1, keepdims=True))
        p = jnp.exp2(s - jnp.concatenate([m_next] * (s.shape[1] // LANES), axis=1))
        alpha = jnp.exp2(m_prev - m_next)
        l_scr[...] = alpha * l_scr[...] + jnp.sum(p, axis=-1, keepdims=True)
        m_scr[...] = m_next
        pv = jnp.dot(p.astype(BF16), v, preferred_element_type=F32)
        acc_scr[...] = acc_scr[...] * jnp.concatenate([alpha] * (KV_RANK // LANES), axis=1) + pv

    def past_body(j, c):
        off = pl.multiple_of(j * tk, tk)
        lat = ckv_ref[0, 0, pl.ds(off, tk), :].astype(BF16)
        pe = kpe_ref[0, 0, pl.ds(off, tk), :].astype(BF16)
        s = (lax.dot_general(q_lat, lat, nt, preferred_element_type=F32)
             + lax.dot_general(q_rope, pe, nt, preferred_element_type=F32))
        k_pos = off + lax.broadcasted_iota(jnp.int32, (m, tk), 1)
        update(s, k_pos < k_lim, lat)
        return c

    lax.fori_loop(0, past_len // tk, past_body, 0)

    kn = knew_ref[0]
    s = lax.dot_general(q, kn, nt, preferred_element_type=F32)
    idx = lax.broadcasted_iota(jnp.int32, (m, kn.shape[0]), 1)
    update(s, jnp.where(idx < sq, past_len + idx, k_lim) < k_lim, kn[:, 0:KV_RANK])

    inv_l = 1.0 / l_scr[...]
    o = (acc_scr[...] * jnp.concatenate([inv_l] * (KV_RANK // LANES), axis=1)).astype(BF16)
    sgb = sgb_ref[0]
    for hd in range(N_HEADS):
        att = jnp.dot(o[hd * sq:(hd + 1) * sq], w_uv_ref[hd], preferred_element_type=F32)
        sl = slice(hd * V_DIM, (hd + 1) * V_DIM)
        yb_ref[0, :, sl] = (att * sgb[:, sl]).astype(BF16)


def _attention_cached(q, cache_ckv, cache_kpe, layer, knew, sgb, w_uv, *, tk):
    b, _, sq, _ = q.shape
    past_len = cache_ckv.shape[2]
    d = sgb.shape[-1]
    m = N_HEADS * sq
    assert sq & (sq - 1) == 0 and past_len % tk == 0 and tk % LANES == 0 and knew.shape[1] % LANES == 0
    kern = functools.partial(_attn_cached_kernel, past_len=past_len, sq=sq, tk=tk)
    return pl.pallas_call(
        kern,
        grid=(b,),
        in_specs=[pl.BlockSpec((1, N_HEADS, sq, K_CAT), lambda i: (i, 0, 0, 0)),
                  pl.BlockSpec((1, 1, past_len, KV_RANK), lambda i: (layer, i, 0, 0)),
                  pl.BlockSpec((1, 1, past_len, QK_ROPE), lambda i: (layer, i, 0, 0)),
                  pl.BlockSpec((1,) + knew.shape[1:], lambda i: (i, 0, 0)),
                  pl.BlockSpec((1, sq, d), lambda i: (i, 0, 0)),
                  _const_spec(w_uv.shape)],
        out_specs=pl.BlockSpec((1, sq, d), lambda i: (i, 0, 0)),
        out_shape=jax.ShapeDtypeStruct((b, sq, d), BF16),
        scratch_shapes=[pltpu.VMEM((m, LANES), F32), pltpu.VMEM((m, LANES), F32),
                        pltpu.VMEM((m, KV_RANK), F32)],
        compiler_params=pltpu.CompilerParams(
            dimension_semantics=("parallel",), vmem_limit_bytes=VMEM_LIMIT_BYTES),
        name="mla_attention_cached",
    )(q, cache_ckv, cache_kpe, knew, sgb, w_uv)


def _outproj_kernel(x_ref, mod_ref, ya_ref, yb_ref, sua_ref, sub_ref,
                    w_a_ref, w_b_ref, w_o_ref, post_norm_ref, y_ref):
    bb, t, d = x_ref.shape
    m = bb * t
    ya = ya_ref[...].reshape(m, d)
    yb = yb_ref[...].reshape(m, d)
    pa = jnp.dot(ya, w_a_ref[...], preferred_element_type=F32)
    pb = jnp.dot(yb, w_b_ref[...], preferred_element_type=F32)
    merged = sua_ref[...].reshape(m, d).astype(F32) * pa + sub_ref[...].reshape(m, d).astype(F32) * pb
    o = jnp.dot(merged.astype(BF16), w_o_ref[...], preferred_element_type=F32)
    gate = mod_ref[...][:, :, 2 * d:3 * d]
    y_ref[...] = x_ref[...] + gate * _rms(o, post_norm_ref[...]).reshape(bb, t, d)


def _outproj(x, mod, ya, yb, sua, sub, lw, *, bb, t):
    b, s, d = x.shape
    tok = pl.BlockSpec((bb, t, d), lambda i, j: (i, j, 0))
    weights = [lw["w_branch_a"], lw["w_branch_b"], lw["w_out"], lw["post_norm"]]
    return pl.pallas_call(
        _outproj_kernel,
        grid=(b // bb, s // t),
        in_specs=[tok, pl.BlockSpec((bb, 1, 3 * d), lambda i, j: (i, 0, 0)), tok, tok, tok, tok]
        + [_const_spec(w.shape) for w in weights],
        out_specs=tok,
        out_shape=jax.ShapeDtypeStruct((b, s, d), F32),
        compiler_params=pltpu.CompilerParams(
            dimension_semantics=("parallel", "parallel"), vmem_limit_bytes=VMEM_LIMIT_BYTES),
        name="outproj",
    )(x, mod, ya, yb, sua, sub, *weights)


def _rot_half(w):
    half = w.shape[-1] // 2
    return jnp.concatenate([-w[..., half:], w[..., :half]], axis=-1)


def _twice(w):
    return jnp.concatenate([w, w], axis=-1)


def _layer_weights(l, p):
    d = p["w_in"].shape[1]
    w_in = p["w_in"][l].astype(BF16)
    o = 0
    seg = {}
    for name, width in (("xa", d), ("ga", d), ("cq", Q_RANK), ("ckv", KV_RANK), ("kpe", QK_ROPE),
                        ("gb", d), ("ua", d), ("ub", d)):
        seg[name] = w_in[:, o:o + width]
        o += width
    w_in_ext = jnp.concatenate(
        [seg["xa"], seg["ga"], seg["cq"], seg["ckv"], _twice(seg["kpe"]),
         _twice(_rot_half(seg["kpe"])), seg["gb"], seg["ua"], seg["ub"]], axis=1)
    wq = p["w_q_up"][l].reshape(Q_RANK, N_HEADS, QK_NOPE + QK_ROPE)
    wq_pe = wq[:, :, QK_NOPE:]
    w_q = jnp.concatenate(
        [wq[:, :, :QK_NOPE].reshape(Q_RANK, -1), wq_pe.reshape(Q_RANK, -1)], axis=1)
    row = lambda v: v.reshape(1, -1)
    return {
        "w_in": w_in_ext,
        "w_q": w_q.astype(BF16),
        "w_uk": jnp.transpose(p["w_uk"][l], (1, 2, 0)).astype(BF16),
        "w_ukv": jnp.concatenate([p["w_uk"][l].reshape(KV_RANK, -1), p["w_uv"][l].reshape(KV_RANK, -1)],
                                 axis=1).astype(BF16),
        "w_uv": jnp.transpose(p["w_uv"][l], (1, 0, 2)).astype(BF16),
        "w_gate": jnp.concatenate([p["lru_wa"][l], p["lru_wx"][l]], axis=-1).astype(BF16),
        "pre_norm": row(p["pre_norm"][l]), "post_norm": row(p["post_norm"][l]),
        "conv_w": p["conv_w"][l], "conv_b": row(p["conv_b"][l]),
        "lru_ba": row(p["lru_ba"][l]), "lru_bx": row(p["lru_bx"][l]),
        "lru_lambda": row(p["lru_lambda"][l]),
        "q_norm": row(p["q_norm"][l]) * Q_SCALE, "kv_norm": row(p["kv_norm"][l]),
        "w_branch_a": p["w_branch_a"][l].astype(BF16), "w_branch_b": p["w_branch_b"][l].astype(BF16),
        "w_out": p["w_out"][l].astype(BF16),
    }


def _rope_tables(pos):
    half = QK_ROPE // 2
    inv = ROPE_THETA ** (-jnp.arange(half, dtype=F32) / half)
    ang = pos.astype(F32)[:, None] * inv[None, :]
    cos, sin = jnp.cos(ang), jnp.sin(ang)
    return jnp.concatenate([cos] * (LANES // half), axis=1), jnp.concatenate([sin] * (LANES // half), axis=1)


def _pick_tile(n, target):
    t = min(n, target)
    while n % t:
        t //= 2
    return t


def kernel(x_prompt, x_sample, c_prompt, c_sample, cache_ckv, cache_kpe, state_conv, state_lru, ada_w, ada_b, pre_norm, post_norm, w_in, conv_w, conv_b, lru_wa, lru_ba, lru_wx, lru_bx, lru_lambda, q_norm, w_q_up, kv_norm, w_uk, w_uv, w_branch_a, w_branch_b, w_out):
    p = dict(ada_w=ada_w, ada_b=ada_b, pre_norm=pre_norm, post_norm=post_norm, w_in=w_in,
             conv_w=conv_w, conv_b=conv_b, lru_wa=lru_wa, lru_ba=lru_ba, lru_wx=lru_wx, lru_bx=lru_bx,
             lru_lambda=lru_lambda, q_norm=q_norm, w_q_up=w_q_up, kv_norm=kv_norm, w_uk=w_uk,
             w_uv=w_uv, w_branch_a=w_branch_a, w_branch_b=w_branch_b, w_out=w_out)
    depth = w_in.shape[0]
    b_p, s_p, d = x_prompt.shape
    b_s, s_s, _ = x_sample.shape
    past_len = cache_ckv.shape[2]
    assert s_p % SUBLANES == 0 and s_s % SUBLANES == 0 and s_p >= SUBLANES and s_s >= SUBLANES

    mods = _modulation(jnp.concatenate([c_prompt, c_sample], axis=0), ada_w, ada_b)
    cos_p, sin_p = _rope_tables(jnp.arange(s_p, dtype=jnp.int32))
    cos_s, sin_s = _rope_tables(past_len + jnp.arange(s_s, dtype=jnp.int32))
    conv0_p = jnp.zeros((b_p, SUBLANES, d), F32)
    h0_p = jnp.zeros((b_p, 1, d), F32)

    t_in_p = _pick_tile(s_p, T_INPROJ)
    t_out_p = _pick_tile(s_p, T_OUTPROJ)
    tq_p = tk_p = _pick_tile(s_p, T_ATTN)
    tk_s = _pick_tile(past_len, T_CACHED)

    yp, ys = x_prompt, x_sample
    outs_p, outs_s = [], []
    lat_p = lat_s = None
    for l in range(depth):
        lw = _layer_weights(l, p)

        mod_p = mods[l, :b_p, None, :]
        ya, sgb, sua, sub, q, ckv, kpe, k_heads, convo, hlast, vt = _inproj(
            yp, mod_p, conv0_p, h0_p, cos_p, sin_p, lw, bb=1, t=t_in_p, per_head=True,
            layer=l, depth=depth, stacked=lat_p)
        lat_p = (ckv, kpe)
        yb = _attention_heads(q, k_heads, vt, sgb, tq=tq_p, tk=tk_p)
        yp = _outproj(yp, mod_p, ya, yb, sua, sub, lw, bb=1, t=t_out_p)
        outs_p.append((convo[:, SUBLANES - (CONV_W - 1):, :], hlast[:, 0, :]))

        mod_s = mods[l, b_p:, None, :]
        conv0_s = jnp.pad(state_conv[l], ((0, 0), (SUBLANES - (CONV_W - 1), 0), (0, 0)))
        ya, sgb, sua, sub, q, ckv, kpe, kcat, convo, hlast = _inproj(
            ys, mod_s, conv0_s, state_lru[l][:, None, :], cos_s, sin_s, lw, bb=b_s, t=s_s,
            per_head=False, layer=l, depth=depth, stacked=lat_s)
        lat_s = (ckv, kpe)
        knew = jnp.pad(kcat, ((0, 0), (0, -s_s % LANES), (0, 0)))
        yb = _attention_cached(q, cache_ckv, cache_kpe, l, knew, sgb, lw["w_uv"], tk=tk_s)
        ys = _outproj(ys, mod_s, ya, yb, sua, sub, lw, bb=b_s, t=s_s)
        outs_s.append((convo[:, SUBLANES - (CONV_W - 1):, :], hlast[:, 0, :]))

    stack = lambda outs, k: jnp.stack([o[k] for o in outs])
    return (yp, ys,
            lat_p[0], lat_p[1], stack(outs_p, 0), stack(outs_p, 1),
            lat_s[0], lat_s[1], stack(outs_s, 0), stack(outs_s, 1))
```

```python
import functools
import math

import jax
import jax.numpy as jnp
from jax import lax
from jax.experimental import pallas as pl
from jax.experimental.pallas import tpu as pltpu

F32 = jnp.float32
BF16 = jnp.bfloat16

EPS = 1e-6
NEG_INF = -1e30
CHUNK = 64
LRU_C = 8.0
ROPE_THETA = 10000.0

LANES = 128
SUBLANES = 8
MXU_DIM = 256
VMEM_LIMIT_BYTES = 56 * 1024 * 1024

T_INPROJ = MXU_DIM
T_OUTPROJ = 4 * MXU_DIM
T_ATTN = 2 * MXU_DIM
T_CACHED = 8 * MXU_DIM

CONV_W = 4
N_HEADS = 8
QK_NOPE = 128
QK_ROPE = 64
V_DIM = 128
KV_RANK = 256
Q_RANK = 768
LRU_BLOCKS = 8
K_CAT = KV_RANK + LANES
Q_SCALE = (QK_NOPE + QK_ROPE) ** -0.5 * math.log2(math.e)


def _sigmoid(x):
    return jax.nn.sigmoid(x)


def _silu(x):
    return x * _sigmoid(x)


def _sqrt_nonneg(x):
    return jnp.where(x > 0.0, x * lax.rsqrt(x), 0.0)


def _rms(x, g):
    return x * lax.rsqrt(jnp.mean(x * x, axis=-1, keepdims=True) + EPS) * g


def _const_spec(shape):
    nd = len(shape)
    return pl.BlockSpec(shape, lambda *_: (0,) * nd, pipeline_mode=pl.Buffered(1))


def _mod_kernel(c_ref, w_ref, b_ref, o_ref):
    c = c_ref[...]
    o_ref[0] = jnp.dot(_silu(c), w_ref[0], preferred_element_type=F32,
                       precision=lax.Precision.HIGHEST) + b_ref[0]


def _modulation(c_all, ada_w, ada_b):
    depth, d, d3 = ada_w.shape
    n = c_all.shape[0]
    nblk = d3 // d
    return pl.pallas_call(
        _mod_kernel,
        grid=(depth, nblk),
        in_specs=[pl.BlockSpec((n, d), lambda l, j: (0, 0)),
                  pl.BlockSpec((1, d, d), lambda l, j: (l, 0, j)),
                  pl.BlockSpec((1, 1, d), lambda l, j: (l, 0, j))],
        out_specs=pl.BlockSpec((1, n, d), lambda l, j: (l, 0, j)),
        out_shape=jax.ShapeDtypeStruct((depth, n, d3), F32),
        name="adaln_mod",
    )(c_all, ada_w, ada_b.reshape(depth, 1, d3))


def _inproj_kernel(x_ref, mod_ref, conv0_ref, h0_ref, cos_ref, sin_ref,
                   w_in_ref, w_q_ref, w_uk_ref, w_gate_ref,
                   pre_norm_ref, conv_w_ref, conv_b_ref, ba_ref, bx_ref, lam_ref, qn_ref, kvn_ref,
                   *rest, per_head, n_alias):
    (ya_ref, sgb_ref, sua_ref, sub_ref, q_ref, ckv_ref, kpe_ref, kcat_ref,
     convo_ref, hlast_ref, *rest) = rest[n_alias:]
    if per_head:
        vt_ref, buf_ref, hcar_ref, hb_ref, cqn_ref, v_scr = rest
    else:
        buf_ref, hcar_ref, hb_ref, cqn_ref = rest
    bb, t, d = x_ref.shape
    q_main = QK_NOPE if per_head else KV_RANK
    m = bb * t
    groups = t // SUBLANES
    gw = d // LRU_BLOCKS
    cw_ = MXU_DIM
    n_chunks = d // cw_
    blocks_per_chunk = cw_ // gw
    nq = N_HEADS * QK_NOPE
    npe = N_HEADS * QK_ROPE
    o_xa, o_ga, o_cq = 0, d, 2 * d
    o_ckv = o_cq + Q_RANK
    o_kpe = o_ckv + KV_RANK
    o_krot = o_kpe + LANES
    o_gb = o_krot + LANES
    o_ua = o_gb + d
    o_ub = o_ua + d

    @pl.when(pl.program_id(1) == 0)
    def _():
        buf_ref[:, 0:SUBLANES, :] = conv0_ref[...]
        hcar_ref[...] = h0_ref[...]

    x = x_ref[...]
    mod = mod_ref[...]
    shift = mod[:, :, 0:d]
    scale = mod[:, :, d:2 * d]
    h = _rms(x, pre_norm_ref[...]) * (1.0 + scale) + shift
    hb_ref[...] = h.reshape(m, d).astype(BF16)

    def proj(lo, width):
        return jnp.dot(hb_ref[...], w_in_ref[:, lo:lo + width], preferred_element_type=F32)

    cos = cos_ref[...][None]
    sin = sin_ref[...][None]

    def conv_phase(c):
        cs = slice(c * cw_, (c + 1) * cw_)
        xa = proj(o_xa + c * cw_, cw_)
        buf_ref[:, SUBLANES:SUBLANES + t, cs] = xa.reshape(bb, t, cw_)
        cw = conv_w_ref[:, cs]
        xc = conv_b_ref[:, cs] + buf_ref[:, SUBLANES - 3:SUBLANES - 3 + t, cs] * cw[0:1]
        xc = xc + buf_ref[:, SUBLANES - 2:SUBLANES - 2 + t, cs] * cw[1:2]
        xc = xc + buf_ref[:, SUBLANES - 1:SUBLANES - 1 + t, cs] * cw[2:3]
        xc = xc + buf_ref[:, SUBLANES:SUBLANES + t, cs] * cw[3:4]
        tail = buf_ref[:, t:t + SUBLANES, cs]
        convo_ref[:, :, cs] = tail
        buf_ref[:, 0:SUBLANES, cs] = tail
        return xc.reshape(m, cw_)

    def gate_phase(c, xc2):
        cs = slice(c * cw_, (c + 1) * cw_)
        xcb = xc2.astype(BF16)
        r_parts, i_parts = [], []
        for n in range(blocks_per_chunk):
            g = jnp.dot(xcb[:, n * gw:(n + 1) * gw], w_gate_ref[c * blocks_per_chunk + n],
                        preferred_element_type=F32)
            r_parts.append(g[:, 0:gw])
            i_parts.append(g[:, gw:2 * gw])
        r = _sigmoid(jnp.concatenate(r_parts, axis=1) + ba_ref[:, cs])
        ig = _sigmoid(jnp.concatenate(i_parts, axis=1) + bx_ref[:, cs])
        nl = -lam_ref[:, cs]
        softplus = jnp.maximum(nl, 0.0) + jnp.log1p(jnp.exp(-jnp.abs(nl)))
        th = jnp.tanh((-0.5 * LRU_C * r) * softplus)
        inv = 1.0 / (1.0 - th)
        a = (1.0 + th) * inv
        mult = (2.0 * _sqrt_nonneg(-th)) * inv
        return a, mult * (ig * xc2)

    def scan_phase(c, a, b):
        cs = slice(c * cw_, (c + 1) * cw_)
        a4 = a.reshape(bb * groups, SUBLANES, cw_)
        b4 = b.reshape(bb * groups, SUBLANES, cw_)
        row = lax.broadcasted_iota(jnp.int32, a4.shape, 1)
        for sft in (1, 2, 4):
            keep = row >= sft
            a_sh = jnp.where(keep, pltpu.roll(a4, sft, axis=1), 1.0)
            b_sh = jnp.where(keep, pltpu.roll(b4, sft, axis=1), 0.0)
            b4 = b4 + a4 * b_sh
            a4 = a4 * a_sh
        a5 = a4.reshape(bb, groups, SUBLANES, cw_)
        b5 = b4.reshape(bb, groups, SUBLANES, cw_)
        hc = hcar_ref[:, :, cs]
        outs = []
        for g in range(groups):
            hs = a5[:, g] * hc + b5[:, g]
            outs.append(hs)
            hc = hs[:, SUBLANES - 1:SUBLANES, :]
        hcar_ref[:, :, cs] = hc
        hlast_ref[:, :, cs] = hc
        return jnp.concatenate(outs, axis=1).reshape(m, cw_)

    def out_phase(c, y_lru):
        cs = slice(c * cw_, (c + 1) * cw_)
        ga = proj(o_ga + c * cw_, cw_)
        ya_ref[:, :, cs] = (y_lru * _silu(ga)).astype(BF16).reshape(bb, t, cw_)

    def gate_out(ref, fn, off, c):
        cs = slice(c * cw_, (c + 1) * cw_)
        ref[:, :, cs] = fn(proj(off + c * cw_, cw_)).astype(BF16).reshape(bb, t, cw_)

    def q_latent(heads):
        q_nope = jnp.dot(cqn_ref[...], w_q_ref[:, heads[0] * QK_NOPE:(heads[-1] + 1) * QK_NOPE],
                         preferred_element_type=F32)
        if per_head:
            width = len(heads) * QK_NOPE
            v_scr[:, 0:width] = q_nope
            for k, hd in enumerate(heads):
                q_ref[0, hd, 0, 0:QK_NOPE, :] = v_scr[:, k * QK_NOPE:(k + 1) * QK_NOPE].T.astype(BF16)
            return
        q_nope = q_nope.astype(BF16)
        for k, hd in enumerate(heads):
            q_h = jnp.dot(q_nope[:, k * QK_NOPE:(k + 1) * QK_NOPE], w_uk_ref[hd],
                          preferred_element_type=F32).astype(BF16)
            q_ref[:, hd, :, 0:q_main] = q_h.reshape(bb, t, q_main)

    def q_rotary():
        q_pe = jnp.dot(cqn_ref[...], w_q_ref[:, nq:nq + npe], preferred_element_type=F32)
        lane = lax.broadcasted_iota(jnp.int32, (1, 1, LANES), 2)
        half = QK_ROPE // 2
        low_half = (lane & (QK_ROPE - 1)) < half
        for pair in range(N_HEADS // 2):
            sl = slice(pair * LANES, (pair + 1) * LANES)
            x = q_pe[:, sl].reshape(bb, t, LANES)
            rot = jnp.where(low_half, -pltpu.roll(x, LANES - half, axis=2), pltpu.roll(x, half, axis=2))
            both = x * cos + rot * sin
            even = jnp.where(lane < QK_ROPE, both, 0.0)
            odd = jnp.where(lane < QK_ROPE, 0.0, both)
            rs = slice(q_main, q_main + LANES)
            if per_head:
                q_ref[0, 2 * pair, 0, rs, :] = even.reshape(m, LANES).T.astype(BF16)
                q_ref[0, 2 * pair + 1, 0, rs, :] = odd.reshape(m, LANES).T.astype(BF16)
            else:
                q_ref[:, 2 * pair, :, rs] = even.astype(BF16)
                q_ref[:, 2 * pair + 1, :, rs] = odd.astype(BF16)

    def key_latents():
        ckv = _rms(proj(o_ckv, KV_RANK), kvn_ref[...])
        ckv_ref[...] = ckv.reshape(bb, t, KV_RANK)
        ckvb = ckv.astype(BF16)
        kp = proj(o_kpe, LANES).reshape(bb, t, LANES)
        kr = proj(o_krot, LANES).reshape(bb, t, LANES)
        kroped = kp * cos + kr * sin
        kpe_ref[...] = kroped[:, :, 0:QK_ROPE]
        if per_head:
            nk = N_HEADS * QK_NOPE
            k_nope = jnp.dot(ckvb, w_uk_ref[:, 0:nk], preferred_element_type=F32).astype(BF16)
            v_scr[...] = jnp.dot(ckvb, w_uk_ref[:, nk:nk + N_HEADS * V_DIM], preferred_element_type=F32)
            v_t = v_scr[...].T
            for hd in range(N_HEADS):
                kcat_ref[:, hd, :, 0:QK_NOPE] = k_nope[:, hd * QK_NOPE:(hd + 1) * QK_NOPE].reshape(
                    bb, t, QK_NOPE)
                kcat_ref[:, hd, :, QK_NOPE:QK_NOPE + LANES] = kroped.astype(BF16)
                vt_ref[0, hd, 0] = v_t[hd * V_DIM:(hd + 1) * V_DIM].astype(BF16)
        else:
            kcat_ref[:, :, 0:KV_RANK] = ckvb.reshape(bb, t, KV_RANK)
            kcat_ref[:, :, KV_RANK:K_CAT] = kroped.astype(BF16)

    cqn_ref[...] = _rms(proj(o_cq, Q_RANK), qn_ref[...]).astype(BF16)
    other = [lambda: q_latent((0, 1, 2, 3)), lambda: q_latent((4, 5, 6, 7)), q_rotary, key_latents]
    for c in range(n_chunks):
        xc2 = conv_phase(c)
        gate_out(sgb_ref, _silu, o_gb, c)
        a, b = gate_phase(c, xc2)
        gate_out(sua_ref, _sigmoid, o_ua, c)
        y_lru = scan_phase(c, a, b)
        gate_out(sub_ref, _sigmoid, o_ub, c)
        out_phase(c, y_lru)
        if c < len(other):
            other[c]()
    for f in other[n_chunks:]:
        f()


def _inproj(x, mod, conv0, h0, cos, sin, lw, *, bb, t, per_head, layer, depth, stacked):
    b, s, d = x.shape
    grid = (b // bb, s // t)
    tok = lambda w: pl.BlockSpec((bb, t, w), lambda i, j: (i, j, 0))
    tok_l = lambda w: pl.BlockSpec((None, bb, t, w), lambda i, j: (layer, i, j, 0))
    per_b = lambda r, w: pl.BlockSpec((bb, r, w), lambda i, j: (i, 0, 0))
    heads = lambda w: pl.BlockSpec((bb, N_HEADS, t, w), lambda i, j: (i, 0, j, 0))
    q_w = (QK_NOPE if per_head else KV_RANK) + LANES
    weights = [lw["w_in"], lw["w_q"], lw["w_ukv"] if per_head else lw["w_uk"], lw["w_gate"],
               lw["pre_norm"], lw["conv_w"],
               lw["conv_b"], lw["lru_ba"], lw["lru_bx"], lw["lru_lambda"], lw["q_norm"], lw["kv_norm"]]
    in_specs = ([tok(d), per_b(1, 3 * d), per_b(SUBLANES, d), per_b(1, d),
                 pl.BlockSpec((t, LANES), lambda i, j: (j, 0)),
                 pl.BlockSpec((t, LANES), lambda i, j: (j, 0))]
                + [_const_spec(w.shape) for w in weights])
    q_shape = (b, N_HEADS, s // t, q_w, t) if per_head else (b, N_HEADS, s, q_w)
    q_spec = (pl.BlockSpec((1, N_HEADS, 1, q_w, t), lambda i, j: (i, 0, j, 0, 0)) if per_head
              else heads(q_w))
    out_shape = [jax.ShapeDtypeStruct((b, s, d), BF16)] * 4 + [
        jax.ShapeDtypeStruct(q_shape, BF16),
        jax.ShapeDtypeStruct((depth, b, s, KV_RANK), F32),
        jax.ShapeDtypeStruct((depth, b, s, QK_ROPE), F32),
        jax.ShapeDtypeStruct((b, N_HEADS, s, q_w) if per_head else (b, s, K_CAT), BF16),
        jax.ShapeDtypeStruct((b, SUBLANES, d), F32),
        jax.ShapeDtypeStruct((b, 1, d), F32),
    ]
    out_specs = [tok(d)] * 4 + [
        q_spec, tok_l(KV_RANK), tok_l(QK_ROPE), heads(q_w) if per_head else tok(K_CAT),
        per_b(SUBLANES, d), per_b(1, d),
    ]
    aliased = list(stacked)
    n_in = len(in_specs)
    in_specs = in_specs + [pl.BlockSpec(memory_space=pl.ANY)] * len(aliased)
    aliases = {n_in + k: 5 + k for k in range(len(aliased))}
    if per_head:
        assert bb == 1
        out_shape.append(jax.ShapeDtypeStruct((b, N_HEADS, s // t, V_DIM, t), BF16))
        out_specs.append(pl.BlockSpec((1, N_HEADS, 1, V_DIM, t), lambda i, j: (i, 0, j, 0, 0)))
    return pl.pallas_call(
        functools.partial(_inproj_kernel, per_head=per_head, n_alias=len(aliased)),
        grid=grid,
        in_specs=in_specs,
        out_specs=out_specs,
        out_shape=out_shape,
        input_output_aliases=aliases,
        scratch_shapes=[pltpu.VMEM((bb, t + SUBLANES, d), F32), pltpu.VMEM((bb, 1, d), F32),
                        pltpu.VMEM((bb * t, d), BF16), pltpu.VMEM((bb * t, Q_RANK), BF16)]
        + ([pltpu.VMEM((bb * t, N_HEADS * V_DIM), F32)] if per_head else []),
        compiler_params=pltpu.CompilerParams(
            dimension_semantics=("parallel", "arbitrary"), vmem_limit_bytes=VMEM_LIMIT_BYTES),
        name="inproj_lru",
    )(x, mod, conv0, h0, cos, sin, *weights, *aliased)


def _attn_heads_kernel(q_ref, k_ref, vt_ref, sgb_ref, bias_ref, yb_ref, m_scr, l_scr, acc_scr, s_scr,
                       *, tk):
    n_groups = q_ref.shape[1]
    q_gran = q_ref.shape[-1]
    n_sub = q_ref.shape[2] * q_gran // tk
    gran = vt_ref.shape[-1]

    m_scr[...] = jnp.full(m_scr.shape, NEG_INF, F32)
    l_scr[...] = jnp.zeros(l_scr.shape, F32)
    acc_scr[...] = jnp.zeros(acc_scr.shape, F32)

    def qk(u, j, g):
        k = k_ref[0, g, pl.ds(pl.multiple_of(j * tk, tk), tk), :]
        q_t = jnp.concatenate([q_ref[0, g, u * (tk // q_gran) + i] for i in range(tk // q_gran)], axis=1)
        return jnp.dot(k, q_t, preferred_element_type=F32)

    def step(u, j, masked, last_tile):
        for g in range(n_groups):
            if g + 1 < n_groups:
                s_scr[g + 1] = qk(u, j, g + 1)
            elif not last_tile:
                s_scr[0] = qk(u, j + 1, 0)
            s = s_scr[g]
            if masked:
                s = s + bias_ref[...]
            m_prev = m_scr[u, g]
            m_next = jnp.maximum(m_prev, jnp.max(s, axis=0, keepdims=True))
            p = jnp.exp2(s - m_next)
            alpha = jnp.exp2(m_prev - m_next)
            l_scr[u, g] = alpha * l_scr[u, g] + jnp.sum(p, axis=0, keepdims=True)
            m_scr[u, g] = m_next
            vt = jnp.concatenate([vt_ref[0, g, j * (tk // gran) + i] for i in range(tk // gran)], axis=1)
            pv = jnp.dot(vt, p.astype(BF16), preferred_element_type=F32)
            acc_scr[u, g] = acc_scr[u, g] * alpha + pv

    s_scr[0] = qk(0, 0, 0)
    for u in range(n_sub):
        last = pl.program_id(2) * n_sub + u

        def pair(jj, c, u=u):
            step(u, 2 * jj, False, False)
            step(u, 2 * jj + 1, False, False)
            return c

        lax.fori_loop(0, last // 2, pair, 0)
        if n_sub % 2 == 0:
            if u % 2 == 1:
                step(u, last - 1, False, False)
        else:
            @pl.when(last % 2 == 1)
            def _(u=u, last=last):
                step(u, last - 1, False, False)
        step(u, last, True, True)
        if u + 1 < n_sub:
            s_scr[0] = qk(u + 1, 0, 0)
        for g in range(n_groups):
            o = (acc_scr[u, g] * (1.0 / l_scr[u, g])).T
            sl = slice(g * V_DIM, (g + 1) * V_DIM)
            yb_ref[0, u * tk:(u + 1) * tk, sl] = (o * sgb_ref[0, u * tk:(u + 1) * tk, sl]).astype(BF16)


def _attention_heads(q, k, vt, sgb, *, tq, tk, heads_per_step=4):
    b, n_heads, q_grans, qw, q_gran = q.shape
    sq = q_grans * q_gran
    sk = k.shape[2]
    d = sgb.shape[-1]
    gran = vt.shape[-1]
    hps = heads_per_step
    n_sub = max(n for n in (4, 2, 1) if sq % (n * tq) == 0)
    tqs = n_sub * tq
    assert n_heads % hps == 0 and sq % tqs == 0 and sk % tk == 0 and tk % gran == 0 and tq % q_gran == 0
    assert vt.shape[2] * gran == sk and (hps * V_DIM) % LANES == 0
    assert tk == tq and tq % CHUNK == 0 and sq == sk
    k_idx = lax.broadcasted_iota(jnp.int32, (tk, tq), 0)
    q_idx = lax.broadcasted_iota(jnp.int32, (tk, tq), 1)
    diag_bias = jnp.where(k_idx // CHUNK <= q_idx // CHUNK, 0.0, NEG_INF).astype(F32)
    kern = functools.partial(_attn_heads_kernel, tk=tk)
    return pl.pallas_call(
        kern,
        grid=(b, n_heads // hps, sq // tqs),
        in_specs=[pl.BlockSpec((1, hps, tqs // q_gran, qw, q_gran), lambda i, h, j: (i, h, j, 0, 0)),
                  pl.BlockSpec((1, hps, sk, qw), lambda i, h, j: (i, h, 0, 0)),
                  pl.BlockSpec((1, hps) + vt.shape[2:], lambda i, h, j: (i, h, 0, 0, 0)),
                  pl.BlockSpec((1, tqs, hps * V_DIM), lambda i, h, j: (i, j, h)),
                  _const_spec((tk, tq))],
        out_specs=pl.BlockSpec((1, tqs, hps * V_DIM), lambda i, h, j: (i, j, h)),
        out_shape=jax.ShapeDtypeStruct((b, sq, d), BF16),
        scratch_shapes=[pltpu.VMEM((n_sub, hps, 1, tq), F32), pltpu.VMEM((n_sub, hps, 1, tq), F32),
                        pltpu.VMEM((n_sub, hps, V_DIM, tq), F32), pltpu.VMEM((hps, tk, tq), F32)],
        compiler_params=pltpu.CompilerParams(
            dimension_semantics=("parallel", "parallel", "arbitrary"),
            vmem_limit_bytes=VMEM_LIMIT_BYTES),
        name="mla_attention_heads",
    )(q, k, vt, sgb, diag_bias)


def _attn_cached_kernel(q_ref, ckv_ref, kpe_ref, knew_ref, sgb_ref, w_uv_ref, yb_ref,
                        m_scr, l_scr, acc_scr, *, past_len, sq, tk):
    m = N_HEADS * sq
    q = q_ref[0].reshape(m, K_CAT)
    q_lat = q[:, 0:KV_RANK]
    q_rope = q[:, KV_RANK:K_CAT].astype(F32)
    q_rope = (q_rope[:, 0:QK_ROPE] + q_rope[:, QK_ROPE:LANES]).astype(BF16)
    q_pos = past_len + (lax.broadcasted_iota(jnp.int32, (m, 1), 0) & (sq - 1))
    k_lim = (q_pos // CHUNK + 1) * CHUNK

    m_scr[...] = jnp.full(m_scr.shape, NEG_INF, F32)
    l_scr[...] = jnp.zeros(l_scr.shape, F32)
    acc_scr[...] = jnp.zeros(acc_scr.shape, F32)
    nt = (((1,), (1,)), ((), ()))

    def update(s, visible, v):
        s = jnp.where(visible, s, NEG_INF)
        m_prev = m_scr[...]
        m_next = jnp.maximum(m_prev, jnp.max(s, axis=-1, keepdims=True))
        p = jnp.exp2(s - jnp.concatenate([m_next] * (s.shape[1] // LANES), axis=1))
        alpha = jnp.exp2(m_prev - m_next)
        l_scr[...] = alpha * l_scr[...] + jnp.sum(p, axis=-1, keepdims=True)
        m_scr[...] = m_next
        pv = jnp.dot(p.astype(BF16), v, preferred_element_type=F32)
        acc_scr[...] = acc_scr[...] * jnp.concatenate([alpha] * (KV_RANK // LANES), axis=1) + pv

    def past_body(j, c):
        off = pl.multiple_of(j * tk, tk)
        lat = ckv_ref[0, 0, pl.ds(off, tk), :].astype(BF16)
        pe = kpe_ref[0, 0, pl.ds(off, tk), :].astype(BF16)
        s = (lax.dot_general(q_lat, lat, nt, preferred_element_type=F32)
             + lax.dot_general(q_rope, pe, nt, preferred_element_type=F32))
        k_pos = off + lax.broadcasted_iota(jnp.int32, (m, tk), 1)
        update(s, k_pos < k_lim, lat)
        return c

    lax.fori_loop(0, past_len // tk, past_body, 0)

    kn = knew_ref[0]
    s = lax.dot_general(q, kn, nt, preferred_element_type=F32)
    idx = lax.broadcasted_iota(jnp.int32, (m, kn.shape[0]), 1)
    update(s, jnp.where(idx < sq, past_len + idx, k_lim) < k_lim, kn[:, 0:KV_RANK])

    inv_l = 1.0 / l_scr[...]
    o = (acc_scr[...] * jnp.concatenate([inv_l] * (KV_RANK // LANES), axis=1)).astype(BF16)
    sgb = sgb_ref[0]
    for hd in range(N_HEADS):
        att = jnp.dot(o[hd * sq:(hd + 1) * sq], w_uv_ref[hd], preferred_element_type=F32)
        sl = slice(hd * V_DIM, (hd + 1) * V_DIM)
        yb_ref[0, :, sl] = (att * sgb[:, sl]).astype(BF16)


def _attention_cached(q, cache_ckv, cache_kpe, layer, knew, sgb, w_uv, *, tk):
    b, _, sq, _ = q.shape
    past_len = cache_ckv.shape[2]
    d = sgb.shape[-1]
    m = N_HEADS * sq
    assert sq & (sq - 1) == 0 and past_len % tk == 0 and tk % LANES == 0 and knew.shape[1] % LANES == 0
    kern = functools.partial(_attn_cached_kernel, past_len=past_len, sq=sq, tk=tk)
    return pl.pallas_call(
        kern,
        grid=(b,),
        in_specs=[pl.BlockSpec((1, N_HEADS, sq, K_CAT), lambda i: (i, 0, 0, 0)),
                  pl.BlockSpec((1, 1, past_len, KV_RANK), lambda i: (layer, i, 0, 0)),
                  pl.BlockSpec((1, 1, past_len, QK_ROPE), lambda i: (layer, i, 0, 0)),
                  pl.BlockSpec((1,) + knew.shape[1:], lambda i: (i, 0, 0)),
                  pl.BlockSpec((1, sq, d), lambda i: (i, 0, 0)),
                  _const_spec(w_uv.shape)],
        out_specs=pl.BlockSpec((1, sq, d), lambda i: (i, 0, 0)),
        out_shape=jax.ShapeDtypeStruct((b, sq, d), BF16),
        scratch_shapes=[pltpu.VMEM((m, LANES), F32), pltpu.VMEM((m, LANES), F32),
                        pltpu.VMEM((m, KV_RANK), F32)],
        compiler_params=pltpu.CompilerParams(
            dimension_semantics=("parallel",), vmem_limit_bytes=VMEM_LIMIT_BYTES),
        name="mla_attention_cached",
    )(q, cache_ckv, cache_kpe, knew, sgb, w_uv)


def _outproj_kernel(x_ref, mod_ref, ya_ref, yb_ref, sua_ref, sub_ref,
                    w_a_ref, w_b_ref, w_o_ref, post_norm_ref, y_ref):
    bb, t, d = x_ref.shape
    m = bb * t
    ya = ya_ref[...].reshape(m, d)
    yb = yb_ref[...].reshape(m, d)
    pa = jnp.dot(ya, w_a_ref[...], preferred_element_type=F32)
    pb = jnp.dot(yb, w_b_ref[...], preferred_element_type=F32)
    merged = sua_ref[...].reshape(m, d).astype(F32) * pa + sub_ref[...].reshape(m, d).astype(F32) * pb
    o = jnp.dot(merged.astype(BF16), w_o_ref[...], preferred_element_type=F32)
    gate = mod_ref[...][:, :, 2 * d:3 * d]
    y_ref[...] = x_ref[...] + gate * _rms(o, post_norm_ref[...]).reshape(bb, t, d)


def _outproj(x, mod, ya, yb, sua, sub, lw, *, bb, t):
    b, s, d = x.shape
    tok = pl.BlockSpec((bb, t, d), lambda i, j: (i, j, 0))
    weights = [lw["w_branch_a"], lw["w_branch_b"], lw["w_out"], lw["post_norm"]]
    return pl.pallas_call(
        _outproj_kernel,
        grid=(b // bb, s // t),
        in_specs=[tok, pl.BlockSpec((bb, 1, 3 * d), lambda i, j: (i, 0, 0)), tok, tok, tok, tok]
        + [_const_spec(w.shape) for w in weights],
        out_specs=tok,
        out_shape=jax.ShapeDtypeStruct((b, s, d), F32),
        compiler_params=pltpu.CompilerParams(
            dimension_semantics=("parallel", "parallel"), vmem_limit_bytes=VMEM_LIMIT_BYTES),
        name="outproj",
    )(x, mod, ya, yb, sua, sub, *weights)


def _rot_half(w):
    half = w.shape[-1] // 2
    return jnp.concatenate([-w[..., half:], w[..., :half]], axis=-1)


def _twice(w):
    return jnp.concatenate([w, w], axis=-1)


def _layer_weights(l, p):
    d = p["w_in"].shape[1]
    w_in = p["w_in"][l].astype(BF16)
    o = 0
    seg = {}
    for name, width in (("xa", d), ("ga", d), ("cq", Q_RANK), ("ckv", KV_RANK), ("kpe", QK_ROPE),
                        ("gb", d), ("ua", d), ("ub", d)):
        seg[name] = w_in[:, o:o + width]
        o += width
    w_in_ext = jnp.concatenate(
        [seg["xa"], seg["ga"], seg["cq"], seg["ckv"], _twice(seg["kpe"]),
         _twice(_rot_half(seg["kpe"])), seg["gb"], seg["ua"], seg["ub"]], axis=1)
    wq = p["w_q_up"][l].reshape(Q_RANK, N_HEADS, QK_NOPE + QK_ROPE)
    wq_pe = wq[:, :, QK_NOPE:]
    w_q = jnp.concatenate(
        [wq[:, :, :QK_NOPE].reshape(Q_RANK, -1), wq_pe.reshape(Q_RANK, -1)], axis=1)
    row = lambda v: v.reshape(1, -1)
    return {
        "w_in": w_in_ext,
        "w_q": w_q.astype(BF16),
        "w_uk": jnp.transpose(p["w_uk"][l], (1, 2, 0)).astype(BF16),
        "w_ukv": jnp.concatenate([p["w_uk"][l].reshape(KV_RANK, -1), p["w_uv"][l].reshape(KV_RANK, -1)],
                                 axis=1).astype(BF16),
        "w_uv": jnp.transpose(p["w_uv"][l], (1, 0, 2)).astype(BF16),
        "w_gate": jnp.concatenate([p["lru_wa"][l], p["lru_wx"][l]], axis=-1).astype(BF16),
        "pre_norm": row(p["pre_norm"][l]), "post_norm": row(p["post_norm"][l]),
        "conv_w": p["conv_w"][l], "conv_b": row(p["conv_b"][l]),
        "lru_ba": row(p["lru_ba"][l]), "lru_bx": row(p["lru_bx"][l]),
        "lru_lambda": row(p["lru_lambda"][l]),
        "q_norm": row(p["q_norm"][l]) * Q_SCALE, "kv_norm": row(p["kv_norm"][l]),
        "w_branch_a": p["w_branch_a"][l].astype(BF16), "w_branch_b": p["w_branch_b"][l].astype(BF16),
        "w_out": p["w_out"][l].astype(BF16),
    }


def _rope_tables(pos):
    half = QK_ROPE // 2
    inv = ROPE_THETA ** (-jnp.arange(half, dtype=F32) / half)
    ang = pos.astype(F32)[:, None] * inv[None, :]
    cos, sin = jnp.cos(ang), jnp.sin(ang)
    return jnp.concatenate([cos] * (LANES // half), axis=1), jnp.concatenate([sin] * (LANES // half), axis=1)


def _pick_tile(n, target):
    t = min(n, target)
    while n % t:
        t //= 2
    return t


def kernel(x_prompt, x_sample, c_prompt, c_sample, cache_ckv, cache_kpe, state_conv, state_lru, ada_w, ada_b, pre_norm, post_norm, w_in, conv_w, conv_b, lru_wa, lru_ba, lru_wx, lru_bx, lru_lambda, q_norm, w_q_up, kv_norm, w_uk, w_uv, w_branch_a, w_branch_b, w_out):
    p = dict(ada_w=ada_w, ada_b=ada_b, pre_norm=pre_norm, post_norm=post_norm, w_in=w_in,
             conv_w=conv_w, conv_b=conv_b, lru_wa=lru_wa, lru_ba=lru_ba, lru_wx=lru_wx, lru_bx=lru_bx,
             lru_lambda=lru_lambda, q_norm=q_norm, w_q_up=w_q_up, kv_norm=kv_norm, w_uk=w_uk,
             w_uv=w_uv, w_branch_a=w_branch_a, w_branch_b=w_branch_b, w_out=w_out)
    depth = w_in.shape[0]
    b_p, s_p, d = x_prompt.shape
    b_s, s_s, _ = x_sample.shape
    past_len = cache_ckv.shape[2]
    assert s_p % SUBLANES == 0 and s_s % SUBLANES == 0 and s_p >= SUBLANES and s_s >= SUBLANES

    mods = _modulation(jnp.concatenate([c_prompt, c_sample], axis=0), ada_w, ada_b)
    cos_p, sin_p = _rope_tables(jnp.arange(s_p, dtype=jnp.int32))
    cos_s, sin_s = _rope_tables(past_len + jnp.arange(s_s, dtype=jnp.int32))
    conv0_p = jnp.zeros((b_p, SUBLANES, d), F32)
    h0_p = jnp.zeros((b_p, 1, d), F32)

    t_in_p = _pick_tile(s_p, T_INPROJ)
    t_out_p = _pick_tile(s_p, T_OUTPROJ)
    tq_p = tk_p = _pick_tile(s_p, T_ATTN)
    tk_s = _pick_tile(past_len, T_CACHED)

    yp, ys = x_prompt, x_sample
    outs_p, outs_s = [], []
    lat_p = (jnp.zeros((depth, b_p, s_p, KV_RANK), F32), jnp.zeros((depth, b_p, s_p, QK_ROPE), F32))
    lat_s = (jnp.zeros((depth, b_s, s_s, KV_RANK), F32), jnp.zeros((depth, b_s, s_s, QK_ROPE), F32))
    for l in range(depth):
        lw = _layer_weights(l, p)

        mod_p = mods[l, :b_p, None, :]
        ya, sgb, sua, sub, q, ckv, kpe, k_heads, convo, hlast, vt = _inproj(
            yp, mod_p, conv0_p, h0_p, cos_p, sin_p, lw, bb=1, t=t_in_p, per_head=True,
            layer=l, depth=depth, stacked=lat_p)
        lat_p = (ckv, kpe)
        yb = _attention_heads(q, k_heads, vt, sgb, tq=tq_p, tk=tk_p)
        yp = _outproj(yp, mod_p, ya, yb, sua, sub, lw, bb=1, t=t_out_p)
        outs_p.append((convo[:, SUBLANES - (CONV_W - 1):, :], hlast[:, 0, :]))

        mod_s = mods[l, b_p:, None, :]
        conv0_s = jnp.pad(state_conv[l], ((0, 0), (SUBLANES - (CONV_W - 1), 0), (0, 0)))
        ya, sgb, sua, sub, q, ckv, kpe, kcat, convo, hlast = _inproj(
            ys, mod_s, conv0_s, state_lru[l][:, None, :], cos_s, sin_s, lw, bb=b_s, t=s_s,
            per_head=False, layer=l, depth=depth, stacked=lat_s)
        lat_s = (ckv, kpe)
        knew = jnp.pad(kcat, ((0, 0), (0, -s_s % LANES), (0, 0)))
        yb = _attention_cached(q, cache_ckv, cache_kpe, l, knew, sgb, lw["w_uv"], tk=tk_s)
        ys = _outproj(ys, mod_s, ya, yb, sua, sub, lw, bb=b_s, t=s_s)
        outs_s.append((convo[:, SUBLANES - (CONV_W - 1):, :], hlast[:, 0, :]))

    stack = lambda outs, k: jnp.stack([o[k] for o in outs])
    return (yp, ys,
            lat_p[0], lat_p[1], stack(outs_p, 0), stack(outs_p, 1),
            lat_s[0], lat_s[1], stack(outs_s, 0), stack(outs_s, 1))
```

```python
import functools
import math

import jax
import jax.numpy as jnp
from jax import lax
from jax.experimental import pallas as pl
from jax.experimental.pallas import tpu as pltpu

F32 = jnp.float32
BF16 = jnp.bfloat16

EPS = 1e-6
NEG_INF = -1e30
CHUNK = 64
LRU_C = 8.0
ROPE_THETA = 10000.0

LANES = 128
SUBLANES = 8
MXU_DIM = 256
VMEM_LIMIT_BYTES = 56 * 1024 * 1024

T_INPROJ = MXU_DIM
T_OUTPROJ = 4 * MXU_DIM
T_ATTN = 2 * MXU_DIM
T_CACHED = 8 * MXU_DIM

CONV_W = 4
N_HEADS = 8
QK_NOPE = 128
QK_ROPE = 64
V_DIM = 128
KV_RANK = 256
Q_RANK = 768
LRU_BLOCKS = 8
K_CAT = KV_RANK + LANES
Q_SCALE = (QK_NOPE + QK_ROPE) ** -0.5 * math.log2(math.e)


def _sigmoid(x):
    return jax.nn.sigmoid(x)


def _silu(x):
    return x * _sigmoid(x)


def _sqrt_nonneg(x):
    return jnp.where(x > 0.0, x * lax.rsqrt(x), 0.0)


def _rms(x, g):
    return x * lax.rsqrt(jnp.mean(x * x, axis=-1, keepdims=True) + EPS) * g


def _const_spec(shape):
    nd = len(shape)
    return pl.BlockSpec(shape, lambda *_: (0,) * nd, pipeline_mode=pl.Buffered(1))


def _mod_kernel(c_ref, w_ref, b_ref, o_ref):
    c = c_ref[...]
    o_ref[0] = jnp.dot(_silu(c), w_ref[0], preferred_element_type=F32,
                       precision=lax.Precision.HIGHEST) + b_ref[0]


def _modulation(c_all, ada_w, ada_b):
    depth, d, d3 = ada_w.shape
    n = c_all.shape[0]
    nblk = d3 // d
    return pl.pallas_call(
        _mod_kernel,
        grid=(depth, nblk),
        in_specs=[pl.BlockSpec((n, d), lambda l, j: (0, 0)),
                  pl.BlockSpec((1, d, d), lambda l, j: (l, 0, j)),
                  pl.BlockSpec((1, 1, d), lambda l, j: (l, 0, j))],
        out_specs=pl.BlockSpec((1, n, d), lambda l, j: (l, 0, j)),
        out_shape=jax.ShapeDtypeStruct((depth, n, d3), F32),
        name="adaln_mod",
    )(c_all, ada_w, ada_b.reshape(depth, 1, d3))


def _inproj_kernel(x_ref, mod_ref, conv0_ref, h0_ref, cos_ref, sin_ref,
                   w_in_ref, w_q_ref, w_uk_ref, w_gate_ref,
                   pre_norm_ref, conv_w_ref, conv_b_ref, ba_ref, bx_ref, lam_ref, qn_ref, kvn_ref,
                   *rest, per_head, n_alias):
    (ya_ref, sgb_ref, sua_ref, sub_ref, q_ref, ckv_ref, kpe_ref, kcat_ref,
     convo_ref, hlast_ref, *rest) = rest[n_alias:]
    if per_head:
        vt_ref, buf_ref, hcar_ref, hb_ref, cqn_ref, v_scr = rest
    else:
        buf_ref, hcar_ref, hb_ref, cqn_ref = rest
    bb, t, d = x_ref.shape
    q_main = QK_NOPE if per_head else KV_RANK
    m = bb * t
    groups = t // SUBLANES
    gw = d // LRU_BLOCKS
    cw_ = MXU_DIM
    n_chunks = d // cw_
    blocks_per_chunk = cw_ // gw
    nq = N_HEADS * QK_NOPE
    npe = N_HEADS * QK_ROPE
    o_xa, o_ga, o_cq = 0, d, 2 * d
    o_ckv = o_cq + Q_RANK
    o_kpe = o_ckv + KV_RANK
    o_krot = o_kpe + LANES
    o_gb = o_krot + LANES
    o_ua = o_gb + d
    o_ub = o_ua + d

    @pl.when(pl.program_id(1) == 0)
    def _():
        buf_ref[:, 0:SUBLANES, :] = conv0_ref[...]
        hcar_ref[...] = h0_ref[...]

    x = x_ref[...]
    mod = mod_ref[...]
    shift = mod[:, :, 0:d]
    scale = mod[:, :, d:2 * d]
    h = _rms(x, pre_norm_ref[...]) * (1.0 + scale) + shift
    hb_ref[...] = h.reshape(m, d).astype(BF16)

    def proj(lo, width):
        return jnp.dot(hb_ref[...], w_in_ref[:, lo:lo + width], preferred_element_type=F32)

    cos = cos_ref[...][None]
    sin = sin_ref[...][None]

    def conv_phase(c):
        cs = slice(c * cw_, (c + 1) * cw_)
        xa = proj(o_xa + c * cw_, cw_)
        buf_ref[:, SUBLANES:SUBLANES + t, cs] = xa.reshape(bb, t, cw_)
        cw = conv_w_ref[:, cs]
        xc = conv_b_ref[:, cs] + buf_ref[:, SUBLANES - 3:SUBLANES - 3 + t, cs] * cw[0:1]
        xc = xc + buf_ref[:, SUBLANES - 2:SUBLANES - 2 + t, cs] * cw[1:2]
        xc = xc + buf_ref[:, SUBLANES - 1:SUBLANES - 1 + t, cs] * cw[2:3]
        xc = xc + buf_ref[:, SUBLANES:SUBLANES + t, cs] * cw[3:4]
        tail = buf_ref[:, t:t + SUBLANES, cs]
        convo_ref[:, :, cs] = tail
        buf_ref[:, 0:SUBLANES, cs] = tail
        return xc.reshape(m, cw_)

    def gate_phase(c, xc2):
        cs = slice(c * cw_, (c + 1) * cw_)
        xcb = xc2.astype(BF16)
        r_parts, i_parts = [], []
        for n in range(blocks_per_chunk):
            g = jnp.dot(xcb[:, n * gw:(n + 1) * gw], w_gate_ref[c * blocks_per_chunk + n],
                        preferred_element_type=F32)
            r_parts.append(g[:, 0:gw])
            i_parts.append(g[:, gw:2 * gw])
        r = _sigmoid(jnp.concatenate(r_parts, axis=1) + ba_ref[:, cs])
        ig = _sigmoid(jnp.concatenate(i_parts, axis=1) + bx_ref[:, cs])
        nl = -lam_ref[:, cs]
        softplus = jnp.maximum(nl, 0.0) + jnp.log1p(jnp.exp(-jnp.abs(nl)))
        th = jnp.tanh((-0.5 * LRU_C * r) * softplus)
        inv = 1.0 / (1.0 - th)
        a = (1.0 + th) * inv
        mult = (2.0 * _sqrt_nonneg(-th)) * inv
        return a, mult * (ig * xc2)

    def scan_phase(c, a, b):
        cs = slice(c * cw_, (c + 1) * cw_)
        a4 = a.reshape(bb * groups, SUBLANES, cw_)
        b4 = b.reshape(bb * groups, SUBLANES, cw_)
        row = lax.broadcasted_iota(jnp.int32, a4.shape, 1)
        for sft in (1, 2, 4):
            keep = row >= sft
            a_sh = jnp.where(keep, pltpu.roll(a4, sft, axis=1), 1.0)
            b_sh = jnp.where(keep, pltpu.roll(b4, sft, axis=1), 0.0)
            b4 = b4 + a4 * b_sh
            a4 = a4 * a_sh
        a5 = a4.reshape(bb, groups, SUBLANES, cw_)
        b5 = b4.reshape(bb, groups, SUBLANES, cw_)
        hc = hcar_ref[:, :, cs]
        outs = []
        for g in range(groups):
            hs = a5[:, g] * hc + b5[:, g]
            outs.append(hs)
            hc = hs[:, SUBLANES - 1:SUBLANES, :]
        hcar_ref[:, :, cs] = hc
        hlast_ref[:, :, cs] = hc
        return jnp.concatenate(outs, axis=1).reshape(m, cw_)

    def out_phase(c, y_lru):
        cs = slice(c * cw_, (c + 1) * cw_)
        ga = proj(o_ga + c * cw_, cw_)
        ya_ref[:, :, cs] = (y_lru * _silu(ga)).astype(BF16).reshape(bb, t, cw_)

    def gate_out(ref, fn, off, c):
        cs = slice(c * cw_, (c + 1) * cw_)
        ref[:, :, cs] = fn(proj(off + c * cw_, cw_)).astype(BF16).reshape(bb, t, cw_)

    def q_latent(heads):
        q_nope = jnp.dot(cqn_ref[...], w_q_ref[:, heads[0] * QK_NOPE:(heads[-1] + 1) * QK_NOPE],
                         preferred_element_type=F32)
        if per_head:
            width = len(heads) * QK_NOPE
            v_scr[:, 0:width] = q_nope
            for k, hd in enumerate(heads):
                q_ref[0, hd, 0, 0:QK_NOPE, :] = v_scr[:, k * QK_NOPE:(k + 1) * QK_NOPE].T.astype(BF16)
            return
        q_nope = q_nope.astype(BF16)
        for k, hd in enumerate(heads):
            q_h = jnp.dot(q_nope[:, k * QK_NOPE:(k + 1) * QK_NOPE], w_uk_ref[hd],
                          preferred_element_type=F32).astype(BF16)
            q_ref[:, hd, :, 0:q_main] = q_h.reshape(bb, t, q_main)

    def q_rotary():
        q_pe = jnp.dot(cqn_ref[...], w_q_ref[:, nq:nq + npe], preferred_element_type=F32)
        lane = lax.broadcasted_iota(jnp.int32, (1, 1, LANES), 2)
        half = QK_ROPE // 2
        low_half = (lane & (QK_ROPE - 1)) < half
        for pair in range(N_HEADS // 2):
            sl = slice(pair * LANES, (pair + 1) * LANES)
            x = q_pe[:, sl].reshape(bb, t, LANES)
            rot = jnp.where(low_half, -pltpu.roll(x, LANES - half, axis=2), pltpu.roll(x, half, axis=2))
            both = x * cos + rot * sin
            even = jnp.where(lane < QK_ROPE, both, 0.0)
            odd = jnp.where(lane < QK_ROPE, 0.0, both)
            rs = slice(q_main, q_main + LANES)
            if per_head:
                q_ref[0, 2 * pair, 0, rs, :] = even.reshape(m, LANES).T.astype(BF16)
                q_ref[0, 2 * pair + 1, 0, rs, :] = odd.reshape(m, LANES).T.astype(BF16)
            else:
                q_ref[:, 2 * pair, :, rs] = even.astype(BF16)
                q_ref[:, 2 * pair + 1, :, rs] = odd.astype(BF16)

    def store_stacked(ref, val):
        if len(ref.shape) == val.ndim:
            ref[...] = val
        else:
            for slab in range(ref.shape[0]):
                ref[slab] = val

    def key_latents():
        ckv = _rms(proj(o_ckv, KV_RANK), kvn_ref[...])
        store_stacked(ckv_ref, ckv.reshape(bb, t, KV_RANK))
        ckvb = ckv.astype(BF16)
        kp = proj(o_kpe, LANES).reshape(bb, t, LANES)
        kr = proj(o_krot, LANES).reshape(bb, t, LANES)
        kroped = kp * cos + kr * sin
        store_stacked(kpe_ref, kroped[:, :, 0:QK_ROPE])
        if per_head:
            nk = N_HEADS * QK_NOPE
            k_nope = jnp.dot(ckvb, w_uk_ref[:, 0:nk], preferred_element_type=F32).astype(BF16)
            v_scr[...] = jnp.dot(ckvb, w_uk_ref[:, nk:nk + N_HEADS * V_DIM], preferred_element_type=F32)
            v_t = v_scr[...].T
            for hd in range(N_HEADS):
                kcat_ref[:, hd, :, 0:QK_NOPE] = k_nope[:, hd * QK_NOPE:(hd + 1) * QK_NOPE].reshape(
                    bb, t, QK_NOPE)
                kcat_ref[:, hd, :, QK_NOPE:QK_NOPE + LANES] = kroped.astype(BF16)
                vt_ref[0, hd, 0] = v_t[hd * V_DIM:(hd + 1) * V_DIM].astype(BF16)
        else:
            kcat_ref[:, :, 0:KV_RANK] = ckvb.reshape(bb, t, KV_RANK)
            kcat_ref[:, :, KV_RANK:K_CAT] = kroped.astype(BF16)

    cqn_ref[...] = _rms(proj(o_cq, Q_RANK), qn_ref[...]).astype(BF16)
    other = [lambda: q_latent((0, 1, 2, 3)), lambda: q_latent((4, 5, 6, 7)), q_rotary, key_latents]
    for c in range(n_chunks):
        xc2 = conv_phase(c)
        gate_out(sgb_ref, _silu, o_gb, c)
        a, b = gate_phase(c, xc2)
        gate_out(sua_ref, _sigmoid, o_ua, c)
        y_lru = scan_phase(c, a, b)
        gate_out(sub_ref, _sigmoid, o_ub, c)
        out_phase(c, y_lru)
        if c < len(other):
            other[c]()
    for f in other[n_chunks:]:
        f()


def _inproj(x, mod, conv0, h0, cos, sin, lw, *, bb, t, per_head, layer, depth, stacked):
    b, s, d = x.shape
    grid = (b // bb, s // t)
    tok = lambda w: pl.BlockSpec((bb, t, w), lambda i, j: (i, j, 0))
    if stacked is None:
        tok_l = lambda w: pl.BlockSpec((depth, bb, t, w), lambda i, j: (0, i, j, 0))
    else:
        tok_l = lambda w: pl.BlockSpec((None, bb, t, w), lambda i, j: (layer, i, j, 0))
    per_b = lambda r, w: pl.BlockSpec((bb, r, w), lambda i, j: (i, 0, 0))
    heads = lambda w: pl.BlockSpec((bb, N_HEADS, t, w), lambda i, j: (i, 0, j, 0))
    q_w = (QK_NOPE if per_head else KV_RANK) + LANES
    weights = [lw["w_in"], lw["w_q"], lw["w_ukv"] if per_head else lw["w_uk"], lw["w_gate"],
               lw["pre_norm"], lw["conv_w"],
               lw["conv_b"], lw["lru_ba"], lw["lru_bx"], lw["lru_lambda"], lw["q_norm"], lw["kv_norm"]]
    in_specs = ([tok(d), per_b(1, 3 * d), per_b(SUBLANES, d), per_b(1, d),
                 pl.BlockSpec((t, LANES), lambda i, j: (j, 0)),
                 pl.BlockSpec((t, LANES), lambda i, j: (j, 0))]
                + [_const_spec(w.shape) for w in weights])
    q_shape = (b, N_HEADS, s // t, q_w, t) if per_head else (b, N_HEADS, s, q_w)
    q_spec = (pl.BlockSpec((1, N_HEADS, 1, q_w, t), lambda i, j: (i, 0, j, 0, 0)) if per_head
              else heads(q_w))
    out_shape = [jax.ShapeDtypeStruct((b, s, d), BF16)] * 4 + [
        jax.ShapeDtypeStruct(q_shape, BF16),
        jax.ShapeDtypeStruct((depth, b, s, KV_RANK), F32),
        jax.ShapeDtypeStruct((depth, b, s, QK_ROPE), F32),
        jax.ShapeDtypeStruct((b, N_HEADS, s, q_w) if per_head else (b, s, K_CAT), BF16),
        jax.ShapeDtypeStruct((b, SUBLANES, d), F32),
        jax.ShapeDtypeStruct((b, 1, d), F32),
    ]
    out_specs = [tok(d)] * 4 + [
        q_spec, tok_l(KV_RANK), tok_l(QK_ROPE), heads(q_w) if per_head else tok(K_CAT),
        per_b(SUBLANES, d), per_b(1, d),
    ]
    aliased = [] if stacked is None else list(stacked)
    n_in = len(in_specs)
    in_specs = in_specs + [pl.BlockSpec(memory_space=pl.ANY)] * len(aliased)
    aliases = {n_in + k: 5 + k for k in range(len(aliased))}
    if per_head:
        assert bb == 1
        out_shape.append(jax.ShapeDtypeStruct((b, N_HEADS, s // t, V_DIM, t), BF16))
        out_specs.append(pl.BlockSpec((1, N_HEADS, 1, V_DIM, t), lambda i, j: (i, 0, j, 0, 0)))
    return pl.pallas_call(
        functools.partial(_inproj_kernel, per_head=per_head, n_alias=len(aliased)),
        grid=grid,
        in_specs=in_specs,
        out_specs=out_specs,
        out_shape=out_shape,
        input_output_aliases=aliases,
        scratch_shapes=[pltpu.VMEM((bb, t + SUBLANES, d), F32), pltpu.VMEM((bb, 1, d), F32),
                        pltpu.VMEM((bb * t, d), BF16), pltpu.VMEM((bb * t, Q_RANK), BF16)]
        + ([pltpu.VMEM((bb * t, N_HEADS * V_DIM), F32)] if per_head else []),
        compiler_params=pltpu.CompilerParams(
            dimension_semantics=("parallel", "arbitrary"), vmem_limit_bytes=VMEM_LIMIT_BYTES),
        name="inproj_lru",
    )(x, mod, conv0, h0, cos, sin, *weights, *aliased)


def _attn_heads_kernel(q_ref, k_ref, vt_ref, sgb_ref, bias_ref, yb_ref, m_scr, l_scr, acc_scr, s_scr,
                       *, tk):
    n_groups = q_ref.shape[1]
    q_gran = q_ref.shape[-1]
    n_sub = q_ref.shape[2] * q_gran // tk
    gran = vt_ref.shape[-1]

    m_scr[...] = jnp.full(m_scr.shape, NEG_INF, F32)
    l_scr[...] = jnp.zeros(l_scr.shape, F32)
    acc_scr[...] = jnp.zeros(acc_scr.shape, F32)

    def qk(u, j, g):
        k = k_ref[0, g, pl.ds(pl.multiple_of(j * tk, tk), tk), :]
        q_t = jnp.concatenate([q_ref[0, g, u * (tk // q_gran) + i] for i in range(tk // q_gran)], axis=1)
        return jnp.dot(k, q_t, preferred_element_type=F32)

    def step(u, j, masked, last_tile):
        for g in range(n_groups):
            if g + 1 < n_groups:
                s_scr[g + 1] = qk(u, j, g + 1)
            elif not last_tile:
                s_scr[0] = qk(u, j + 1, 0)
            s = s_scr[g]
            if masked:
                s = s + bias_ref[...]
            m_prev = m_scr[u, g]
            m_next = jnp.maximum(m_prev, jnp.max(s, axis=0, keepdims=True))
            p = jnp.exp2(s - m_next)
            alpha = jnp.exp2(m_prev - m_next)
            l_scr[u, g] = alpha * l_scr[u, g] + jnp.sum(p, axis=0, keepdims=True)
            m_scr[u, g] = m_next
            vt = jnp.concatenate([vt_ref[0, g, j * (tk // gran) + i] for i in range(tk // gran)], axis=1)
            pv = jnp.dot(vt, p.astype(BF16), preferred_element_type=F32)
            acc_scr[u, g] = acc_scr[u, g] * alpha + pv

    s_scr[0] = qk(0, 0, 0)
    for u in range(n_sub):
        last = pl.program_id(2) * n_sub + u

        def pair(jj, c, u=u):
            step(u, 2 * jj, False, False)
            step(u, 2 * jj + 1, False, False)
            return c

        lax.fori_loop(0, last // 2, pair, 0)
        if n_sub % 2 == 0:
            if u % 2 == 1:
                step(u, last - 1, False, False)
        else:
            @pl.when(last % 2 == 1)
            def _(u=u, last=last):
                step(u, last - 1, False, False)
        step(u, last, True, True)
        if u + 1 < n_sub:
            s_scr[0] = qk(u + 1, 0, 0)
        for g in range(n_groups):
            o = (acc_scr[u, g] * (1.0 / l_scr[u, g])).T
            sl = slice(g * V_DIM, (g + 1) * V_DIM)
            yb_ref[0, u * tk:(u + 1) * tk, sl] = (o * sgb_ref[0, u * tk:(u + 1) * tk, sl]).astype(BF16)


def _attention_heads(q, k, vt, sgb, *, tq, tk, heads_per_step=4):
    b, n_heads, q_grans, qw, q_gran = q.shape
    sq = q_grans * q_gran
    sk = k.shape[2]
    d = sgb.shape[-1]
    gran = vt.shape[-1]
    hps = heads_per_step
    n_sub = max(n for n in (4, 2, 1) if sq % (n * tq) == 0)
    tqs = n_sub * tq
    assert n_heads % hps == 0 and sq % tqs == 0 and sk % tk == 0 and tk % gran == 0 and tq % q_gran == 0
    assert vt.shape[2] * gran == sk and (hps * V_DIM) % LANES == 0
    assert tk == tq and tq % CHUNK == 0 and sq == sk
    k_idx = lax.broadcasted_iota(jnp.int32, (tk, tq), 0)
    q_idx = lax.broadcasted_iota(jnp.int32, (tk, tq), 1)
    diag_bias = jnp.where(k_idx // CHUNK <= q_idx // CHUNK, 0.0, NEG_INF).astype(F32)
    kern = functools.partial(_attn_heads_kernel, tk=tk)
    return pl.pallas_call(
        kern,
        grid=(b, n_heads // hps, sq // tqs),
        in_specs=[pl.BlockSpec((1, hps, tqs // q_gran, qw, q_gran), lambda i, h, j: (i, h, j, 0, 0)),
                  pl.BlockSpec((1, hps, sk, qw), lambda i, h, j: (i, h, 0, 0)),
                  pl.BlockSpec((1, hps) + vt.shape[2:], lambda i, h, j: (i, h, 0, 0, 0)),
                  pl.BlockSpec((1, tqs, hps * V_DIM), lambda i, h, j: (i, j, h)),
                  _const_spec((tk, tq))],
        out_specs=pl.BlockSpec((1, tqs, hps * V_DIM), lambda i, h, j: (i, j, h)),
        out_shape=jax.ShapeDtypeStruct((b, sq, d), BF16),
        scratch_shapes=[pltpu.VMEM((n_sub, hps, 1, tq), F32), pltpu.VMEM((n_sub, hps, 1, tq), F32),
                        pltpu.VMEM((n_sub, hps, V_DIM, tq), F32), pltpu.VMEM((hps, tk, tq), F32)],
        compiler_params=pltpu.CompilerParams(
            dimension_semantics=("parallel", "parallel", "arbitrary"),
            vmem_limit_bytes=VMEM_LIMIT_BYTES),
        name="mla_attention_heads",
    )(q, k, vt, sgb, diag_bias)


def _attn_cached_kernel(q_ref, ckv_ref, kpe_ref, knew_ref, sgb_ref, w_uv_ref, yb_ref,
                        m_scr, l_scr, acc_scr, *, past_len, sq, tk):
    m = N_HEADS * sq
    q = q_ref[0].reshape(m, K_CAT)
    q_lat = q[:, 0:KV_RANK]
    q_rope = q[:, KV_RANK:K_CAT].astype(F32)
    q_rope = (q_rope[:, 0:QK_ROPE] + q_rope[:, QK_ROPE:LANES]).astype(BF16)
    q_pos = past_len + (lax.broadcasted_iota(jnp.int32, (m, 1), 0) & (sq - 1))
    k_lim = (q_pos // CHUNK + 1) * CHUNK

    m_scr[...] = jnp.full(m_scr.shape, NEG_INF, F32)
    l_scr[...] = jnp.zeros(l_scr.shape, F32)
    acc_scr[...] = jnp.zeros(acc_scr.shape, F32)
    nt = (((1,), (1,)), ((), ()))

    def update(s, visible, v):
        s = jnp.where(visible, s, NEG_INF)
        m_prev = m_scr[...]
        m_next = jnp.maximum(m_prev, jnp.max(s, axis=-1, keepdims=True))
        p = jnp.exp2(s - jnp.concatenate([m_next] * (s.shape[1] // LANES), axis=1))
        alpha = jnp.exp2(m_prev - m_next)
        l_scr[...] = alpha * l_scr[...] + jnp.sum(p, axis=-1, keepdims=True)
        m_scr[...] = m_next
        pv = jnp.dot(p.astype(BF16), v, preferred_element_type=F32)
        acc_scr[...] = acc_scr[...] * jnp.concatenate([alpha] * (KV_RANK // LANES), axis=1) + pv

    def past_body(j, c):
        off = pl.multiple_of(j * tk, tk)
        lat = ckv_ref[0, 0, pl.ds(off, tk), :].astype(BF16)
        pe = kpe_ref[0, 0, pl.ds(off, tk), :].astype(BF16)
        s = (lax.dot_general(q_lat, lat, nt, preferred_element_type=F32)
             + lax.dot_general(q_rope, pe, nt, preferred_element_type=F32))
        k_pos = off + lax.broadcasted_iota(jnp.int32, (m, tk), 1)
        update(s, k_pos < k_lim, lat)
        return c

    lax.fori_loop(0, past_len // tk, past_body, 0)

    kn = knew_ref[0]
    s = lax.dot_general(q, kn, nt, preferred_element_type=F32)
    idx = lax.broadcasted_iota(jnp.int32, (m, kn.shape[0]), 1)
    update(s, jnp.where(idx < sq, past_len + idx, k_lim) < k_lim, kn[:, 0:KV_RANK])

    inv_l = 1.0 / l_scr[...]
    o = (acc_scr[...] * jnp.concatenate([inv_l] * (KV_RANK // LANES), axis=1)).astype(BF16)
    sgb = sgb_ref[0]
    for hd in range(N_HEADS):
        att = jnp.dot(o[hd * sq:(hd + 1) * sq], w_uv_ref[hd], preferred_element_type=F32)
        sl = slice(hd * V_DIM, (hd + 1) * V_DIM)
        yb_ref[0, :, sl] = (att * sgb[:, sl]).astype(BF16)


def _attention_cached(q, cache_ckv, cache_kpe, layer, knew, sgb, w_uv, *, tk):
    b, _, sq, _ = q.shape
    past_len = cache_ckv.shape[2]
    d = sgb.shape[-1]
    m = N_HEADS * sq
    assert sq & (sq - 1) == 0 and past_len % tk == 0 and tk % LANES == 0 and knew.shape[1] % LANES == 0
    kern = functools.partial(_attn_cached_kernel, past_len=past_len, sq=sq, tk=tk)
    return pl.pallas_call(
        kern,
        grid=(b,),
        in_specs=[pl.BlockSpec((1, N_HEADS, sq, K_CAT), lambda i: (i, 0, 0, 0)),
                  pl.BlockSpec((1, 1, past_len, KV_RANK), lambda i: (layer, i, 0, 0)),
                  pl.BlockSpec((1, 1, past_len, QK_ROPE), lambda i: (layer, i, 0, 0)),
                  pl.BlockSpec((1,) + knew.shape[1:], lambda i: (i, 0, 0)),
                  pl.BlockSpec((1, sq, d), lambda i: (i, 0, 0)),
                  _const_spec(w_uv.shape)],
        out_specs=pl.BlockSpec((1, sq, d), lambda i: (i, 0, 0)),
        out_shape=jax.ShapeDtypeStruct((b, sq, d), BF16),
        scratch_shapes=[pltpu.VMEM((m, LANES), F32), pltpu.VMEM((m, LANES), F32),
                        pltpu.VMEM((m, KV_RANK), F32)],
        compiler_params=pltpu.CompilerParams(
            dimension_semantics=("parallel",), vmem_limit_bytes=VMEM_LIMIT_BYTES),
        name="mla_attention_cached",
    )(q, cache_ckv, cache_kpe, knew, sgb, w_uv)


def _outproj_kernel(x_ref, mod_ref, ya_ref, yb_ref, sua_ref, sub_ref,
                    w_a_ref, w_b_ref, w_o_ref, post_norm_ref, y_ref):
    bb, t, d = x_ref.shape
    m = bb * t
    ya = ya_ref[...].reshape(m, d)
    yb = yb_ref[...].reshape(m, d)
    pa = jnp.dot(ya, w_a_ref[...], preferred_element_type=F32)
    pb = jnp.dot(yb, w_b_ref[...], preferred_element_type=F32)
    merged = sua_ref[...].reshape(m, d).astype(F32) * pa + sub_ref[...].reshape(m, d).astype(F32) * pb
    o = jnp.dot(merged.astype(BF16), w_o_ref[...], preferred_element_type=F32)
    gate = mod_ref[...][:, :, 2 * d:3 * d]
    y_ref[...] = x_ref[...] + gate * _rms(o, post_norm_ref[...]).reshape(bb, t, d)


def _outproj(x, mod, ya, yb, sua, sub, lw, *, bb, t):
    b, s, d = x.shape
    tok = pl.BlockSpec((bb, t, d), lambda i, j: (i, j, 0))
    weights = [lw["w_branch_a"], lw["w_branch_b"], lw["w_out"], lw["post_norm"]]
    return pl.pallas_call(
        _outproj_kernel,
        grid=(b // bb, s // t),
        in_specs=[tok, pl.BlockSpec((bb, 1, 3 * d), lambda i, j: (i, 0, 0)), tok, tok, tok, tok]
        + [_const_spec(w.shape) for w in weights],
        out_specs=tok,
        out_shape=jax.ShapeDtypeStruct((b, s, d), F32),
        compiler_params=pltpu.CompilerParams(
            dimension_semantics=("parallel", "parallel"), vmem_limit_bytes=VMEM_LIMIT_BYTES),
        name="outproj",
    )(x, mod, ya, yb, sua, sub, *weights)


def _rot_half(w):
    half = w.shape[-1] // 2
    return jnp.concatenate([-w[..., half:], w[..., :half]], axis=-1)


def _twice(w):
    return jnp.concatenate([w, w], axis=-1)


def _layer_weights(l, p):
    d = p["w_in"].shape[1]
    w_in = p["w_in"][l].astype(BF16)
    o = 0
    seg = {}
    for name, width in (("xa", d), ("ga", d), ("cq", Q_RANK), ("ckv", KV_RANK), ("kpe", QK_ROPE),
                        ("gb", d), ("ua", d), ("ub", d)):
        seg[name] = w_in[:, o:o + width]
        o += width
    w_in_ext = jnp.concatenate(
        [seg["xa"], seg["ga"], seg["cq"], seg["ckv"], _twice(seg["kpe"]),
         _twice(_rot_half(seg["kpe"])), seg["gb"], seg["ua"], seg["ub"]], axis=1)
    wq = p["w_q_up"][l].reshape(Q_RANK, N_HEADS, QK_NOPE + QK_ROPE)
    wq_pe = wq[:, :, QK_NOPE:]
    w_q = jnp.concatenate(
        [wq[:, :, :QK_NOPE].reshape(Q_RANK, -1), wq_pe.reshape(Q_RANK, -1)], axis=1)
    row = lambda v: v.reshape(1, -1)
    return {
        "w_in": w_in_ext,
        "w_q": w_q.astype(BF16),
        "w_uk": jnp.transpose(p["w_uk"][l], (1, 2, 0)).astype(BF16),
        "w_ukv": jnp.concatenate([p["w_uk"][l].reshape(KV_RANK, -1), p["w_uv"][l].reshape(KV_RANK, -1)],
                                 axis=1).astype(BF16),
        "w_uv": jnp.transpose(p["w_uv"][l], (1, 0, 2)).astype(BF16),
        "w_gate": jnp.concatenate([p["lru_wa"][l], p["lru_wx"][l]], axis=-1).astype(BF16),
        "pre_norm": row(p["pre_norm"][l]), "post_norm": row(p["post_norm"][l]),
        "conv_w": p["conv_w"][l], "conv_b": row(p["conv_b"][l]),
        "lru_ba": row(p["lru_ba"][l]), "lru_bx": row(p["lru_bx"][l]),
        "lru_lambda": row(p["lru_lambda"][l]),
        "q_norm": row(p["q_norm"][l]) * Q_SCALE, "kv_norm": row(p["kv_norm"][l]),
        "w_branch_a": p["w_branch_a"][l].astype(BF16), "w_branch_b": p["w_branch_b"][l].astype(BF16),
        "w_out": p["w_out"][l].astype(BF16),
    }


def _rope_tables(pos):
    half = QK_ROPE // 2
    inv = ROPE_THETA ** (-jnp.arange(half, dtype=F32) / half)
    ang = pos.astype(F32)[:, None] * inv[None, :]
    cos, sin = jnp.cos(ang), jnp.sin(ang)
    return jnp.concatenate([cos] * (LANES // half), axis=1), jnp.concatenate([sin] * (LANES // half), axis=1)


def _pick_tile(n, target):
    t = min(n, target)
    while n % t:
        t //= 2
    return t


def kernel(x_prompt, x_sample, c_prompt, c_sample, cache_ckv, cache_kpe, state_conv, state_lru, ada_w, ada_b, pre_norm, post_norm, w_in, conv_w, conv_b, lru_wa, lru_ba, lru_wx, lru_bx, lru_lambda, q_norm, w_q_up, kv_norm, w_uk, w_uv, w_branch_a, w_branch_b, w_out):
    p = dict(ada_w=ada_w, ada_b=ada_b, pre_norm=pre_norm, post_norm=post_norm, w_in=w_in,
             conv_w=conv_w, conv_b=conv_b, lru_wa=lru_wa, lru_ba=lru_ba, lru_wx=lru_wx, lru_bx=lru_bx,
             lru_lambda=lru_lambda, q_norm=q_norm, w_q_up=w_q_up, kv_norm=kv_norm, w_uk=w_uk,
             w_uv=w_uv, w_branch_a=w_branch_a, w_branch_b=w_branch_b, w_out=w_out)
    depth = w_in.shape[0]
    b_p, s_p, d = x_prompt.shape
    b_s, s_s, _ = x_sample.shape
    past_len = cache_ckv.shape[2]
    assert s_p % SUBLANES == 0 and s_s % SUBLANES == 0 and s_p >= SUBLANES and s_s >= SUBLANES

    mods = _modulation(jnp.concatenate([c_prompt, c_sample], axis=0), ada_w, ada_b)
    cos_p, sin_p = _rope_tables(jnp.arange(s_p, dtype=jnp.int32))
    cos_s, sin_s = _rope_tables(past_len + jnp.arange(s_s, dtype=jnp.int32))
    conv0_p = jnp.zeros((b_p, SUBLANES, d), F32)
    h0_p = jnp.zeros((b_p, 1, d), F32)

    t_in_p = _pick_tile(s_p, T_INPROJ)
    t_out_p = _pick_tile(s_p, T_OUTPROJ)
    tq_p = tk_p = _pick_tile(s_p, T_ATTN)
    tk_s = _pick_tile(past_len, T_CACHED)

    yp, ys = x_prompt, x_sample
    outs_p, outs_s = [], []
    lat_p = lat_s = None
    for l in range(depth):
        lw = _layer_weights(l, p)

        mod_p = mods[l, :b_p, None, :]
        ya, sgb, sua, sub, q, ckv, kpe, k_heads, convo, hlast, vt = _inproj(
            yp, mod_p, conv0_p, h0_p, cos_p, sin_p, lw, bb=1, t=t_in_p, per_head=True,
            layer=l, depth=depth, stacked=lat_p)
        lat_p = (ckv, kpe)
        yb = _attention_heads(q, k_heads, vt, sgb, tq=tq_p, tk=tk_p)
        yp = _outproj(yp, mod_p, ya, yb, sua, sub, lw, bb=1, t=t_out_p)
        outs_p.append((convo[:, SUBLANES - (CONV_W - 1):, :], hlast[:, 0, :]))

        mod_s = mods[l, b_p:, None, :]
        conv0_s = jnp.pad(state_conv[l], ((0, 0), (SUBLANES - (CONV_W - 1), 0), (0, 0)))
        ya, sgb, sua, sub, q, ckv, kpe, kcat, convo, hlast = _inproj(
            ys, mod_s, conv0_s, state_lru[l][:, None, :], cos_s, sin_s, lw, bb=b_s, t=s_s,
            per_head=False, layer=l, depth=depth, stacked=lat_s)
        lat_s = (ckv, kpe)
        knew = jnp.pad(kcat, ((0, 0), (0, -s_s % LANES), (0, 0)))
        yb = _attention_cached(q, cache_ckv, cache_kpe, l, knew, sgb, lw["w_uv"], tk=tk_s)
        ys = _outproj(ys, mod_s, ya, yb, sua, sub, lw, bb=b_s, t=s_s)
        outs_s.append((convo[:, SUBLANES - (CONV_W - 1):, :], hlast[:, 0, :]))

    stack = lambda outs, k: jnp.stack([o[k] for o in outs])
    return (yp, ys,
            lat_p[0], lat_p[1], stack(outs_p, 0), stack(outs_p, 1),
            lat_s[0], lat_s[1], stack(outs_s, 0), stack(outs_s, 1))
```
